```python
import math
import jax, jax.numpy as jnp
from jax import lax
import numpy as np


D_MODEL = 2048
BATCH = 1
SEQ = 8192
DEPTH = 1

CHUNK = 64
S5_WIDTH = 1024
S5_GROUP_WIDTH = 16
S5_GROUPS = S5_WIDTH // S5_GROUP_WIDTH
S5_STATE = 64
SGU_WIDTH = 1024
SGU_HEADS = 8
SGU_HEAD_DIM = SGU_WIDTH // SGU_HEADS
MLP_CHUNK = 128
MIX_IN = S5_WIDTH + 2 * SGU_WIDTH
D_FF = 5632
NORM_EPS = 1e-6
DT_MIN = 1e-3
DT_MAX = 1e-1

kernel_name = 'hybrid_s5_sgu_macaron_block'


def rms_norm(x, g):
    xf = x.astype(jnp.float32)
    y = xf * lax.rsqrt(jnp.mean(xf * xf, axis=-1, keepdims=True) + NORM_EPS)
    return (y * g.astype(jnp.float32)).astype(x.dtype)


def layer_norm(x, g, b):
    xf = x.astype(jnp.float32)
    mu = jnp.mean(xf, axis=-1, keepdims=True)
    var = jnp.mean(jnp.square(xf - mu), axis=-1, keepdims=True)
    y = (xf - mu) * lax.rsqrt(var + NORM_EPS)
    return (y * g.astype(jnp.float32) + b.astype(jnp.float32)).astype(x.dtype)


def swiglu_ffn(h, w_gate, w_up, w_down):
    return (jax.nn.silu(h @ w_gate) * (h @ w_up)) @ w_down


def _complex_affine_combine(e1, e2):
    a1r, a1i, b1r, b1i = e1
    a2r, a2i, b2r, b2i = e2
    ar = a2r * a1r - a2i * a1i
    ai = a2r * a1i + a2i * a1r
    br = a2r * b1r - a2i * b1i + b2r
    bi = a2r * b1i + a2i * b1r + b2i
    return (ar, ai, br, bi)


def s5_mixer(u, a_re, a_im, log_dt, b_re, b_im, c_re, c_im, d_skip, w_glu, b_glu):
    bsz, seq, _ = u.shape
    uf = u.astype(jnp.float32).reshape(bsz, seq, S5_GROUPS, S5_GROUP_WIDTH)
    lam_re = a_re.astype(jnp.float32)
    lam_im = a_im.astype(jnp.float32)
    dt = jnp.exp(log_dt.astype(jnp.float32))[:, None]
    decay = jnp.exp(lam_re * dt)
    abar_re = decay * jnp.cos(lam_im * dt)
    abar_im = decay * jnp.sin(lam_im * dt)
    denom = lam_re * lam_re + lam_im * lam_im
    num_re = abar_re - 1.0
    num_im = abar_im
    k_re = (num_re * lam_re + num_im * lam_im) / denom
    k_im = (num_im * lam_re - num_re * lam_im) / denom
    bu_re = jnp.einsum('blgc,gpc->blgp', uf, b_re.astype(jnp.float32))
    bu_im = jnp.einsum('blgc,gpc->blgp', uf, b_im.astype(jnp.float32))
    in_re = k_re * bu_re - k_im * bu_im
    in_im = k_re * bu_im + k_im * bu_re
    a_r = jnp.broadcast_to(abar_re, in_re.shape)
    a_i = jnp.broadcast_to(abar_im, in_re.shape)
    _, _, x_re, x_im = lax.associative_scan(
        _complex_affine_combine, (a_r, a_i, in_re, in_im), axis=1)
    y = (jnp.einsum('blgp,gcp->blgc', x_re, c_re.astype(jnp.float32))
         - jnp.einsum('blgp,gcp->blgc', x_im, c_im.astype(jnp.float32))
         + d_skip.astype(jnp.float32) * uf)
    y = jax.nn.gelu(y.reshape(bsz, seq, S5_WIDTH)).astype(u.dtype)
    return y * jax.nn.sigmoid(y @ w_glu + b_glu)


def sgu_mixer(uv, ln_g, ln_b, w_s, b_s):
    u, v = jnp.split(jax.nn.gelu(uv), 2, axis=-1)
    v = layer_norm(v, ln_g, ln_b)
    bsz, seq, _ = v.shape
    n_chunks = seq // MLP_CHUNK
    v = v.reshape(bsz, n_chunks, MLP_CHUNK, SGU_HEADS, SGU_HEAD_DIM)
    blk = jnp.arange(MLP_CHUNK) // CHUNK
    mask = blk[:, None] >= blk[None, :]
    ws = jnp.where(mask[None], w_s, jnp.zeros((), w_s.dtype))
    mixed = jnp.einsum('hts,bnshc->bnthc', ws, v) + jnp.transpose(b_s)[:, :, None]
    return u * mixed.reshape(bsz, seq, SGU_WIDTH)


def setup_inputs(seed: int = 0) -> dict:
    key = jax.random.key(seed)
    keys = iter(jax.random.split(key, 40))
    L = DEPTH
    D = D_MODEL

    def nrm(shape, scale):
        return scale * jax.random.normal(next(keys), shape, jnp.float32)

    x = nrm((BATCH, SEQ, D), 1.0)
    ffn1_norm = 1.0 + nrm((L, D), 0.02)
    ffn1_w_gate = nrm((L, D, D_FF), D ** -0.5)
    ffn1_w_up = nrm((L, D, D_FF), D ** -0.5)
    ffn1_w_down = nrm((L, D_FF, D), D_FF ** -0.5)
    mix_norm = 1.0 + nrm((L, D), 0.02)
    w_in = nrm((L, D, MIX_IN), D ** -0.5)
    n_idx = jnp.arange(S5_STATE, dtype=jnp.float32)
    s5_a_re = -0.5 + nrm((L, S5_GROUPS, S5_STATE), 0.01)
    s5_a_im = math.pi * n_idx + nrm((L, S5_GROUPS, S5_STATE), 0.01)
    s5_log_dt = math.log(DT_MIN) + jax.random.uniform(
        next(keys), (L, S5_GROUPS), jnp.float32) * (math.log(DT_MAX) - math.log(DT_MIN))
    s5_b_re = nrm((L, S5_GROUPS, S5_STATE, S5_GROUP_WIDTH), (2 * S5_GROUP_WIDTH) ** -0.5)
    s5_b_im = nrm((L, S5_GROUPS, S5_STATE, S5_GROUP_WIDTH), (2 * S5_GROUP_WIDTH) ** -0.5)
    s5_c_re = nrm((L, S5_GROUPS, S5_GROUP_WIDTH, S5_STATE), S5_STATE ** -0.5)
    s5_c_im = nrm((L, S5_GROUPS, S5_GROUP_WIDTH, S5_STATE), S5_STATE ** -0.5)
    s5_d = nrm((L, S5_GROUPS, S5_GROUP_WIDTH), 1.0)
    s5_w_glu = nrm((L, S5_WIDTH, S5_WIDTH), S5_WIDTH ** -0.5)
    s5_b_glu = nrm((L, S5_WIDTH), 0.01)
    sgu_ln_g = 1.0 + nrm((L, SGU_WIDTH), 0.02)
    sgu_ln_b = nrm((L, SGU_WIDTH), 0.01)
    sgu_w_s = nrm((L, SGU_HEADS, MLP_CHUNK, MLP_CHUNK), 0.05)
    sgu_b_s = 1.0 + nrm((L, SGU_HEADS, MLP_CHUNK), 0.05)
    w_branch_a = nrm((L, S5_WIDTH, D), S5_WIDTH ** -0.5)
    w_branch_b = nrm((L, SGU_WIDTH, D), SGU_WIDTH ** -0.5)
    w_gate = nrm((L, D, 2 * D), D ** -0.5)
    b_gate = nrm((L, 2 * D), 0.01)
    w_out = nrm((L, D, D), D ** -0.5)
    ffn2_norm = 1.0 + nrm((L, D), 0.02)
    ffn2_w_gate = nrm((L, D, D_FF), D ** -0.5)
    ffn2_w_up = nrm((L, D, D_FF), D ** -0.5)
    ffn2_w_down = nrm((L, D_FF, D), D_FF ** -0.5)
    final_norm = 1.0 + nrm((D,), 0.02)
    return {
        'x': x,
        'ffn1_norm': ffn1_norm, 'ffn1_w_gate': ffn1_w_gate, 'ffn1_w_up': ffn1_w_up,
        'ffn1_w_down': ffn1_w_down,
        'mix_norm': mix_norm, 'w_in': w_in,
        's5_a_re': s5_a_re, 's5_a_im': s5_a_im, 's5_log_dt': s5_log_dt,
        's5_b_re': s5_b_re, 's5_b_im': s5_b_im, 's5_c_re': s5_c_re, 's5_c_im': s5_c_im,
        's5_d': s5_d, 's5_w_glu': s5_w_glu, 's5_b_glu': s5_b_glu,
        'sgu_ln_g': sgu_ln_g, 'sgu_ln_b': sgu_ln_b, 'sgu_w_s': sgu_w_s, 'sgu_b_s': sgu_b_s,
        'w_branch_a': w_branch_a, 'w_branch_b': w_branch_b,
        'w_gate': w_gate, 'b_gate': b_gate, 'w_out': w_out,
        'ffn2_norm': ffn2_norm, 'ffn2_w_gate': ffn2_w_gate, 'ffn2_w_up': ffn2_w_up,
        'ffn2_w_down': ffn2_w_down,
        'final_norm': final_norm,
    }


def reference(x, ffn1_norm, ffn1_w_gate, ffn1_w_up, ffn1_w_down, mix_norm, w_in,
              s5_a_re, s5_a_im, s5_log_dt, s5_b_re, s5_b_im, s5_c_re, s5_c_im, s5_d,
              s5_w_glu, s5_b_glu, sgu_ln_g, sgu_ln_b, sgu_w_s, sgu_b_s,
              w_branch_a, w_branch_b, w_gate, b_gate, w_out,
              ffn2_norm, ffn2_w_gate, ffn2_w_up, ffn2_w_down, final_norm):
    for i in range(DEPTH):
        h = rms_norm(x, ffn1_norm[i])
        x = x + 0.5 * swiglu_ffn(h, ffn1_w_gate[i], ffn1_w_up[i], ffn1_w_down[i])
        h = rms_norm(x, mix_norm[i])
        proj = h @ w_in[i]
        u_a = proj[..., :S5_WIDTH]
        uv_b = proj[..., S5_WIDTH:]
        y_a = s5_mixer(u_a, s5_a_re[i], s5_a_im[i], s5_log_dt[i], s5_b_re[i], s5_b_im[i],
                       s5_c_re[i], s5_c_im[i], s5_d[i], s5_w_glu[i], s5_b_glu[i])
        y_b = sgu_mixer(uv_b, sgu_ln_g[i], sgu_ln_b[i], sgu_w_s[i], sgu_b_s[i])
        gates = jax.nn.sigmoid(h @ w_gate[i] + b_gate[i])
        g_a, g_b = jnp.split(gates, 2, axis=-1)
        merged = g_a * (y_a @ w_branch_a[i]) + g_b * (y_b @ w_branch_b[i])
        x = x + merged @ w_out[i]
        h = rms_norm(x, ffn2_norm[i])
        x = x + 0.5 * swiglu_ffn(h, ffn2_w_gate[i], ffn2_w_up[i], ffn2_w_down[i])
    return rms_norm(x, final_norm)
```

```python
import functools
import math

import jax
import jax.numpy as jnp
from jax import lax
from jax.experimental import pallas as pl
from jax.experimental.pallas import tpu as pltpu

F32 = jnp.float32
BF16 = jnp.bfloat16

NORM_EPS = 1e-6
S5_GROUP_WIDTH = 16
S5_STATE = 64
SGU_HEADS = 8
SGU_CHUNK = 128
SGU_CAUSAL_BLOCK = 64
S5_CHUNK = 16
LANE = 128
GROUPS_PER_BLOCK = LANE // S5_GROUP_WIDTH
VMEM_LIMIT = 56 * 1024 * 1024


def _rms(x, g):
    ms = jnp.mean(x * x, axis=-1, keepdims=True)
    return (x * lax.rsqrt(ms + NORM_EPS)) * g


def _dot(a, b):
    return jnp.dot(a, b, preferred_element_type=F32)


def _ffn_kernel(x_ref, g_ref, wg_ref, wu_ref, wd_ref, fin_ref, o_ref, h_ref, acc_ref, *, final):
    j = pl.program_id(1)

    @pl.when(j == 0)
    def _():
        h_ref[...] = _rms(x_ref[...], g_ref[...]).astype(BF16)
        acc_ref[...] = jnp.zeros_like(acc_ref)

    h = h_ref[...]
    a = _dot(h, wg_ref[...])
    b = _dot(h, wu_ref[...])
    hid = (a * jax.nn.sigmoid(a)) * b
    acc_ref[...] += _dot(hid.astype(BF16), wd_ref[...])

    @pl.when(j == pl.num_programs(1) - 1)
    def _():
        y = x_ref[...] + 0.5 * acc_ref[...]
        if final:
            y = _rms(y, fin_ref[...])
        o_ref[...] = y


def _ffn(x, g, wg, wu, wd, fin, *, final, tm=512, tf=512):
    n, d = x.shape
    dff = wg.shape[1]
    return pl.pallas_call(
        functools.partial(_ffn_kernel, final=final),
        grid=(n // tm, dff // tf),
        in_specs=[
            pl.BlockSpec((tm, d), lambda i, j: (i, 0)),
            pl.BlockSpec((1, d), lambda i, j: (0, 0)),
            pl.BlockSpec((d, tf), lambda i, j: (0, j)),
            pl.BlockSpec((d, tf), lambda i, j: (0, j)),
            pl.BlockSpec((tf, d), lambda i, j: (j, 0)),
            pl.BlockSpec((1, d), lambda i, j: (0, 0)),
        ],
        out_specs=pl.BlockSpec((tm, d), lambda i, j: (i, 0)),
        out_shape=jax.ShapeDtypeStruct((n, d), F32),
        scratch_shapes=[pltpu.VMEM((tm, d), BF16), pltpu.VMEM((tm, d), F32)],
        compiler_params=pltpu.CompilerParams(
            dimension_semantics=("parallel", "arbitrary"), vmem_limit_bytes=VMEM_LIMIT),
        name="ffn_final" if final else "ffn",
    )(x, g, wg, wu, wd, fin)


def _mix_in_kernel(x_ref, g_ref, w_ref, lng_ref, lnb_ref, ws_ref, bs_ref,
                   ua_ref, h_ref, yb_ref, uv_ref, *, tn, sw):
    j = pl.program_id(1)
    n_a = sw // tn

    @pl.when(j == 0)
    def _():
        h_ref[...] = _rms(x_ref[...], g_ref[...]).astype(BF16)

    p = _dot(h_ref[...], w_ref[...])

    @pl.when(j < n_a)
    def _():
        ua_ref[...] = p

    for jj in range(n_a, 3 * n_a):
        @pl.when(j == jj)
        def _(jj=jj):
            uv_ref[:, (jj - n_a) * tn:(jj - n_a + 1) * tn] = p

    @pl.when(j == 3 * n_a - 1)
    def _():
        tm = uv_ref.shape[0]
        gl = jax.nn.gelu(uv_ref[...])
        u = gl[:, :sw]
        v = gl[:, sw:]
        mu = jnp.mean(v, axis=-1, keepdims=True)
        vc = v - mu
        var = jnp.mean(vc * vc, axis=-1, keepdims=True)
        vn = ((vc * lax.rsqrt(var + NORM_EPS)) * lng_ref[...] + lnb_ref[...]).astype(BF16)
        r = lax.broadcasted_iota(jnp.int32, (SGU_CHUNK, SGU_CHUNK), 0) // SGU_CAUSAL_BLOCK
        c = lax.broadcasted_iota(jnp.int32, (SGU_CHUNK, SGU_CHUNK), 1) // SGU_CAUSAL_BLOCK
        keep = r >= c
        hd = sw // SGU_HEADS
        for h in range(SGU_HEADS):
            wsm = jnp.where(keep, ws_ref[h], 0.0).astype(BF16)
            for q in range(tm // SGU_CHUNK):
                rows = slice(q * SGU_CHUNK, (q + 1) * SGU_CHUNK)
                cols = slice(h * hd, (h + 1) * hd)
                mixed = _dot(wsm, vn[rows, cols]) + bs_ref[:, cols]
                yb_ref[rows, cols] = (u[rows, cols] * mixed).astype(BF16)


def _mix_in(x, g, w_in, ln_g, ln_b, w_s, bsb, *, sw, tm=512, tn=512):
    n, d = x.shape
    n_a = sw // tn
    return pl.pallas_call(
        functools.partial(_mix_in_kernel, tn=tn, sw=sw),
        grid=(n // tm, 3 * n_a),
        in_specs=[
            pl.BlockSpec((tm, d), lambda i, j: (i, 0)),
            pl.BlockSpec((1, d), lambda i, j: (0, 0)),
            pl.BlockSpec((d, tn), lambda i, j: (0, j)),
            pl.BlockSpec((1, sw), lambda i, j: (0, 0)),
            pl.BlockSpec((1, sw), lambda i, j: (0, 0)),
            pl.BlockSpec(w_s.shape, lambda i, j: (0, 0, 0)),
            pl.BlockSpec(bsb.shape, lambda i, j: (0, 0)),
        ],
        out_specs=[
            pl.BlockSpec((tm, tn), lambda i, j: (i, jnp.minimum(j, n_a - 1))),
            pl.BlockSpec((tm, d), lambda i, j: (i, 0)),
            pl.BlockSpec((tm, sw), lambda i, j: (i, 0)),
        ],
        out_shape=[
            jax.ShapeDtypeStruct((n, sw), F32),
            jax.ShapeDtypeStruct((n, d), BF16),
            jax.ShapeDtypeStruct((n, sw), BF16),
        ],
        scratch_shapes=[pltpu.VMEM((tm, 2 * sw), F32)],
        compiler_params=pltpu.CompilerParams(
            dimension_semantics=("parallel", "arbitrary"), vmem_limit_bytes=VMEM_LIMIT),
        name="mix_in",
    )(x, g, w_in, ln_g, ln_b, w_s, bsb)


def _s5_kernel(u_ref, wt_ref, wb_ref, wc_ref, a_ref, d_ref, o_ref,
               ut_ref, vt_ref, v_ref, xs_ref, xt_ref, yt_ref, yn_ref):
    nc = u_ref.shape[0] // S5_CHUNK
    gw = S5_GROUP_WIDTH
    ns = GROUPS_PER_BLOCK * S5_STATE

    for s in range(S5_CHUNK):
        piece = u_ref[pl.ds(s, nc, stride=S5_CHUNK), :]
        ut_ref[s] = piece.T.astype(BF16)

    def chunk_inputs(g):
        return jnp.concatenate(
            [ut_ref[s, g * gw:(g + 1) * gw, :] for s in range(S5_CHUNK)], axis=0)

    for g in range(GROUPS_PER_BLOCK):
        vg = _dot(wb_ref[g], chunk_inputs(g))
        vt_ref[g * S5_STATE:(g + 1) * S5_STATE, :] = vg[:S5_STATE]
        vt_ref[ns + g * S5_STATE:ns + (g + 1) * S5_STATE, :] = vg[S5_STATE:]
    v_ref[...] = vt_ref[...].T

    ar = a_ref[0:1, :]
    ai = a_ref[1:2, :]

    def step(c, carry):
        xr, xi = carry
        xs_ref[pl.ds(c, 1), 0:ns] = xr
        xs_ref[pl.ds(c, 1), ns:2 * ns] = xi
        vr = v_ref[pl.ds(c, 1), 0:ns]
        vi = v_ref[pl.ds(c, 1), ns:2 * ns]
        return ar * xr - ai * xi + vr, ar * xi + ai * xr + vi

    zero = jnp.zeros((1, ns), F32)
    lax.fori_loop(0, nc, step, (zero, zero), unroll=8)
    xt_ref[...] = xs_ref[...].T.astype(BF16)

    for g in range(GROUPS_PER_BLOCK):
        xg = jnp.concatenate(
            [xt_ref[g * S5_STATE:(g + 1) * S5_STATE, :],
             xt_ref[ns + g * S5_STATE:ns + (g + 1) * S5_STATE, :]], axis=0)
        yg = _dot(wt_ref[g], chunk_inputs(g)) + _dot(wc_ref[g], xg)
        for t in range(S5_CHUNK):
            yt_ref[t, g * gw:(g + 1) * gw, :] = yg[t * gw:(t + 1) * gw, :]

    for t in range(S5_CHUNK):
        up = u_ref[pl.ds(t, nc, stride=S5_CHUNK), :]
        y = yt_ref[t].T + d_ref[...] * up
        yn_ref[pl.ds(t, nc, stride=S5_CHUNK), :] = jax.nn.gelu(y)
    o_ref[...] = yn_ref[...].astype(BF16)


def _s5_core(u, wt, wb, wc, a16, dsk):
    n, sw = u.shape
    nb = sw // LANE
    nc = n // S5_CHUNK
    gb = GROUPS_PER_BLOCK
    ns2 = 2 * gb * S5_STATE
    return pl.pallas_call(
        _s5_kernel,
        grid=(nb,),
        in_specs=[
            pl.BlockSpec((n, LANE), lambda j: (0, j)),
            pl.BlockSpec((gb,) + wt.shape[1:], lambda j: (j, 0, 0)),
            pl.BlockSpec((gb,) + wb.shape[1:], lambda j: (j, 0, 0)),
            pl.BlockSpec((gb,) + wc.shape[1:], lambda j: (j, 0, 0)),
            pl.BlockSpec((None, 2, ns2 // 2), lambda j: (j, 0, 0)),
            pl.BlockSpec((1, LANE), lambda j: (0, j)),
        ],
        out_specs=pl.BlockSpec((n, LANE), lambda j: (0, j)),
        out_shape=jax.ShapeDtypeStruct((n, sw), BF16),
        scratch_shapes=[
            pltpu.VMEM((S5_CHUNK, LANE, nc), BF16),
            pltpu.VMEM((ns2, nc), F32),
            pltpu.VMEM((nc, ns2), F32),
            pltpu.VMEM((nc, ns2), F32),
            pltpu.VMEM((ns2, nc), BF16),
            pltpu.VMEM((S5_CHUNK, LANE, nc), F32),
            pltpu.VMEM((n, LANE), F32),
        ],
        compiler_params=pltpu.CompilerParams(
            dimension_semantics=("parallel",), vmem_limit_bytes=VMEM_LIMIT),
        name="s5_core",
    )(u, wt, wb, wc, a16, dsk)


def _s5_weights(a_re, a_im, log_dt, b_re, b_im, c_re, c_im, d_skip):
    g, p = a_re.shape
    tc = S5_CHUNK
    dt = jnp.exp(log_dt)[:, None]
    decay = jnp.exp(a_re * dt)
    ab_re = decay * jnp.cos(a_im * dt)
    ab_im = decay * jnp.sin(a_im * dt)
    denom = a_re * a_re + a_im * a_im
    num_re = ab_re - 1.0
    num_im = ab_im
    k_re = (num_re * a_re + num_im * a_im) / denom
    k_im = (num_im * a_re - num_re * a_im) / denom
    bp_re = k_re[..., None] * b_re - k_im[..., None] * b_im
    bp_im = k_re[..., None] * b_im + k_im[..., None] * b_re

    def cmul(xr, xi, yr, yi):
        return xr * yr - xi * yi, xr * yi + xi * yr

    pw_re, pw_im = [jnp.ones_like(ab_re)], [jnp.zeros_like(ab_im)]
    for _ in range(tc):
        nr, ni = cmul(pw_re[-1], pw_im[-1], ab_re, ab_im)
        pw_re.append(nr)
        pw_im.append(ni)
    pw_re = jnp.stack(pw_re)
    pw_im = jnp.stack(pw_im)

    ab_r = pw_re[:tc, :, :, None] * bp_re[None] - pw_im[:tc, :, :, None] * bp_im[None]
    ab_i = pw_re[:tc, :, :, None] * bp_im[None] + pw_im[:tc, :, :, None] * bp_re[None]
    hp = lax.Precision.HIGHEST
    h = (jnp.einsum('gcp,jgpd->jgcd', c_re, ab_r, precision=hp)
         - jnp.einsum('gcp,jgpd->jgcd', c_im, ab_i, precision=hp))
    lag = jnp.arange(tc)[:, None] - jnp.arange(tc)[None, :]
    ht = jnp.where((lag >= 0)[:, :, None, None, None], h[jnp.clip(lag, 0, tc - 1)], 0.0)
    wt = jnp.transpose(ht, (2, 0, 3, 1, 4)).reshape(g, tc * S5_GROUP_WIDTH, tc * S5_GROUP_WIDTH)
    wb_r = jnp.transpose(ab_r[::-1], (1, 2, 0, 3)).reshape(g, p, tc * S5_GROUP_WIDTH)
    wb_i = jnp.transpose(ab_i[::-1], (1, 2, 0, 3)).reshape(g, p, tc * S5_GROUP_WIDTH)
    wb = jnp.concatenate([wb_r, wb_i], axis=1)
    ca_r = c_re[None] * pw_re[1:, :, None, :] - c_im[None] * pw_im[1:, :, None, :]
    ca_i = c_re[None] * pw_im[1:, :, None, :] + c_im[None] * pw_re[1:, :, None, :]
    wc = jnp.concatenate([ca_r, -ca_i], axis=-1)
    wc = jnp.transpose(wc, (1, 0, 2, 3)).reshape(g, tc * S5_GROUP_WIDTH, 2 * p)
    gb = GROUPS_PER_BLOCK
    a16 = jnp.stack([pw_re[tc].reshape(g // gb, gb * p), pw_im[tc].reshape(g // gb, gb * p)], axis=1)
    return wt.astype(BF16), wb.astype(BF16), wc.astype(BF16), a16, d_skip.reshape(1, -1)


def _merge_kernel(x_ref, h_ref, yap_ref, yb_ref, wglu_ref, bglu_ref, wga_ref, wgb_ref,
                  bga_ref, bgb_ref, wa_ref, wb_ref, wo_ref, o_ref, ya_ref, acc_ref):
    j = pl.program_id(1)

    @pl.when(j == 0)
    def _():
        yp = yap_ref[...]
        z = _dot(yp, wglu_ref[...]) + bglu_ref[...]
        ya_ref[...] = (yp.astype(F32) * jax.nn.sigmoid(z)).astype(BF16)
        acc_ref[...] = jnp.zeros_like(acc_ref)

    h = h_ref[...]
    ga = jax.nn.sigmoid(_dot(h, wga_ref[...]) + bga_ref[...])
    gb = jax.nn.sigmoid(_dot(h, wgb_ref[...]) + bgb_ref[...])
    pa = _dot(ya_ref[...], wa_ref[...])
    pb = _dot(yb_ref[...], wb_ref[...])
    m = (ga * pa + gb * pb).astype(BF16)
    acc_ref[...] += _dot(m, wo_ref[...])

    @pl.when(j == pl.num_programs(1) - 1)
    def _():
        o_ref[...] = x_ref[...] + acc_ref[...]


def _merge(x, h, yap, yb, wglu, bglu, wgate, bgate, wa, wb, wo, *, tm=512, tn=256):
    n, d = x.shape
    sw = yap.shape[1]
    nj = d // tn
    return pl.pallas_call(
        _merge_kernel,
        grid=(n // tm, nj),
        in_specs=[
            pl.BlockSpec((tm, d), lambda i, j: (i, 0)),
            pl.BlockSpec((tm, d), lambda i, j: (i, 0)),
            pl.BlockSpec((tm, sw), lambda i, j: (i, 0)),
            pl.BlockSpec((tm, sw), lambda i, j: (i, 0)),
            pl.BlockSpec((sw, sw), lambda i, j: (0, 0)),
            pl.BlockSpec((1, sw), lambda i, j: (0, 0)),
            pl.BlockSpec((d, tn), lambda i, j: (0, j)),
            pl.BlockSpec((d, tn), lambda i, j: (0, j + nj)),
            pl.BlockSpec((1, tn), lambda i, j: (0, j)),
            pl.BlockSpec((1, tn), lambda i, j: (0, j + nj)),
            pl.BlockSpec((sw, tn), lambda i, j: (0, j)),
            pl.BlockSpec((sw, tn), lambda i, j: (0, j)),
            pl.BlockSpec((tn, d), lambda i, j: (j, 0)),
        ],
        out_specs=pl.BlockSpec((tm, d), lambda i, j: (i, 0)),
        out_shape=jax.ShapeDtypeStruct((n, d), F32),
        scratch_shapes=[pltpu.VMEM((tm, sw), BF16), pltpu.VMEM((tm, d), F32)],
        compiler_params=pltpu.CompilerParams(
            dimension_semantics=("parallel", "arbitrary"), vmem_limit_bytes=VMEM_LIMIT),
        name="merge",
    )(x, h, yap, yb, wglu, bglu, wgate, wgate, bgate, bgate, wa, wb, wo)


def kernel(x, ffn1_norm, ffn1_w_gate, ffn1_w_up, ffn1_w_down, mix_norm, w_in, s5_a_re, s5_a_im, s5_log_dt, s5_b_re, s5_b_im, s5_c_re, s5_c_im, s5_d, s5_w_glu, s5_b_glu, sgu_ln_g, sgu_ln_b, sgu_w_s, sgu_b_s, w_branch_a, w_branch_b, w_gate, b_gate, w_out, ffn2_norm, ffn2_w_gate, ffn2_w_up, ffn2_w_down, final_norm):
    bsz, seq, d = x.shape
    depth = ffn1_norm.shape[0]
    sw = s5_w_glu.shape[1]
    assert bsz == 1 and sw == w_branch_b.shape[1] and sw % LANE == 0
    assert seq % (S5_CHUNK * LANE) == 0 and sgu_w_s.shape[2] == SGU_CHUNK
    bf = lambda w: w.astype(BF16)
    row = lambda v: v.reshape(1, -1)
    xs = x.reshape(seq, d)
    fin = row(final_norm)
    for i in range(depth):
        xs = _ffn(xs, row(ffn1_norm[i]), bf(ffn1_w_gate[i]), bf(ffn1_w_up[i]), bf(ffn1_w_down[i]),
                  fin, final=False)
        bsb = jnp.repeat(jnp.transpose(sgu_b_s[i]), sw // SGU_HEADS, axis=1)
        ua, h, yb = _mix_in(xs, row(mix_norm[i]), bf(w_in[i]), row(sgu_ln_g[i]), row(sgu_ln_b[i]),
                            sgu_w_s[i], bsb, sw=sw)
        wt, wb, wc, a16, dsk = _s5_weights(s5_a_re[i], s5_a_im[i], s5_log_dt[i], s5_b_re[i], s5_b_im[i],
                                           s5_c_re[i], s5_c_im[i], s5_d[i])
        yap = _s5_core(ua, wt, wb, wc, a16, dsk)
        xs = _merge(xs, h, yap, yb, bf(s5_w_glu[i]), row(s5_b_glu[i]), bf(w_gate[i]), row(b_gate[i]),
                    bf(w_branch_a[i]), bf(w_branch_b[i]), bf(w_out[i]))
        xs = _ffn(xs, row(ffn2_norm[i]), bf(ffn2_w_gate[i]), bf(ffn2_w_up[i]), bf(ffn2_w_down[i]),
                  fin, final=(i == depth - 1))
    return xs.reshape(bsz, seq, d)
```

```python
import functools
import math

import jax
import jax.numpy as jnp
from jax import lax
from jax.experimental import pallas as pl
from jax.experimental.pallas import tpu as pltpu

F32 = jnp.float32
BF16 = jnp.bfloat16

NORM_EPS = 1e-6
S5_GROUP_WIDTH = 16
S5_STATE = 64
SGU_HEADS = 8
SGU_CHUNK = 128
SGU_CAUSAL_BLOCK = 64
S5_CHUNK = 16
LANE = 128
GROUPS_PER_BLOCK = LANE // S5_GROUP_WIDTH
VMEM_LIMIT = 60 * 1024 * 1024


def _rms(x, g):
    ms = jnp.mean(x * x, axis=-1, keepdims=True)
    return (x * lax.rsqrt(ms + NORM_EPS)) * g


def _dot(a, b):
    return jnp.dot(a, b, preferred_element_type=F32)


def _ffn_kernel(x_ref, g_ref, wg_ref, wu_ref, wd_ref, fin_ref, o_ref, h_ref, *, final):
    j = pl.program_id(1)

    @pl.when(j == 0)
    def _():
        h_ref[...] = _rms(x_ref[...], g_ref[...]).astype(BF16)
        o_ref[...] = jnp.zeros_like(o_ref)

    h = h_ref[...]
    a = _dot(h, wg_ref[...].astype(BF16))
    b = _dot(h, wu_ref[...].astype(BF16))
    hid = (a * jax.nn.sigmoid(a)) * b
    o_ref[...] += _dot(hid.astype(BF16), wd_ref[...].astype(BF16))

    @pl.when(j == pl.num_programs(1) - 1)
    def _():
        y = x_ref[...] + 0.5 * o_ref[...]
        if final:
            y = _rms(y, fin_ref[...])
        o_ref[...] = y


def _ffn(x, g, wg, wu, wd, fin, *, final, tm=1024, tf=256):
    n, d = x.shape
    dff = wg.shape[1]
    return pl.pallas_call(
        functools.partial(_ffn_kernel, final=final),
        grid=(n // tm, dff // tf),
        in_specs=[
            pl.BlockSpec((tm, d), lambda i, j: (i, 0)),
            pl.BlockSpec((1, d), lambda i, j: (0, 0)),
            pl.BlockSpec((d, tf), lambda i, j: (0, j)),
            pl.BlockSpec((d, tf), lambda i, j: (0, j)),
            pl.BlockSpec((tf, d), lambda i, j: (j, 0)),
            pl.BlockSpec((1, d), lambda i, j: (0, 0)),
        ],
        out_specs=pl.BlockSpec((tm, d), lambda i, j: (i, 0)),
        out_shape=jax.ShapeDtypeStruct((n, d), F32),
        scratch_shapes=[pltpu.VMEM((tm, d), BF16)],
        compiler_params=pltpu.CompilerParams(
            dimension_semantics=("parallel", "arbitrary"), vmem_limit_bytes=VMEM_LIMIT),
        name="ffn_final" if final else "ffn",
    )(x, g, wg, wu, wd, fin)


def _mix_in_kernel(x_ref, g_ref, w_ref, lng_ref, lnb_ref, ws_ref, bs_ref,
                   ua_ref, h_ref, yb_ref, uv_ref, *, tn, sw):
    j = pl.program_id(1)
    n_a = sw // tn

    @pl.when(j == 0)
    def _():
        h_ref[...] = _rms(x_ref[...], g_ref[...]).astype(BF16)

    p = _dot(h_ref[...], w_ref[...])

    @pl.when(j < n_a)
    def _():
        ua_ref[...] = p

    for jj in range(n_a, 3 * n_a):
        @pl.when(j == jj)
        def _(jj=jj):
            uv_ref[:, (jj - n_a) * tn:(jj - n_a + 1) * tn] = p

    @pl.when(j == 3 * n_a - 1)
    def _():
        tm = uv_ref.shape[0]
        gl = jax.nn.gelu(uv_ref[...])
        u = gl[:, :sw]
        v = gl[:, sw:]
        mu = jnp.mean(v, axis=-1, keepdims=True)
        vc = v - mu
        var = jnp.mean(vc * vc, axis=-1, keepdims=True)
        vn = ((vc * lax.rsqrt(var + NORM_EPS)) * lng_ref[...] + lnb_ref[...]).astype(BF16)
        r = lax.broadcasted_iota(jnp.int32, (SGU_CHUNK, SGU_CHUNK), 0) // SGU_CAUSAL_BLOCK
        c = lax.broadcasted_iota(jnp.int32, (SGU_CHUNK, SGU_CHUNK), 1) // SGU_CAUSAL_BLOCK
        keep = r >= c
        hd = sw // SGU_HEADS
        for h in range(SGU_HEADS):
            wsm = jnp.where(keep, ws_ref[h], 0.0).astype(BF16)
            for q in range(tm // SGU_CHUNK):
                rows = slice(q * SGU_CHUNK, (q + 1) * SGU_CHUNK)
                cols = slice(h * hd, (h + 1) * hd)
                mixed = _dot(wsm, vn[rows, cols]) + bs_ref[:, cols]
                yb_ref[rows, cols] = (u[rows, cols] * mixed).astype(BF16)


def _mix_in(x, g, w_in, ln_g, ln_b, w_s, bsb, *, sw, tm=512, tn=512):
    n, d = x.shape
    n_a = sw // tn
    return pl.pallas_call(
        functools.partial(_mix_in_kernel, tn=tn, sw=sw),
        grid=(n // tm, 3 * n_a),
        in_specs=[
            pl.BlockSpec((tm, d), lambda i, j: (i, 0)),
            pl.BlockSpec((1, d), lambda i, j: (0, 0)),
            pl.BlockSpec((d, tn), lambda i, j: (0, j)),
            pl.BlockSpec((1, sw), lambda i, j: (0, 0)),
            pl.BlockSpec((1, sw), lambda i, j: (0, 0)),
            pl.BlockSpec(w_s.shape, lambda i, j: (0, 0, 0)),
            pl.BlockSpec(bsb.shape, lambda i, j: (0, 0)),
        ],
        out_specs=[
            pl.BlockSpec((tm, tn), lambda i, j: (i, jnp.minimum(j, n_a - 1))),
            pl.BlockSpec((tm, d), lambda i, j: (i, 0)),
            pl.BlockSpec((tm, sw), lambda i, j: (i, 0)),
        ],
        out_shape=[
            jax.ShapeDtypeStruct((n, sw), F32),
            jax.ShapeDtypeStruct((n, d), BF16),
            jax.ShapeDtypeStruct((n, sw), BF16),
        ],
        scratch_shapes=[pltpu.VMEM((tm, 2 * sw), F32)],
        compiler_params=pltpu.CompilerParams(
            dimension_semantics=("parallel", "arbitrary"), vmem_limit_bytes=VMEM_LIMIT),
        name="mix_in",
    )(x, g, w_in, ln_g, ln_b, w_s, bsb)


def _dot_split(a, b):
    a_hi = a.astype(BF16)
    b_hi = b.astype(BF16)
    a_lo = (a - a_hi.astype(F32)).astype(BF16)
    b_lo = (b - b_hi.astype(F32)).astype(BF16)
    return _dot(a_hi, b_hi) + (_dot(a_hi, b_lo) + _dot(a_lo, b_hi))


def _s5_kernel(u_ref, c2_ref, wbf_ref, wc_ref, a_ref, d_ref, o_ref,
               wt_ref, wb_ref, ut_ref, vt_ref, v_ref, xs_ref, xt_ref, yt_ref, yn_ref):
    nc = u_ref.shape[0] // S5_CHUNK
    gw = S5_GROUP_WIDTH
    tw = S5_CHUNK * gw
    ns = GROUPS_PER_BLOCK * S5_STATE

    lane = lax.broadcasted_iota(jnp.int32, (gw, tw), 1)
    for g in range(GROUPS_PER_BLOCK):
        wbf = wbf_ref[g]
        wb_ref[g] = wbf.astype(BF16)
        hrow = _dot_split(c2_ref[g], wbf)
        for t in range(S5_CHUNK):
            shift = (gw * (t + 1)) % tw
            rolled = pltpu.roll(hrow, shift, axis=1) if shift else hrow
            wt_ref[g, t * gw:(t + 1) * gw, :] = jnp.where(lane < gw * (t + 1), rolled, 0.0).astype(BF16)

    for s in range(S5_CHUNK):
        piece = u_ref[pl.ds(s, nc, stride=S5_CHUNK), :]
        ut_ref[s] = piece.T.astype(BF16)

    def chunk_inputs(g):
        return jnp.concatenate(
            [ut_ref[s, g * gw:(g + 1) * gw, :] for s in range(S5_CHUNK)], axis=0)

    for g in range(GROUPS_PER_BLOCK):
        vg = _dot(wb_ref[g], chunk_inputs(g))
        vt_ref[g * S5_STATE:(g + 1) * S5_STATE, :] = vg[:S5_STATE]
        vt_ref[ns + g * S5_STATE:ns + (g + 1) * S5_STATE, :] = vg[S5_STATE:]
    v_ref[...] = vt_ref[...].T

    ar = a_ref[0:1, :]
    ai = a_ref[1:2, :]

    def step(c, carry):
        xr, xi = carry
        xs_ref[pl.ds(c, 1), 0:ns] = xr
        xs_ref[pl.ds(c, 1), ns:2 * ns] = xi
        vr = v_ref[pl.ds(c, 1), 0:ns]
        vi = v_ref[pl.ds(c, 1), ns:2 * ns]
        return ar * xr - ai * xi + vr, ar * xi + ai * xr + vi

    zero = jnp.zeros((1, ns), F32)
    lax.fori_loop(0, nc, step, (zero, zero), unroll=8)
    xt_ref[...] = xs_ref[...].T.astype(BF16)

    for g in range(GROUPS_PER_BLOCK):
        xg = jnp.concatenate(
            [xt_ref[g * S5_STATE:(g + 1) * S5_STATE, :],
             xt_ref[ns + g * S5_STATE:ns + (g + 1) * S5_STATE, :]], axis=0)
        yg = _dot(wt_ref[g], chunk_inputs(g)) + _dot(wc_ref[g], xg)
        for t in range(S5_CHUNK):
            yt_ref[t, g * gw:(g + 1) * gw, :] = yg[t * gw:(t + 1) * gw, :]

    for t in range(S5_CHUNK):
        up = u_ref[pl.ds(t, nc, stride=S5_CHUNK), :]
        y = yt_ref[t].T + d_ref[...] * up
        yn_ref[pl.ds(t, nc, stride=S5_CHUNK), :] = jax.nn.gelu(y)
    o_ref[...] = yn_ref[...].astype(BF16)


def _s5_core(u, c2, wbf, wc, a16, dsk):
    n, sw = u.shape
    nb = sw // LANE
    nc = n // S5_CHUNK
    gb = GROUPS_PER_BLOCK
    ns2 = 2 * gb * S5_STATE
    tw = S5_CHUNK * S5_GROUP_WIDTH
    return pl.pallas_call(
        _s5_kernel,
        grid=(nb,),
        in_specs=[
            pl.BlockSpec((n, LANE), lambda j: (0, j)),
            pl.BlockSpec((gb,) + c2.shape[1:], lambda j: (j, 0, 0)),
            pl.BlockSpec((gb,) + wbf.shape[1:], lambda j: (j, 0, 0)),
            pl.BlockSpec((gb,) + wc.shape[1:], lambda j: (j, 0, 0)),
            pl.BlockSpec((None, 2, ns2 // 2), lambda j: (j, 0, 0)),
            pl.BlockSpec((1, LANE), lambda j: (0, j)),
        ],
        out_specs=pl.BlockSpec((n, LANE), lambda j: (0, j)),
        out_shape=jax.ShapeDtypeStruct((n, sw), BF16),
        scratch_shapes=[
            pltpu.VMEM((gb, tw, tw), BF16),
            pltpu.VMEM((gb,) + wbf.shape[1:], BF16),
            pltpu.VMEM((S5_CHUNK, LANE, nc), BF16),
            pltpu.VMEM((ns2, nc), F32),
            pltpu.VMEM((nc, ns2), F32),
            pltpu.VMEM((nc, ns2), F32),
            pltpu.VMEM((ns2, nc), BF16),
            pltpu.VMEM((S5_CHUNK, LANE, nc), F32),
            pltpu.VMEM((n, LANE), F32),
        ],
        compiler_params=pltpu.CompilerParams(
            dimension_semantics=("parallel",), vmem_limit_bytes=VMEM_LIMIT),
        name="s5_core",
    )(u, c2, wbf, wc, a16, dsk)


def _s5_weights(a_re, a_im, log_dt, b_re, b_im, c_re, c_im, d_skip):
    g, p = a_re.shape
    tc = S5_CHUNK
    dt = jnp.exp(log_dt)[:, None]
    decay = jnp.exp(a_re * dt)
    ab_re = decay * jnp.cos(a_im * dt)
    ab_im = decay * jnp.sin(a_im * dt)
    denom = a_re * a_re + a_im * a_im
    num_re = ab_re - 1.0
    num_im = ab_im
    k_re = (num_re * a_re + num_im * a_im) / denom
    k_im = (num_im * a_re - num_re * a_im) / denom
    bp_re = k_re[..., None] * b_re - k_im[..., None] * b_im
    bp_im = k_re[..., None] * b_im + k_im[..., None] * b_re

    def cmul(xr, xi, yr, yi):
        return xr * yr - xi * yi, xr * yi + xi * yr

    pw_re, pw_im = [jnp.ones_like(ab_re)], [jnp.zeros_like(ab_im)]
    for _ in range(tc):
        nr, ni = cmul(pw_re[-1], pw_im[-1], ab_re, ab_im)
        pw_re.append(nr)
        pw_im.append(ni)
    pw_re = jnp.stack(pw_re)
    pw_im = jnp.stack(pw_im)

    ab_r = pw_re[:tc, :, :, None] * bp_re[None] - pw_im[:tc, :, :, None] * bp_im[None]
    ab_i = pw_re[:tc, :, :, None] * bp_im[None] + pw_im[:tc, :, :, None] * bp_re[None]
    wb_r = jnp.transpose(ab_r[::-1], (1, 2, 0, 3)).reshape(g, p, tc * S5_GROUP_WIDTH)
    wb_i = jnp.transpose(ab_i[::-1], (1, 2, 0, 3)).reshape(g, p, tc * S5_GROUP_WIDTH)
    wb = jnp.concatenate([wb_r, wb_i], axis=1)
    ca_r = c_re[None] * pw_re[1:, :, None, :] - c_im[None] * pw_im[1:, :, None, :]
    ca_i = c_re[None] * pw_im[1:, :, None, :] + c_im[None] * pw_re[1:, :, None, :]
    wc = jnp.concatenate([ca_r, -ca_i], axis=-1)
    wc = jnp.transpose(wc, (1, 0, 2, 3)).reshape(g, tc * S5_GROUP_WIDTH, 2 * p)
    gb = GROUPS_PER_BLOCK
    a16 = jnp.stack([pw_re[tc].reshape(g // gb, gb * p), pw_im[tc].reshape(g // gb, gb * p)], axis=1)
    c2 = jnp.concatenate([c_re, -c_im], axis=-1)
    return c2, wb, wc.astype(BF16), a16, d_skip.reshape(1, -1)


def _merge_kernel(x_ref, h_ref, yap_ref, yb_ref, wglu_ref, bglu_ref, wga_ref, wgb_ref,
                  bga_ref, bgb_ref, wa_ref, wb_ref, wo_ref, o_ref, ya_ref, acc_ref):
    j = pl.program_id(1)

    @pl.when(j == 0)
    def _():
        yp = yap_ref[...]
        z = _dot(yp, wglu_ref[...]) + bglu_ref[...]
        ya_ref[...] = (yp.astype(F32) * jax.nn.sigmoid(z)).astype(BF16)
        acc_ref[...] = jnp.zeros_like(acc_ref)

    h = h_ref[...]
    ga = jax.nn.sigmoid(_dot(h, wga_ref[...]) + bga_ref[...])
    gb = jax.nn.sigmoid(_dot(h, wgb_ref[...]) + bgb_ref[...])
    pa = _dot(ya_ref[...], wa_ref[...])
    pb = _dot(yb_ref[...], wb_ref[...])
    m = (ga * pa + gb * pb).astype(BF16)
    acc_ref[...] += _dot(m, wo_ref[...])

    @pl.when(j == pl.num_programs(1) - 1)
    def _():
        o_ref[...] = x_ref[...] + acc_ref[...]


def _merge(x, h, yap, yb, wglu, bglu, wgate, bgate, wa, wb, wo, *, tm=512, tn=256):
    n, d = x.shape
    sw = yap.shape[1]
    nj = d // tn
    return pl.pallas_call(
        _merge_kernel,
        grid=(n // tm, nj),
        in_specs=[
            pl.BlockSpec((tm, d), lambda i, j: (i, 0)),
            pl.BlockSpec((tm, d), lambda i, j: (i, 0)),
            pl.BlockSpec((tm, sw), lambda i, j: (i, 0)),
            pl.BlockSpec((tm, sw), lambda i, j: (i, 0)),
            pl.BlockSpec((sw, sw), lambda i, j: (0, 0)),
            pl.BlockSpec((1, sw), lambda i, j: (0, 0)),
            pl.BlockSpec((d, tn), lambda i, j: (0, j)),
            pl.BlockSpec((d, tn), lambda i, j: (0, j + nj)),
            pl.BlockSpec((1, tn), lambda i, j: (0, j)),
            pl.BlockSpec((1, tn), lambda i, j: (0, j + nj)),
            pl.BlockSpec((sw, tn), lambda i, j: (0, j)),
            pl.BlockSpec((sw, tn), lambda i, j: (0, j)),
            pl.BlockSpec((tn, d), lambda i, j: (j, 0)),
        ],
        out_specs=pl.BlockSpec((tm, d), lambda i, j: (i, 0)),
        out_shape=jax.ShapeDtypeStruct((n, d), F32),
        scratch_shapes=[pltpu.VMEM((tm, sw), BF16), pltpu.VMEM((tm, d), F32)],
        compiler_params=pltpu.CompilerParams(
            dimension_semantics=("parallel", "arbitrary"), vmem_limit_bytes=VMEM_LIMIT),
        name="merge",
    )(x, h, yap, yb, wglu, bglu, wgate, wgate, bgate, bgate, wa, wb, wo)


def kernel(x, ffn1_norm, ffn1_w_gate, ffn1_w_up, ffn1_w_down, mix_norm, w_in, s5_a_re, s5_a_im, s5_log_dt, s5_b_re, s5_b_im, s5_c_re, s5_c_im, s5_d, s5_w_glu, s5_b_glu, sgu_ln_g, sgu_ln_b, sgu_w_s, sgu_b_s, w_branch_a, w_branch_b, w_gate, b_gate, w_out, ffn2_norm, ffn2_w_gate, ffn2_w_up, ffn2_w_down, final_norm):
    bsz, seq, d = x.shape
    depth = ffn1_norm.shape[0]
    sw = s5_w_glu.shape[1]
    assert bsz == 1 and sw == w_branch_b.shape[1] and sw % LANE == 0
    assert seq % (S5_CHUNK * LANE) == 0 and sgu_w_s.shape[2] == SGU_CHUNK
    bf = lambda w: w.astype(BF16)
    row = lambda v: v.reshape(1, -1)
    xs = x.reshape(seq, d)
    fin = row(final_norm)
    for i in range(depth):
        xs = _ffn(xs, row(ffn1_norm[i]), ffn1_w_gate[i], ffn1_w_up[i], ffn1_w_down[i], fin, final=False)
        bsb = jnp.repeat(jnp.transpose(sgu_b_s[i]), sw // SGU_HEADS, axis=1)
        ua, h, yb = _mix_in(xs, row(mix_norm[i]), bf(w_in[i]), row(sgu_ln_g[i]), row(sgu_ln_b[i]),
                            sgu_w_s[i], bsb, sw=sw)
        c2, wbf, wc, a16, dsk = _s5_weights(s5_a_re[i], s5_a_im[i], s5_log_dt[i], s5_b_re[i], s5_b_im[i],
                                            s5_c_re[i], s5_c_im[i], s5_d[i])
        yap = _s5_core(ua, c2, wbf, wc, a16, dsk)
        xs = _merge(xs, h, yap, yb, bf(s5_w_glu[i]), row(s5_b_glu[i]), bf(w_gate[i]), row(b_gate[i]),
                    bf(w_branch_a[i]), bf(w_branch_b[i]), bf(w_out[i]))
        xs = _ffn(xs, row(ffn2_norm[i]), ffn2_w_gate[i], ffn2_w_up[i], ffn2_w_down[i], fin,
                  final=(i == depth - 1))
    return xs.reshape(bsz, seq, d)
```

```python
import functools

import jax
import jax.numpy as jnp
from jax import lax
from jax.experimental import pallas as pl
from jax.experimental.pallas import tpu as pltpu

F32 = jnp.float32
BF16 = jnp.bfloat16

NORM_EPS = 1e-6
S5_GROUP_WIDTH = 16
S5_STATE = 64
SGU_HEADS = 8
SGU_CHUNK = 128
SGU_CAUSAL_BLOCK = 64
S5_CHUNK = 16
LANE = 128
GROUPS_PER_BLOCK = LANE // S5_GROUP_WIDTH
VMEM_LIMIT = 60 * 1024 * 1024


def _rms(x, g):
    ms = jnp.mean(x * x, axis=-1, keepdims=True)
    return (x * lax.rsqrt(ms + NORM_EPS)) * g


def _dot(a, b):
    return jnp.dot(a, b, preferred_element_type=F32)


def _ffn_kernel(x_ref, g_ref, wg_ref, wu_ref, wd_ref, fin_ref, o_ref, h_ref, *, final):
    j = pl.program_id(1)

    @pl.when(j == 0)
    def _():
        h_ref[...] = _rms(x_ref[...], g_ref[...]).astype(BF16)
        o_ref[...] = jnp.zeros_like(o_ref)

    h = h_ref[...]
    a = _dot(h, wg_ref[...].astype(BF16))
    b = _dot(h, wu_ref[...].astype(BF16))
    hid = (a * jax.nn.sigmoid(a)) * b
    o_ref[...] += _dot(hid.astype(BF16), wd_ref[...].astype(BF16))

    @pl.when(j == pl.num_programs(1) - 1)
    def _():
        y = x_ref[...] + 0.5 * o_ref[...]
        if final:
            y = _rms(y, fin_ref[...])
        o_ref[...] = y


def _ffn(x, g, wg, wu, wd, fin, *, final, tm=1024, tf=256):
    n, d = x.shape
    dff = wg.shape[1]
    return pl.pallas_call(
        functools.partial(_ffn_kernel, final=final),
        grid=(n // tm, dff // tf),
        in_specs=[
            pl.BlockSpec((tm, d), lambda i, j: (i, 0)),
            pl.BlockSpec((1, d), lambda i, j: (0, 0)),
            pl.BlockSpec((d, tf), lambda i, j: (0, j)),
            pl.BlockSpec((d, tf), lambda i, j: (0, j)),
            pl.BlockSpec((tf, d), lambda i, j: (j, 0)),
            pl.BlockSpec((1, d), lambda i, j: (0, 0)),
        ],
        out_specs=pl.BlockSpec((tm, d), lambda i, j: (i, 0)),
        out_shape=jax.ShapeDtypeStruct((n, d), F32),
        scratch_shapes=[pltpu.VMEM((tm, d), BF16)],
        compiler_params=pltpu.CompilerParams(
            dimension_semantics=("parallel", "arbitrary"), vmem_limit_bytes=VMEM_LIMIT),
        name="ffn_final" if final else "ffn",
    )(x, g, wg, wu, wd, fin)


def _mix_in_kernel(x_ref, g_ref, w_ref, lng_ref, lnb_ref, ws_ref, bs_ref,
                   ua_ref, h_ref, yb_ref, ug_ref, *, sw):
    j = pl.program_id(1)

    @pl.when(j == 0)
    def _():
        h = _rms(x_ref[...], g_ref[...]).astype(BF16)
        h_ref[...] = h
        ua_ref[...] = _dot(h, w_ref[...])

    @pl.when(j == 1)
    def _():
        ug_ref[...] = jax.nn.gelu(_dot(h_ref[...], w_ref[...]))

    @pl.when(j == 2)
    def _():
        tm = ug_ref.shape[0]
        v = jax.nn.gelu(_dot(h_ref[...], w_ref[...]))
        mu = jnp.mean(v, axis=-1, keepdims=True)
        vc = v - mu
        var = jnp.mean(vc * vc, axis=-1, keepdims=True)
        vn = ((vc * lax.rsqrt(var + NORM_EPS)) * lng_ref[...] + lnb_ref[...]).astype(BF16)
        r = lax.broadcasted_iota(jnp.int32, (SGU_CHUNK, SGU_CHUNK), 0) // SGU_CAUSAL_BLOCK
        c = lax.broadcasted_iota(jnp.int32, (SGU_CHUNK, SGU_CHUNK), 1) // SGU_CAUSAL_BLOCK
        keep = r >= c
        hd = sw // SGU_HEADS
        for h in range(SGU_HEADS):
            wsm = jnp.where(keep, ws_ref[h], 0.0).astype(BF16)
            cols = slice(h * hd, (h + 1) * hd)
            for q in range(tm // SGU_CHUNK):
                rows = slice(q * SGU_CHUNK, (q + 1) * SGU_CHUNK)
                mixed = _dot(wsm, vn[rows, cols]) + bs_ref[:, cols]
                yb_ref[rows, cols] = (ug_ref[rows, cols] * mixed).astype(BF16)


def _mix_in(x, g, w_in, ln_g, ln_b, w_s, bsb, *, sw, tm=512):
    n, d = x.shape
    assert w_in.shape[1] == 3 * sw
    return pl.pallas_call(
        functools.partial(_mix_in_kernel, sw=sw),
        grid=(n // tm, 3),
        in_specs=[
            pl.BlockSpec((tm, d), lambda i, j: (i, 0)),
            pl.BlockSpec((1, d), lambda i, j: (0, 0)),
            pl.BlockSpec((d, sw), lambda i, j: (0, j)),
            pl.BlockSpec((1, sw), lambda i, j: (0, 0)),
            pl.BlockSpec((1, sw), lambda i, j: (0, 0)),
            pl.BlockSpec(w_s.shape, lambda i, j: (0, 0, 0)),
            pl.BlockSpec(bsb.shape, lambda i, j: (0, 0)),
        ],
        out_specs=[
            pl.BlockSpec((tm, sw), lambda i, j: (i, 0)),
            pl.BlockSpec((tm, d), lambda i, j: (i, 0)),
            pl.BlockSpec((tm, sw), lambda i, j: (i, 0)),
        ],
        out_shape=[
            jax.ShapeDtypeStruct((n, sw), F32),
            jax.ShapeDtypeStruct((n, d), BF16),
            jax.ShapeDtypeStruct((n, sw), BF16),
        ],
        scratch_shapes=[pltpu.VMEM((tm, sw), F32)],
        compiler_params=pltpu.CompilerParams(
            dimension_semantics=("parallel", "arbitrary"), vmem_limit_bytes=VMEM_LIMIT),
        name="mix_in",
    )(x, g, w_in, ln_g, ln_b, w_s, bsb)


def _dot_split(a, b):
    a_hi = a.astype(BF16)
    b_hi = b.astype(BF16)
    a_lo = (a - a_hi.astype(F32)).astype(BF16)
    b_lo = (b - b_hi.astype(F32)).astype(BF16)
    return _dot(a_hi, b_hi) + (_dot(a_hi, b_lo) + _dot(a_lo, b_hi))


def _s5_kernel(u_ref, c2_ref, wbf_ref, wc_ref, a_ref, d_ref, o_ref,
               wt_ref, wb_ref, ut_ref, vt_ref, v_ref, xs_ref, xt_ref, yt_ref, yn_ref):
    nc = u_ref.shape[0] // S5_CHUNK
    gw = S5_GROUP_WIDTH
    tw = S5_CHUNK * gw
    ns = GROUPS_PER_BLOCK * S5_STATE

    lane = lax.broadcasted_iota(jnp.int32, (gw, tw), 1)
    for g in range(GROUPS_PER_BLOCK):
        wbf = wbf_ref[g]
        wb_ref[g] = wbf.astype(BF16)
        hrow = _dot_split(c2_ref[g], wbf)
        for t in range(S5_CHUNK):
            shift = (gw * (t + 1)) % tw
            rolled = pltpu.roll(hrow, shift, axis=1) if shift else hrow
            wt_ref[g, t * gw:(t + 1) * gw, :] = jnp.where(lane < gw * (t + 1), rolled, 0.0).astype(BF16)

    for s in range(S5_CHUNK):
        piece = u_ref[pl.ds(s, nc, stride=S5_CHUNK), :]
        ut_ref[s] = piece.T.astype(BF16)

    def chunk_inputs(g):
        return jnp.concatenate(
            [ut_ref[s, g * gw:(g + 1) * gw, :] for s in range(S5_CHUNK)], axis=0)

    for g in range(GROUPS_PER_BLOCK):
        vg = _dot(wb_ref[g], chunk_inputs(g))
        vt_ref[g * S5_STATE:(g + 1) * S5_STATE, :] = vg[:S5_STATE]
        vt_ref[ns + g * S5_STATE:ns + (g + 1) * S5_STATE, :] = vg[S5_STATE:]
    v_ref[...] = vt_ref[...].T

    ar = a_ref[0:1, :]
    ai = a_ref[1:2, :]

    def step(c, carry):
        xr, xi = carry
        xs_ref[pl.ds(c, 1), 0:ns] = xr
        xs_ref[pl.ds(c, 1), ns:2 * ns] = xi
        vr = v_ref[pl.ds(c, 1), 0:ns]
        vi = v_ref[pl.ds(c, 1), ns:2 * ns]
        return ar * xr - ai * xi + vr, ar * xi + ai * xr + vi

    zero = jnp.zeros((1, ns), F32)
    lax.fori_loop(0, nc, step, (zero, zero), unroll=8)
    xt_ref[...] = xs_ref[...].T.astype(BF16)

    for g in range(GROUPS_PER_BLOCK):
        xg = jnp.concatenate(
            [xt_ref[g * S5_STATE:(g + 1) * S5_STATE, :],
             xt_ref[ns + g * S5_STATE:ns + (g + 1) * S5_STATE, :]], axis=0)
        yg = _dot(wt_ref[g], chunk_inputs(g)) + _dot(wc_ref[g], xg)
        for t in range(S5_CHUNK):
            yt_ref[t, g * gw:(g + 1) * gw, :] = yg[t * gw:(t + 1) * gw, :]

    for t in range(S5_CHUNK):
        up = u_ref[pl.ds(t, nc, stride=S5_CHUNK), :]
        y = yt_ref[t].T + d_ref[...] * up
        yn_ref[pl.ds(t, nc, stride=S5_CHUNK), :] = jax.nn.gelu(y)
    o_ref[...] = yn_ref[...].astype(BF16)


def _s5_core(u, c2, wbf, wc, a16, dsk):
    n, sw = u.shape
    nb = sw // LANE
    nc = n // S5_CHUNK
    gb = GROUPS_PER_BLOCK
    ns2 = 2 * gb * S5_STATE
    tw = S5_CHUNK * S5_GROUP_WIDTH
    return pl.pallas_call(
        _s5_kernel,
        grid=(nb,),
        in_specs=[
            pl.BlockSpec((n, LANE), lambda j: (0, j)),
            pl.BlockSpec((gb,) + c2.shape[1:], lambda j: (j, 0, 0)),
            pl.BlockSpec((gb,) + wbf.shape[1:], lambda j: (j, 0, 0)),
            pl.BlockSpec((gb,) + wc.shape[1:], lambda j: (j, 0, 0)),
            pl.BlockSpec((None, 2, ns2 // 2), lambda j: (j, 0, 0)),
            pl.BlockSpec((1, LANE), lambda j: (0, j)),
        ],
        out_specs=pl.BlockSpec((n, LANE), lambda j: (0, j)),
        out_shape=jax.ShapeDtypeStruct((n, sw), BF16),
        scratch_shapes=[
            pltpu.VMEM((gb, tw, tw), BF16),
            pltpu.VMEM((gb,) + wbf.shape[1:], BF16),
            pltpu.VMEM((S5_CHUNK, LANE, nc), BF16),
            pltpu.VMEM((ns2, nc), F32),
            pltpu.VMEM((nc, ns2), F32),
            pltpu.VMEM((nc, ns2), F32),
            pltpu.VMEM((ns2, nc), BF16),
            pltpu.VMEM((S5_CHUNK, LANE, nc), F32),
            pltpu.VMEM((n, LANE), F32),
        ],
        compiler_params=pltpu.CompilerParams(
            dimension_semantics=("parallel",), vmem_limit_bytes=VMEM_LIMIT),
        name="s5_core",
    )(u, c2, wbf, wc, a16, dsk)


def _s5_weights(a_re, a_im, log_dt, b_re, b_im, c_re, c_im, d_skip):
    g, p = a_re.shape
    tc = S5_CHUNK
    dt = jnp.exp(log_dt)[:, None]
    decay = jnp.exp(a_re * dt)
    ab_re = decay * jnp.cos(a_im * dt)
    ab_im = decay * jnp.sin(a_im * dt)
    denom = a_re * a_re + a_im * a_im
    num_re = ab_re - 1.0
    num_im = ab_im
    k_re = (num_re * a_re + num_im * a_im) / denom
    k_im = (num_im * a_re - num_re * a_im) / denom
    bp_re = k_re[..., None] * b_re - k_im[..., None] * b_im
    bp_im = k_re[..., None] * b_im + k_im[..., None] * b_re

    def cmul(xr, xi, yr, yi):
        return xr * yr - xi * yi, xr * yi + xi * yr

    pw_re, pw_im = [jnp.ones_like(ab_re)], [jnp.zeros_like(ab_im)]
    for _ in range(tc):
        nr, ni = cmul(pw_re[-1], pw_im[-1], ab_re, ab_im)
        pw_re.append(nr)
        pw_im.append(ni)
    pw_re = jnp.stack(pw_re)
    pw_im = jnp.stack(pw_im)

    ab_r = pw_re[:tc, :, :, None] * bp_re[None] - pw_im[:tc, :, :, None] * bp_im[None]
    ab_i = pw_re[:tc, :, :, None] * bp_im[None] + pw_im[:tc, :, :, None] * bp_re[None]
    wb_r = jnp.transpose(ab_r[::-1], (1, 2, 0, 3)).reshape(g, p, tc * S5_GROUP_WIDTH)
    wb_i = jnp.transpose(ab_i[::-1], (1, 2, 0, 3)).reshape(g, p, tc * S5_GROUP_WIDTH)
    wb = jnp.concatenate([wb_r, wb_i], axis=1)
    ca_r = c_re[None] * pw_re[1:, :, None, :] - c_im[None] * pw_im[1:, :, None, :]
    ca_i = c_re[None] * pw_im[1:, :, None, :] + c_im[None] * pw_re[1:, :, None, :]
    wc = jnp.concatenate([ca_r, -ca_i], axis=-1)
    wc = jnp.transpose(wc, (1, 0, 2, 3)).reshape(g, tc * S5_GROUP_WIDTH, 2 * p)
    gb = GROUPS_PER_BLOCK
    a16 = jnp.stack([pw_re[tc].reshape(g // gb, gb * p), pw_im[tc].reshape(g // gb, gb * p)], axis=1)
    c2 = jnp.concatenate([c_re, -c_im], axis=-1)
    return c2, wb, wc.astype(BF16), a16, d_skip.reshape(1, -1)


def _merge_kernel(x_ref, h_ref, yap_ref, yb_ref, wglu_ref, bglu_ref, wga_ref, wgb_ref,
                  bga_ref, bgb_ref, wa_ref, wb_ref, wo_ref, o_ref, ya_ref, m_ref, *, nj):
    j = pl.program_id(1)

    @pl.when(j == 0)
    def _():
        yp = yap_ref[...]
        z = _dot(yp, wglu_ref[...]) + bglu_ref[...]
        ya_ref[...] = (yp.astype(F32) * jax.nn.sigmoid(z)).astype(BF16)

    @pl.when(j < nj)
    def _():
        h = h_ref[...]
        ga = jax.nn.sigmoid(_dot(h, wga_ref[...]) + bga_ref[...])
        gb = jax.nn.sigmoid(_dot(h, wgb_ref[...]) + bgb_ref[...])
        pa = _dot(ya_ref[...], wa_ref[...])
        pb = _dot(yb_ref[...], wb_ref[...])
        m_ref[j] = (ga * pa + gb * pb).astype(BF16)

    @pl.when(j >= nj)
    def _():
        m = jnp.concatenate([m_ref[k] for k in range(nj)], axis=1)
        o_ref[...] = x_ref[...] + _dot(m, wo_ref[...])


def _merge(x, h, yap, yb, wglu, bglu, wgate, bgate, wa, wb, wo, *, tm=1024, tn=512):
    n, d = x.shape
    sw = yap.shape[1]
    nj = d // tn
    first = lambda j: jnp.minimum(j, nj - 1)
    second = lambda j: jnp.maximum(j - nj, 0)
    return pl.pallas_call(
        functools.partial(_merge_kernel, nj=nj),
        grid=(n // tm, 2 * nj),
        in_specs=[
            pl.BlockSpec((tm, tn), lambda i, j: (i, second(j))),
            pl.BlockSpec((tm, d), lambda i, j: (i, 0)),
            pl.BlockSpec((tm, sw), lambda i, j: (i, 0)),
            pl.BlockSpec((tm, sw), lambda i, j: (i, 0)),
            pl.BlockSpec((sw, sw), lambda i, j: (0, 0)),
            pl.BlockSpec((1, sw), lambda i, j: (0, 0)),
            pl.BlockSpec((d, tn), lambda i, j: (0, first(j))),
            pl.BlockSpec((d, tn), lambda i, j: (0, first(j) + nj)),
            pl.BlockSpec((1, tn), lambda i, j: (0, first(j))),
            pl.BlockSpec((1, tn), lambda i, j: (0, first(j) + nj)),
            pl.BlockSpec((sw, tn), lambda i, j: (0, first(j))),
            pl.BlockSpec((sw, tn), lambda i, j: (0, first(j))),
            pl.BlockSpec((d, tn), lambda i, j: (0, second(j))),
        ],
        out_specs=pl.BlockSpec((tm, tn), lambda i, j: (i, second(j))),
        out_shape=jax.ShapeDtypeStruct((n, d), F32),
        scratch_shapes=[pltpu.VMEM((tm, sw), BF16), pltpu.VMEM((nj, tm, tn), BF16)],
        compiler_params=pltpu.CompilerParams(
            dimension_semantics=("parallel", "arbitrary"), vmem_limit_bytes=VMEM_LIMIT),
        name="merge",
    )(x, h, yap, yb, wglu, bglu, wgate, wgate, bgate, bgate, wa, wb, wo)


def kernel(x, ffn1_norm, ffn1_w_gate, ffn1_w_up, ffn1_w_down, mix_norm, w_in, s5_a_re, s5_a_im, s5_log_dt, s5_b_re, s5_b_im, s5_c_re, s5_c_im, s5_d, s5_w_glu, s5_b_glu, sgu_ln_g, sgu_ln_b, sgu_w_s, sgu_b_s, w_branch_a, w_branch_b, w_gate, b_gate, w_out, ffn2_norm, ffn2_w_gate, ffn2_w_up, ffn2_w_down, final_norm):
    bsz, seq, d = x.shape
    depth = ffn1_norm.shape[0]
    sw = s5_w_glu.shape[1]
    assert bsz == 1 and sw == w_branch_b.shape[1] and sw % LANE == 0
    assert seq % (S5_CHUNK * LANE) == 0 and sgu_w_s.shape[2] == SGU_CHUNK
    bf = lambda w: w.astype(BF16)
    row = lambda v: v.reshape(1, -1)
    xs = x.reshape(seq, d)
    fin = row(final_norm)
    for i in range(depth):
        xs = _ffn(xs, row(ffn1_norm[i]), ffn1_w_gate[i], ffn1_w_up[i], ffn1_w_down[i], fin, final=False)
        bsb = jnp.repeat(jnp.transpose(sgu_b_s[i]), sw // SGU_HEADS, axis=1)
        ua, h, yb = _mix_in(xs, row(mix_norm[i]), bf(w_in[i]), row(sgu_ln_g[i]), row(sgu_ln_b[i]),
                            sgu_w_s[i], bsb, sw=sw)
        c2, wbf, wc, a16, dsk = _s5_weights(s5_a_re[i], s5_a_im[i], s5_log_dt[i], s5_b_re[i], s5_b_im[i],
                                            s5_c_re[i], s5_c_im[i], s5_d[i])
        yap = _s5_core(ua, c2, wbf, wc, a16, dsk)
        xs = _merge(xs, h, yap, yb, bf(s5_w_glu[i]), row(s5_b_glu[i]), bf(w_gate[i]), row(b_gate[i]),
                    bf(w_branch_a[i]), bf(w_branch_b[i]), bf(w_out[i]))
        xs = _ffn(xs, row(ffn2_norm[i]), ffn2_w_gate[i], ffn2_w_up[i], ffn2_w_down[i], fin,
                  final=(i == depth - 1))
    return xs.reshape(bsz, seq, d)
```

```python
import functools

import jax
import jax.numpy as jnp
from jax import lax
from jax.experimental import pallas as pl
from jax.experimental.pallas import tpu as pltpu

F32 = jnp.float32
BF16 = jnp.bfloat16

NORM_EPS = 1e-6
S5_GROUP_WIDTH = 16
S5_STATE = 64
SGU_HEADS = 8
SGU_CHUNK = 128
SGU_CAUSAL_BLOCK = 64
S5_CHUNK = 16
LANE = 128
GROUPS_PER_BLOCK = LANE // S5_GROUP_WIDTH
VMEM_LIMIT = 60 * 1024 * 1024
ROW_CHUNK = 256


def _rms(x, g):
    ms = jnp.mean(x * x, axis=-1, keepdims=True)
    return (x * lax.rsqrt(ms + NORM_EPS)) * g


def _dot(a, b):
    return jnp.dot(a, b, preferred_element_type=F32)


def _ffn_kernel(x_ref, g_ref, wg_ref, wu_ref, wd_ref, fin_ref, o_ref, h_ref, *, final):
    j = pl.program_id(1)
    n_chunks = x_ref.shape[0] // ROW_CHUNK

    @pl.when(j == 0)
    def _():
        g = g_ref[...]

        def rows(r, carry):
            rs = pl.ds(pl.multiple_of(r * ROW_CHUNK, ROW_CHUNK), ROW_CHUNK)
            x = x_ref[rs, :]
            h_ref[rs, :] = _rms(x, g).astype(BF16)
            o_ref[rs, :] = x
            return carry

        lax.fori_loop(0, n_chunks, rows, 0, unroll=2)

    h = h_ref[...]
    a = _dot(h, wg_ref[...].astype(BF16))
    b = _dot(h, wu_ref[...].astype(BF16))
    hid = (a * jax.nn.sigmoid(a)) * b
    o_ref[...] += _dot(hid.astype(BF16), (0.5 * wd_ref[...]).astype(BF16))

    if final:
        @pl.when(j == pl.num_programs(1) - 1)
        def _():
            fin = fin_ref[...]

            def rows(r, carry):
                rs = pl.ds(pl.multiple_of(r * ROW_CHUNK, ROW_CHUNK), ROW_CHUNK)
                o_ref[rs, :] = _rms(o_ref[rs, :], fin)
                return carry

            lax.fori_loop(0, n_chunks, rows, 0, unroll=2)


def _next_tile_after(i, j, first_step, n_tiles):
    return jnp.minimum(i + jnp.where(j >= first_step, 1, 0), n_tiles - 1)


def _ffn(x, g, wg, wu, wd, fin, *, final, tm=1024, tf=256):
    n, d = x.shape
    dff = wg.shape[1]
    nt = n // tm
    return pl.pallas_call(
        functools.partial(_ffn_kernel, final=final),
        grid=(nt, dff // tf),
        in_specs=[
            pl.BlockSpec((tm, d), lambda i, j: (_next_tile_after(i, j, 1, nt), 0)),
            pl.BlockSpec((1, d), lambda i, j: (0, 0)),
            pl.BlockSpec((d, tf), lambda i, j: (0, j)),
            pl.BlockSpec((d, tf), lambda i, j: (0, j)),
            pl.BlockSpec((tf, d), lambda i, j: (j, 0)),
            pl.BlockSpec((1, d), lambda i, j: (0, 0)),
        ],
        out_specs=pl.BlockSpec((tm, d), lambda i, j: (i, 0)),
        out_shape=jax.ShapeDtypeStruct((n, d), F32),
        scratch_shapes=[pltpu.VMEM((tm, d), BF16)],
        compiler_params=pltpu.CompilerParams(
            dimension_semantics=("parallel", "arbitrary"), vmem_limit_bytes=VMEM_LIMIT),
        name="ffn_final" if final else "ffn",
    )(x, g, wg, wu, wd, fin)


def _mix_in_kernel(x_ref, g_ref, w_ref, lng_ref, lnb_ref, ws_ref, bs_ref,
                   ua_ref, h_ref, yb_ref, ug_ref, *, sw):
    j = pl.program_id(1)

    @pl.when(j == 0)
    def _():
        h = _rms(x_ref[...], g_ref[...]).astype(BF16)
        h_ref[...] = h
        ua_ref[...] = _dot(h, w_ref[...])

    @pl.when(j == 1)
    def _():
        ug_ref[...] = jax.nn.gelu(_dot(h_ref[...], w_ref[...]))

    @pl.when(j == 2)
    def _():
        tm = ug_ref.shape[0]
        v = jax.nn.gelu(_dot(h_ref[...], w_ref[...]))
        mu = jnp.mean(v, axis=-1, keepdims=True)
        vc = v - mu
        var = jnp.mean(vc * vc, axis=-1, keepdims=True)
        vn = ((vc * lax.rsqrt(var + NORM_EPS)) * lng_ref[...] + lnb_ref[...]).astype(BF16)
        r = lax.broadcasted_iota(jnp.int32, (SGU_CHUNK, SGU_CHUNK), 0) // SGU_CAUSAL_BLOCK
        c = lax.broadcasted_iota(jnp.int32, (SGU_CHUNK, SGU_CHUNK), 1) // SGU_CAUSAL_BLOCK
        keep = r >= c
        hd = sw // SGU_HEADS
        for h in range(SGU_HEADS):
            wsm = jnp.where(keep, ws_ref[h], 0.0).astype(BF16)
            cols = slice(h * hd, (h + 1) * hd)
            for q in range(tm // SGU_CHUNK):
                rows = slice(q * SGU_CHUNK, (q + 1) * SGU_CHUNK)
                mixed = _dot(wsm, vn[rows, cols]) + bs_ref[:, cols]
                yb_ref[rows, cols] = (ug_ref[rows, cols] * mixed).astype(BF16)


def _mix_in(x, g, w_in, ln_g, ln_b, w_s, bsb, *, sw, tm=512):
    n, d = x.shape
    assert w_in.shape[1] == 3 * sw
    nt = n // tm
    return pl.pallas_call(
        functools.partial(_mix_in_kernel, sw=sw),
        grid=(nt, 3),
        in_specs=[
            pl.BlockSpec((tm, d), lambda i, j: (_next_tile_after(i, j, 1, nt), 0)),
            pl.BlockSpec((1, d), lambda i, j: (0, 0)),
            pl.BlockSpec((d, sw), lambda i, j: (0, j)),
            pl.BlockSpec((1, sw), lambda i, j: (0, 0)),
            pl.BlockSpec((1, sw), lambda i, j: (0, 0)),
            pl.BlockSpec(w_s.shape, lambda i, j: (0, 0, 0)),
            pl.BlockSpec(bsb.shape, lambda i, j: (0, 0)),
        ],
        out_specs=[
            pl.BlockSpec((tm, sw), lambda i, j: (i, 0)),
            pl.BlockSpec((tm, d), lambda i, j: (i, 0)),
            pl.BlockSpec((tm, sw), lambda i, j: (i, 0)),
        ],
        out_shape=[
            jax.ShapeDtypeStruct((n, sw), F32),
            jax.ShapeDtypeStruct((n, d), BF16),
            jax.ShapeDtypeStruct((n, sw), BF16),
        ],
        scratch_shapes=[pltpu.VMEM((tm, sw), F32)],
        compiler_params=pltpu.CompilerParams(
            dimension_semantics=("parallel", "arbitrary"), vmem_limit_bytes=VMEM_LIMIT),
        name="mix_in",
    )(x, g, w_in, ln_g, ln_b, w_s, bsb)


def _dot_split(a, b):
    a_hi = a.astype(BF16)
    b_hi = b.astype(BF16)
    a_lo = (a - a_hi.astype(F32)).astype(BF16)
    b_lo = (b - b_hi.astype(F32)).astype(BF16)
    return _dot(a_hi, b_hi) + (_dot(a_hi, b_lo) + _dot(a_lo, b_hi))


def _s5_kernel(u_ref, c2_ref, wbf_ref, wc_ref, a_ref, d_ref, o_ref,
               wt_ref, wb_ref, ut_ref, vt_ref, v_ref, xs_ref, xt_ref, yt_ref, yn_ref):
    nc = u_ref.shape[0] // S5_CHUNK
    gw = S5_GROUP_WIDTH
    tw = S5_CHUNK * gw
    ns = GROUPS_PER_BLOCK * S5_STATE

    lane = lax.broadcasted_iota(jnp.int32, (gw, tw), 1)
    for g in range(GROUPS_PER_BLOCK):
        wbf = wbf_ref[g]
        wb_ref[g] = wbf.astype(BF16)
        hrow = _dot_split(c2_ref[g], wbf)
        for t in range(S5_CHUNK):
            shift = (gw * (t + 1)) % tw
            rolled = pltpu.roll(hrow, shift, axis=1) if shift else hrow
            wt_ref[g, t * gw:(t + 1) * gw, :] = jnp.where(lane < gw * (t + 1), rolled, 0.0).astype(BF16)

    for s in range(S5_CHUNK):
        piece = u_ref[pl.ds(s, nc, stride=S5_CHUNK), :]
        ut_ref[s] = piece.T.astype(BF16)

    def chunk_inputs(g):
        return jnp.concatenate(
            [ut_ref[s, g * gw:(g + 1) * gw, :] for s in range(S5_CHUNK)], axis=0)

    for g in range(GROUPS_PER_BLOCK):
        vg = _dot(wb_ref[g], chunk_inputs(g))
        vt_ref[g * S5_STATE:(g + 1) * S5_STATE, :] = vg[:S5_STATE]
        vt_ref[ns + g * S5_STATE:ns + (g + 1) * S5_STATE, :] = vg[S5_STATE:]
    v_ref[...] = vt_ref[...].T

    ar = a_ref[0:1, :]
    ai = a_ref[1:2, :]

    def step(c, carry):
        xr, xi = carry
        xs_ref[pl.ds(c, 1), 0:ns] = xr
        xs_ref[pl.ds(c, 1), ns:2 * ns] = xi
        vr = v_ref[pl.ds(c, 1), 0:ns]
        vi = v_ref[pl.ds(c, 1), ns:2 * ns]
        return ar * xr - ai * xi + vr, ar * xi + ai * xr + vi

    zero = jnp.zeros((1, ns), F32)
    lax.fori_loop(0, nc, step, (zero, zero), unroll=8)
    xt_ref[...] = xs_ref[...].T.astype(BF16)

    for g in range(GROUPS_PER_BLOCK):
        xg = jnp.concatenate(
            [xt_ref[g * S5_STATE:(g + 1) * S5_STATE, :],
             xt_ref[ns + g * S5_STATE:ns + (g + 1) * S5_STATE, :]], axis=0)
        yg = _dot(wt_ref[g], chunk_inputs(g)) + _dot(wc_ref[g], xg)
        for t in range(S5_CHUNK):
            yt_ref[t, g * gw:(g + 1) * gw, :] = yg[t * gw:(t + 1) * gw, :]

    for t in range(S5_CHUNK):
        up = u_ref[pl.ds(t, nc, stride=S5_CHUNK), :]
        y = yt_ref[t].T + d_ref[...] * up
        yn_ref[pl.ds(t, nc, stride=S5_CHUNK), :] = jax.nn.gelu(y)
    o_ref[...] = yn_ref[...].astype(BF16)


def _s5_core(u, c2, wbf, wc, a16, dsk):
    n, sw = u.shape
    nb = sw // LANE
    nc = n // S5_CHUNK
    gb = GROUPS_PER_BLOCK
    ns2 = 2 * gb * S5_STATE
    tw = S5_CHUNK * S5_GROUP_WIDTH
    return pl.pallas_call(
        _s5_kernel,
        grid=(nb,),
        in_specs=[
            pl.BlockSpec((n, LANE), lambda j: (0, j)),
            pl.BlockSpec((gb,) + c2.shape[1:], lambda j: (j, 0, 0)),
            pl.BlockSpec((gb,) + wbf.shape[1:], lambda j: (j, 0, 0)),
            pl.BlockSpec((gb,) + wc.shape[1:], lambda j: (j, 0, 0)),
            pl.BlockSpec((None, 2, ns2 // 2), lambda j: (j, 0, 0)),
            pl.BlockSpec((1, LANE), lambda j: (0, j)),
        ],
        out_specs=pl.BlockSpec((n, LANE), lambda j: (0, j)),
        out_shape=jax.ShapeDtypeStruct((n, sw), BF16),
        scratch_shapes=[
            pltpu.VMEM((gb, tw, tw), BF16),
            pltpu.VMEM((gb,) + wbf.shape[1:], BF16),
            pltpu.VMEM((S5_CHUNK, LANE, nc), BF16),
            pltpu.VMEM((ns2, nc), F32),
            pltpu.VMEM((nc, ns2), F32),
            pltpu.VMEM((nc, ns2), F32),
            pltpu.VMEM((ns2, nc), BF16),
            pltpu.VMEM((S5_CHUNK, LANE, nc), F32),
            pltpu.VMEM((n, LANE), F32),
        ],
        compiler_params=pltpu.CompilerParams(
            dimension_semantics=("parallel",), vmem_limit_bytes=VMEM_LIMIT),
        name="s5_core",
    )(u, c2, wbf, wc, a16, dsk)


def _s5_weights(a_re, a_im, log_dt, b_re, b_im, c_re, c_im, d_skip):
    g, p = a_re.shape
    tc = S5_CHUNK
    dt = jnp.exp(log_dt)[:, None]
    decay = jnp.exp(a_re * dt)
    ab_re = decay * jnp.cos(a_im * dt)
    ab_im = decay * jnp.sin(a_im * dt)
    denom = a_re * a_re + a_im * a_im
    num_re = ab_re - 1.0
    num_im = ab_im
    k_re = (num_re * a_re + num_im * a_im) / denom
    k_im = (num_im * a_re - num_re * a_im) / denom
    bp_re = k_re[..., None] * b_re - k_im[..., None] * b_im
    bp_im = k_re[..., None] * b_im + k_im[..., None] * b_re

    def cmul(xr, xi, yr, yi):
        return xr * yr - xi * yi, xr * yi + xi * yr

    pw_re, pw_im = [jnp.ones_like(ab_re)], [jnp.zeros_like(ab_im)]
    for _ in range(tc):
        nr, ni = cmul(pw_re[-1], pw_im[-1], ab_re, ab_im)
        pw_re.append(nr)
        pw_im.append(ni)
    pw_re = jnp.stack(pw_re)
    pw_im = jnp.stack(pw_im)

    ab_r = pw_re[:tc, :, :, None] * bp_re[None] - pw_im[:tc, :, :, None] * bp_im[None]
    ab_i = pw_re[:tc, :, :, None] * bp_im[None] + pw_im[:tc, :, :, None] * bp_re[None]
    wb_r = jnp.transpose(ab_r[::-1], (1, 2, 0, 3)).reshape(g, p, tc * S5_GROUP_WIDTH)
    wb_i = jnp.transpose(ab_i[::-1], (1, 2, 0, 3)).reshape(g, p, tc * S5_GROUP_WIDTH)
    wb = jnp.concatenate([wb_r, wb_i], axis=1)
    ca_r = c_re[None] * pw_re[1:, :, None, :] - c_im[None] * pw_im[1:, :, None, :]
    ca_i = c_re[None] * pw_im[1:, :, None, :] + c_im[None] * pw_re[1:, :, None, :]
    wc = jnp.concatenate([ca_r, -ca_i], axis=-1)
    wc = jnp.transpose(wc, (1, 0, 2, 3)).reshape(g, tc * S5_GROUP_WIDTH, 2 * p)
    gb = GROUPS_PER_BLOCK
    a16 = jnp.stack([pw_re[tc].reshape(g // gb, gb * p), pw_im[tc].reshape(g // gb, gb * p)], axis=1)
    c2 = jnp.concatenate([c_re, -c_im], axis=-1)
    return c2, wb, wc.astype(BF16), a16, d_skip.reshape(1, -1)


def _merge_kernel(x_ref, h_ref, yap_ref, yb_ref, wglu_ref, bglu_ref, wga_ref, wgb_ref,
                  bga_ref, bgb_ref, wa_ref, wb_ref, wo_ref, o_ref, ya_ref, m_ref, *, nj):
    j = pl.program_id(1)

    @pl.when(j == 0)
    def _():
        yp = yap_ref[...]
        z = _dot(yp, wglu_ref[...]) + bglu_ref[...]
        ya_ref[...] = (yp.astype(F32) * jax.nn.sigmoid(z)).astype(BF16)

    @pl.when(j < nj)
    def _():
        h = h_ref[...]
        ga = jax.nn.sigmoid(_dot(h, wga_ref[...]) + bga_ref[...])
        gb = jax.nn.sigmoid(_dot(h, wgb_ref[...]) + bgb_ref[...])
        pa = _dot(ya_ref[...], wa_ref[...])
        pb = _dot(yb_ref[...], wb_ref[...])
        m_ref[j] = (ga * pa + gb * pb).astype(BF16)

    @pl.when(j >= nj)
    def _():
        m = jnp.concatenate([m_ref[k] for k in range(nj)], axis=1)
        o_ref[...] = x_ref[...] + _dot(m, wo_ref[j - nj])


def _merge(x, h, yap, yb, wglu, bglu, wgate, bgate, wa, wb, wo, *, tm=1024, tn=512):
    n, d = x.shape
    sw = yap.shape[1]
    nj = d // tn
    nt = n // tm
    wo = jnp.transpose(wo.reshape(d, nj, tn), (1, 0, 2))
    second = lambda j: jnp.maximum(j - nj, 0)

    def first(j, back):
        return jnp.where(j < nj, j, jnp.where(j < nj + back, nj - 1, 0))

    once = pl.Buffered(1)
    return pl.pallas_call(
        functools.partial(_merge_kernel, nj=nj),
        grid=(nt, 2 * nj),
        in_specs=[
            pl.BlockSpec((tm, tn), lambda i, j: (i, second(j))),
            pl.BlockSpec((tm, d), lambda i, j: (_next_tile_after(i, j, nj, nt), 0)),
            pl.BlockSpec((tm, sw), lambda i, j: (_next_tile_after(i, j, 1, nt), 0)),
            pl.BlockSpec((tm, sw), lambda i, j: (_next_tile_after(i, j, nj, nt), 0)),
            pl.BlockSpec((sw, sw), lambda i, j: (0, 0), pipeline_mode=once),
            pl.BlockSpec((1, sw), lambda i, j: (0, 0)),
            pl.BlockSpec((d, tn), lambda i, j: (0, first(j, 1))),
            pl.BlockSpec((d, tn), lambda i, j: (0, first(j, 2) + nj)),
            pl.BlockSpec((1, tn), lambda i, j: (0, first(j, 1))),
            pl.BlockSpec((1, tn), lambda i, j: (0, first(j, 2) + nj)),
            pl.BlockSpec((sw, tn), lambda i, j: (0, first(j, 3))),
            pl.BlockSpec((sw, tn), lambda i, j: (0, first(j, 3))),
            pl.BlockSpec((nj, d, tn), lambda i, j: (0, 0, 0), pipeline_mode=once),
        ],
        out_specs=pl.BlockSpec((tm, tn), lambda i, j: (i, second(j))),
        out_shape=jax.ShapeDtypeStruct((n, d), F32),
        scratch_shapes=[pltpu.VMEM((tm, sw), BF16), pltpu.VMEM((nj, tm, tn), BF16)],
        compiler_params=pltpu.CompilerParams(
            dimension_semantics=("parallel", "arbitrary"), vmem_limit_bytes=VMEM_LIMIT),
        name="merge",
    )(x, h, yap, yb, wglu, bglu, wgate, wgate, bgate, bgate, wa, wb, wo)


def kernel(x, ffn1_norm, ffn1_w_gate, ffn1_w_up, ffn1_w_down, mix_norm, w_in, s5_a_re, s5_a_im, s5_log_dt, s5_b_re, s5_b_im, s5_c_re, s5_c_im, s5_d, s5_w_glu, s5_b_glu, sgu_ln_g, sgu_ln_b, sgu_w_s, sgu_b_s, w_branch_a, w_branch_b, w_gate, b_gate, w_out, ffn2_norm, ffn2_w_gate, ffn2_w_up, ffn2_w_down, final_norm):
    bsz, seq, d = x.shape
    depth = ffn1_norm.shape[0]
    sw = s5_w_glu.shape[1]
    assert bsz == 1 and sw == w_branch_b.shape[1] and sw % LANE == 0
    assert seq % (S5_CHUNK * LANE) == 0 and sgu_w_s.shape[2] == SGU_CHUNK
    bf = lambda w: w.astype(BF16)
    row = lambda v: v.reshape(1, -1)
    xs = x.reshape(seq, d)
    fin = row(final_norm)
    for i in range(depth):
        xs = _ffn(xs, row(ffn1_norm[i]), ffn1_w_gate[i], ffn1_w_up[i], ffn1_w_down[i], fin, final=False)
        bsb = jnp.repeat(jnp.transpose(sgu_b_s[i]), sw // SGU_HEADS, axis=1)
        ua, h, yb = _mix_in(xs, row(mix_norm[i]), bf(w_in[i]), row(sgu_ln_g[i]), row(sgu_ln_b[i]),
                            sgu_w_s[i], bsb, sw=sw)
        c2, wbf, wc, a16, dsk = _s5_weights(s5_a_re[i], s5_a_im[i], s5_log_dt[i], s5_b_re[i], s5_b_im[i],
                                            s5_c_re[i], s5_c_im[i], s5_d[i])
        yap = _s5_core(ua, c2, wbf, wc, a16, dsk)
        xs = _merge(xs, h, yap, yb, bf(s5_w_glu[i]), row(s5_b_glu[i]), bf(w_gate[i]), row(b_gate[i]),
                    bf(w_branch_a[i]), bf(w_branch_b[i]), bf(w_out[i]))
        xs = _ffn(xs, row(ffn2_norm[i]), ffn2_w_gate[i], ffn2_w_up[i], ffn2_w_down[i], fin,
                  final=(i == depth - 1))
    return xs.reshape(bsz, seq, d)
```

```python
import functools

import jax
import jax.numpy as jnp
from jax import lax
from jax.experimental import pallas as pl
from jax.experimental.pallas import tpu as pltpu

F32 = jnp.float32
BF16 = jnp.bfloat16

NORM_EPS = 1e-6
S5_GROUP_WIDTH = 16
S5_STATE = 64
SGU_HEADS = 8
SGU_CHUNK = 128
SGU_CAUSAL_BLOCK = 64
S5_CHUNK = 16
LANE = 128
GROUPS_PER_BLOCK = LANE // S5_GROUP_WIDTH
VMEM_LIMIT = 60 * 1024 * 1024
ROW_CHUNK = 256


def _rms(x, g):
    ms = jnp.mean(x * x, axis=-1, keepdims=True)
    return (x * lax.rsqrt(ms + NORM_EPS)) * g


def _dot(a, b):
    return jnp.dot(a, b, preferred_element_type=F32)


def _ffn_kernel(x_ref, g_ref, wg_ref, wu_ref, wd_ref, fin_ref, o_ref, h_ref, *, final):
    j = pl.program_id(1)
    n_chunks = x_ref.shape[0] // ROW_CHUNK

    @pl.when(j == 0)
    def _():
        g = g_ref[...]

        def rows(r, carry):
            rs = pl.ds(pl.multiple_of(r * ROW_CHUNK, ROW_CHUNK), ROW_CHUNK)
            x = x_ref[rs, :]
            h_ref[rs, :] = _rms(x, g).astype(BF16)
            o_ref[rs, :] = x
            return carry

        lax.fori_loop(0, n_chunks, rows, 0, unroll=2)

    h = h_ref[...]
    a = _dot(h, wg_ref[...].astype(BF16))
    b = _dot(h, wu_ref[...].astype(BF16))
    hid = (a * jax.nn.sigmoid(a)) * b
    o_ref[...] += _dot(hid.astype(BF16), (0.5 * wd_ref[...]).astype(BF16))

    if final:
        @pl.when(j == pl.num_programs(1) - 1)
        def _():
            fin = fin_ref[...]

            def rows(r, carry):
                rs = pl.ds(pl.multiple_of(r * ROW_CHUNK, ROW_CHUNK), ROW_CHUNK)
                o_ref[rs, :] = _rms(o_ref[rs, :], fin)
                return carry

            lax.fori_loop(0, n_chunks, rows, 0, unroll=2)


def _next_tile_after(i, j, first_step, n_tiles):
    return jnp.minimum(i + jnp.where(j >= first_step, 1, 0), n_tiles - 1)


def _ffn(x, g, wg, wu, wd, fin, *, final, tm=1024, tf=256):
    n, d = x.shape
    dff = wg.shape[1]
    nt = n // tm
    return pl.pallas_call(
        functools.partial(_ffn_kernel, final=final),
        grid=(nt, dff // tf),
        in_specs=[
            pl.BlockSpec((tm, d), lambda i, j: (i, 0)),
            pl.BlockSpec((1, d), lambda i, j: (0, 0)),
            pl.BlockSpec((d, tf), lambda i, j: (0, j)),
            pl.BlockSpec((d, tf), lambda i, j: (0, j)),
            pl.BlockSpec((tf, d), lambda i, j: (j, 0)),
            pl.BlockSpec((1, d), lambda i, j: (0, 0)),
        ],
        out_specs=pl.BlockSpec((tm, d), lambda i, j: (i, 0)),
        out_shape=jax.ShapeDtypeStruct((n, d), F32),
        scratch_shapes=[pltpu.VMEM((tm, d), BF16)],
        compiler_params=pltpu.CompilerParams(
            dimension_semantics=("parallel", "arbitrary"), vmem_limit_bytes=VMEM_LIMIT),
        name="ffn_final" if final else "ffn",
    )(x, g, wg, wu, wd, fin)


def _mix_in_kernel(x_ref, g_ref, w_ref, lng_ref, lnb_ref, ws_ref, bs_ref,
                   ua_ref, h_ref, yb_ref, ug_ref, *, sw):
    j = pl.program_id(1)

    @pl.when(j == 0)
    def _():
        h = _rms(x_ref[...], g_ref[...]).astype(BF16)
        h_ref[...] = h
        ua_ref[...] = _dot(h, w_ref[...])

    @pl.when(j == 1)
    def _():
        ug_ref[...] = jax.nn.gelu(_dot(h_ref[...], w_ref[...]))

    @pl.when(j == 2)
    def _():
        tm = ug_ref.shape[0]
        v = jax.nn.gelu(_dot(h_ref[...], w_ref[...]))
        mu = jnp.mean(v, axis=-1, keepdims=True)
        vc = v - mu
        var = jnp.mean(vc * vc, axis=-1, keepdims=True)
        vn = ((vc * lax.rsqrt(var + NORM_EPS)) * lng_ref[...] + lnb_ref[...]).astype(BF16)
        r = lax.broadcasted_iota(jnp.int32, (SGU_CHUNK, SGU_CHUNK), 0) // SGU_CAUSAL_BLOCK
        c = lax.broadcasted_iota(jnp.int32, (SGU_CHUNK, SGU_CHUNK), 1) // SGU_CAUSAL_BLOCK
        keep = r >= c
        hd = sw // SGU_HEADS
        for h in range(SGU_HEADS):
            wsm = jnp.where(keep, ws_ref[h], 0.0).astype(BF16)
            cols = slice(h * hd, (h + 1) * hd)
            for q in range(tm // SGU_CHUNK):
                rows = slice(q * SGU_CHUNK, (q + 1) * SGU_CHUNK)
                mixed = _dot(wsm, vn[rows, cols]) + bs_ref[:, cols]
                yb_ref[rows, cols] = (ug_ref[rows, cols] * mixed).astype(BF16)


def _mix_in(x, g, w_in, ln_g, ln_b, w_s, bsb, *, sw, tm=512):
    n, d = x.shape
    assert w_in.shape[1] == 3 * sw
    nt = n // tm
    return pl.pallas_call(
        functools.partial(_mix_in_kernel, sw=sw),
        grid=(nt, 3),
        in_specs=[
            pl.BlockSpec((tm, d), lambda i, j: (i, 0)),
            pl.BlockSpec((1, d), lambda i, j: (0, 0)),
            pl.BlockSpec((d, sw), lambda i, j: (0, j)),
            pl.BlockSpec((1, sw), lambda i, j: (0, 0)),
            pl.BlockSpec((1, sw), lambda i, j: (0, 0)),
            pl.BlockSpec(w_s.shape, lambda i, j: (0, 0, 0)),
            pl.BlockSpec(bsb.shape, lambda i, j: (0, 0)),
        ],
        out_specs=[
            pl.BlockSpec((tm, sw), lambda i, j: (i, 0)),
            pl.BlockSpec((tm, d), lambda i, j: (i, 0)),
            pl.BlockSpec((tm, sw), lambda i, j: (i, 0)),
        ],
        out_shape=[
            jax.ShapeDtypeStruct((n, sw), F32),
            jax.ShapeDtypeStruct((n, d), BF16),
            jax.ShapeDtypeStruct((n, sw), BF16),
        ],
        scratch_shapes=[pltpu.VMEM((tm, sw), F32)],
        compiler_params=pltpu.CompilerParams(
            dimension_semantics=("parallel", "arbitrary"), vmem_limit_bytes=VMEM_LIMIT),
        name="mix_in",
    )(x, g, w_in, ln_g, ln_b, w_s, bsb)


def _dot_split(a, b):
    a_hi = a.astype(BF16)
    b_hi = b.astype(BF16)
    a_lo = (a - a_hi.astype(F32)).astype(BF16)
    b_lo = (b - b_hi.astype(F32)).astype(BF16)
    return _dot(a_hi, b_hi) + (_dot(a_hi, b_lo) + _dot(a_lo, b_hi))


def _s5_kernel(u_ref, c2_ref, wbf_ref, wc_ref, a_ref, d_ref, o_ref,
               wt_ref, wb_ref, ut_ref, vt_ref, v_ref, xs_ref, xt_ref, yt_ref, yn_ref):
    nc = u_ref.shape[0] // S5_CHUNK
    gw = S5_GROUP_WIDTH
    tw = S5_CHUNK * gw
    ns = GROUPS_PER_BLOCK * S5_STATE

    lane = lax.broadcasted_iota(jnp.int32, (gw, tw), 1)
    for g in range(GROUPS_PER_BLOCK):
        wbf = wbf_ref[g]
        wb_ref[g] = wbf.astype(BF16)
        hrow = _dot_split(c2_ref[g], wbf)
        for t in range(S5_CHUNK):
            shift = (gw * (t + 1)) % tw
            rolled = pltpu.roll(hrow, shift, axis=1) if shift else hrow
            wt_ref[g, t * gw:(t + 1) * gw, :] = jnp.where(lane < gw * (t + 1), rolled, 0.0).astype(BF16)

    for s in range(S5_CHUNK):
        piece = u_ref[pl.ds(s, nc, stride=S5_CHUNK), :]
        ut_ref[s] = piece.T.astype(BF16)

    def chunk_inputs(g):
        return jnp.concatenate(
            [ut_ref[s, g * gw:(g + 1) * gw, :] for s in range(S5_CHUNK)], axis=0)

    for g in range(GROUPS_PER_BLOCK):
        vg = _dot(wb_ref[g], chunk_inputs(g))
        vt_ref[g * S5_STATE:(g + 1) * S5_STATE, :] = vg[:S5_STATE]
        vt_ref[ns + g * S5_STATE:ns + (g + 1) * S5_STATE, :] = vg[S5_STATE:]
    v_ref[...] = vt_ref[...].T

    ar = a_ref[0:1, :]
    ai = a_ref[1:2, :]

    def step(c, carry):
        xr, xi = carry
        xs_ref[pl.ds(c, 1), 0:ns] = xr
        xs_ref[pl.ds(c, 1), ns:2 * ns] = xi
        vr = v_ref[pl.ds(c, 1), 0:ns]
        vi = v_ref[pl.ds(c, 1), ns:2 * ns]
        return ar * xr - ai * xi + vr, ar * xi + ai * xr + vi

    zero = jnp.zeros((1, ns), F32)
    lax.fori_loop(0, nc, step, (zero, zero), unroll=8)
    xt_ref[...] = xs_ref[...].T.astype(BF16)

    for g in range(GROUPS_PER_BLOCK):
        xg = jnp.concatenate(
            [xt_ref[g * S5_STATE:(g + 1) * S5_STATE, :],
             xt_ref[ns + g * S5_STATE:ns + (g + 1) * S5_STATE, :]], axis=0)
        yg = _dot(wt_ref[g], chunk_inputs(g)) + _dot(wc_ref[g], xg)
        for t in range(S5_CHUNK):
            yt_ref[t, g * gw:(g + 1) * gw, :] = yg[t * gw:(t + 1) * gw, :]

    for t in range(S5_CHUNK):
        up = u_ref[pl.ds(t, nc, stride=S5_CHUNK), :]
        y = yt_ref[t].T + d_ref[...] * up
        yn_ref[pl.ds(t, nc, stride=S5_CHUNK), :] = jax.nn.gelu(y)
    o_ref[...] = yn_ref[...].astype(BF16)


def _s5_core(u, c2, wbf, wc, a16, dsk):
    n, sw = u.shape
    nb = sw // LANE
    nc = n // S5_CHUNK
    gb = GROUPS_PER_BLOCK
    ns2 = 2 * gb * S5_STATE
    tw = S5_CHUNK * S5_GROUP_WIDTH
    return pl.pallas_call(
        _s5_kernel,
        grid=(nb,),
        in_specs=[
            pl.BlockSpec((n, LANE), lambda j: (0, j)),
            pl.BlockSpec((gb,) + c2.shape[1:], lambda j: (j, 0, 0)),
            pl.BlockSpec((gb,) + wbf.shape[1:], lambda j: (j, 0, 0)),
            pl.BlockSpec((gb,) + wc.shape[1:], lambda j: (j, 0, 0)),
            pl.BlockSpec((None, 2, ns2 // 2), lambda j: (j, 0, 0)),
            pl.BlockSpec((1, LANE), lambda j: (0, j)),
        ],
        out_specs=pl.BlockSpec((n, LANE), lambda j: (0, j)),
        out_shape=jax.ShapeDtypeStruct((n, sw), BF16),
        scratch_shapes=[
            pltpu.VMEM((gb, tw, tw), BF16),
            pltpu.VMEM((gb,) + wbf.shape[1:], BF16),
            pltpu.VMEM((S5_CHUNK, LANE, nc), BF16),
            pltpu.VMEM((ns2, nc), F32),
            pltpu.VMEM((nc, ns2), F32),
            pltpu.VMEM((nc, ns2), F32),
            pltpu.VMEM((ns2, nc), BF16),
            pltpu.VMEM((S5_CHUNK, LANE, nc), F32),
            pltpu.VMEM((n, LANE), F32),
        ],
        compiler_params=pltpu.CompilerParams(
            dimension_semantics=("parallel",), vmem_limit_bytes=VMEM_LIMIT),
        name="s5_core",
    )(u, c2, wbf, wc, a16, dsk)


def _s5_weights(a_re, a_im, log_dt, b_re, b_im, c_re, c_im, d_skip):
    g, p = a_re.shape
    tc = S5_CHUNK
    dt = jnp.exp(log_dt)[:, None]
    decay = jnp.exp(a_re * dt)
    ab_re = decay * jnp.cos(a_im * dt)
    ab_im = decay * jnp.sin(a_im * dt)
    denom = a_re * a_re + a_im * a_im
    num_re = ab_re - 1.0
    num_im = ab_im
    k_re = (num_re * a_re + num_im * a_im) / denom
    k_im = (num_im * a_re - num_re * a_im) / denom
    bp_re = k_re[..., None] * b_re - k_im[..., None] * b_im
    bp_im = k_re[..., None] * b_im + k_im[..., None] * b_re

    def cmul(xr, xi, yr, yi):
        return xr * yr - xi * yi, xr * yi + xi * yr

    pw_re, pw_im = [jnp.ones_like(ab_re)], [jnp.zeros_like(ab_im)]
    for _ in range(tc):
        nr, ni = cmul(pw_re[-1], pw_im[-1], ab_re, ab_im)
        pw_re.append(nr)
        pw_im.append(ni)
    dn_re = jnp.stack(pw_re[tc - 1::-1], axis=2)[..., None]
    dn_im = jnp.stack(pw_im[tc - 1::-1], axis=2)[..., None]
    br, bi = bp_re[:, :, None, :], bp_im[:, :, None, :]
    wb_r = (dn_re * br - dn_im * bi).reshape(g, p, tc * S5_GROUP_WIDTH)
    wb_i = (dn_re * bi + dn_im * br).reshape(g, p, tc * S5_GROUP_WIDTH)
    wb = jnp.concatenate([wb_r, wb_i], axis=1)
    up_re = jnp.stack(pw_re[1:], axis=1)[:, :, None, :]
    up_im = jnp.stack(pw_im[1:], axis=1)[:, :, None, :]
    cr, ci = c_re[:, None], c_im[:, None]
    wc = jnp.concatenate([cr * up_re - ci * up_im, -(cr * up_im + ci * up_re)], axis=-1)
    wc = wc.reshape(g, tc * S5_GROUP_WIDTH, 2 * p)
    gb = GROUPS_PER_BLOCK
    a16 = jnp.stack([pw_re[tc].reshape(g // gb, gb * p), pw_im[tc].reshape(g // gb, gb * p)], axis=1)
    c2 = jnp.concatenate([c_re, -c_im], axis=-1)
    return c2, wb, wc.astype(BF16), a16, d_skip.reshape(1, -1)


def _merge_kernel(x_ref, h_ref, yap_ref, yb_ref, wglu_ref, bglu_ref, wga_ref, wgb_ref,
                  bga_ref, bgb_ref, wa_ref, wb_ref, *rest, nj):
    wo_refs = rest[:nj]
    o_ref, ya_ref, m_ref = rest[nj:]
    j = pl.program_id(1)

    @pl.when(j == 0)
    def _():
        yp = yap_ref[...]
        z = _dot(yp, wglu_ref[...]) + bglu_ref[...]
        ya_ref[...] = (yp.astype(F32) * jax.nn.sigmoid(z)).astype(BF16)

    @pl.when(j < nj)
    def _():
        h = h_ref[...]
        ga = jax.nn.sigmoid(_dot(h, wga_ref[...]) + bga_ref[...])
        gb = jax.nn.sigmoid(_dot(h, wgb_ref[...]) + bgb_ref[...])
        pa = _dot(ya_ref[...], wa_ref[...])
        pb = _dot(yb_ref[...], wb_ref[...])
        m_ref[j] = (ga * pa + gb * pb).astype(BF16)

    for k in range(nj):
        @pl.when(j == nj + k)
        def _(k=k):
            m = jnp.concatenate([m_ref[q] for q in range(nj)], axis=1)
            o_ref[...] = x_ref[...] + _dot(m, wo_refs[k][...])


def _merge(x, h, yap, yb, wglu, bglu, wgate, bgate, wa, wb, wo, *, tm=1024, tn=512):
    n, d = x.shape
    sw = yap.shape[1]
    nj = d // tn
    nt = n // tm
    second = lambda j: jnp.maximum(j - nj, 0)

    def first(j, back):
        return jnp.where(j < nj, j, jnp.where(j < nj + back, nj - 1, 0))

    once = pl.Buffered(1)
    return pl.pallas_call(
        functools.partial(_merge_kernel, nj=nj),
        grid=(nt, 2 * nj),
        in_specs=[
            pl.BlockSpec((tm, tn), lambda i, j: (i, second(j))),
            pl.BlockSpec((tm, d), lambda i, j: (_next_tile_after(i, j, nj, nt), 0)),
            pl.BlockSpec((tm, sw), lambda i, j: (_next_tile_after(i, j, 1, nt), 0)),
            pl.BlockSpec((tm, sw), lambda i, j: (_next_tile_after(i, j, nj, nt), 0)),
            pl.BlockSpec((sw, sw), lambda i, j: (0, 0), pipeline_mode=once),
            pl.BlockSpec((1, sw), lambda i, j: (0, 0)),
            pl.BlockSpec((d, tn), lambda i, j: (0, first(j, 1))),
            pl.BlockSpec((d, tn), lambda i, j: (0, first(j, 2) + nj)),
            pl.BlockSpec((1, tn), lambda i, j: (0, first(j, 1))),
            pl.BlockSpec((1, tn), lambda i, j: (0, first(j, 2) + nj)),
            pl.BlockSpec((sw, tn), lambda i, j: (0, first(j, 3))),
            pl.BlockSpec((sw, tn), lambda i, j: (0, first(j, 3))),
        ] + [pl.BlockSpec((d, tn), lambda i, j, k=k: (0, k), pipeline_mode=once) for k in range(nj)],
        out_specs=pl.BlockSpec((tm, tn), lambda i, j: (i, second(j))),
        out_shape=jax.ShapeDtypeStruct((n, d), F32),
        scratch_shapes=[pltpu.VMEM((tm, sw), BF16), pltpu.VMEM((nj, tm, tn), BF16)],
        compiler_params=pltpu.CompilerParams(
            dimension_semantics=("parallel", "arbitrary"), vmem_limit_bytes=VMEM_LIMIT),
        name="merge",
    )(x, h, yap, yb, wglu, bglu, wgate, wgate, bgate, bgate, wa, wb, *([wo] * nj))


def kernel(x, ffn1_norm, ffn1_w_gate, ffn1_w_up, ffn1_w_down, mix_norm, w_in, s5_a_re, s5_a_im, s5_log_dt, s5_b_re, s5_b_im, s5_c_re, s5_c_im, s5_d, s5_w_glu, s5_b_glu, sgu_ln_g, sgu_ln_b, sgu_w_s, sgu_b_s, w_branch_a, w_branch_b, w_gate, b_gate, w_out, ffn2_norm, ffn2_w_gate, ffn2_w_up, ffn2_w_down, final_norm):
    bsz, seq, d = x.shape
    depth = ffn1_norm.shape[0]
    sw = s5_w_glu.shape[1]
    assert bsz == 1 and sw == w_branch_b.shape[1] and sw % LANE == 0
    assert seq % (S5_CHUNK * LANE) == 0 and sgu_w_s.shape[2] == SGU_CHUNK
    bf = lambda w: w.astype(BF16)
    row = lambda v: v.reshape(1, -1)
    xs = x.reshape(seq, d)
    fin = row(final_norm)
    for i in range(depth):
        xs = _ffn(xs, row(ffn1_norm[i]), ffn1_w_gate[i], ffn1_w_up[i], ffn1_w_down[i], fin, final=False)
        bsb = jnp.repeat(jnp.transpose(sgu_b_s[i]), sw // SGU_HEADS, axis=1)
        ua, h, yb = _mix_in(xs, row(mix_norm[i]), bf(w_in[i]), row(sgu_ln_g[i]), row(sgu_ln_b[i]),
                            sgu_w_s[i], bsb, sw=sw)
        c2, wbf, wc, a16, dsk = _s5_weights(s5_a_re[i], s5_a_im[i], s5_log_dt[i], s5_b_re[i], s5_b_im[i],
                                            s5_c_re[i], s5_c_im[i], s5_d[i])
        yap = _s5_core(ua, c2, wbf, wc, a16, dsk)
        xs = _merge(xs, h, yap, yb, bf(s5_w_glu[i]), row(s5_b_glu[i]), bf(w_gate[i]), row(b_gate[i]),
                    bf(w_branch_a[i]), bf(w_branch_b[i]), bf(w_out[i]))
        xs = _ffn(xs, row(ffn2_norm[i]), ffn2_w_gate[i], ffn2_w_up[i], ffn2_w_down[i], fin,
                  final=(i == depth - 1))
    return xs.reshape(bsz, seq, d)
```

```python
import functools

import jax
import jax.numpy as jnp
from jax import lax
from jax.experimental import pallas as pl
from jax.experimental.pallas import tpu as pltpu

F32 = jnp.float32
BF16 = jnp.bfloat16

NORM_EPS = 1e-6
S5_GROUP_WIDTH = 16
S5_STATE = 64
SGU_HEADS = 8
SGU_CHUNK = 128
SGU_CAUSAL_BLOCK = 64
S5_CHUNK = 16
LANE = 128
GROUPS_PER_BLOCK = LANE // S5_GROUP_WIDTH
VMEM_LIMIT = 60 * 1024 * 1024
ROW_CHUNK = 256


def _rms(x, g):
    ms = jnp.mean(x * x, axis=-1, keepdims=True)
    return (x * lax.rsqrt(ms + NORM_EPS)) * g


def _dot(a, b):
    return jnp.dot(a, b, preferred_element_type=F32)


def _ffn_kernel(x_ref, g_ref, wg_ref, wu_ref, wd_ref, fin_ref, o_ref, h_ref, *, final):
    j = pl.program_id(1)
    n_chunks = x_ref.shape[0] // ROW_CHUNK

    @pl.when(j == 0)
    def _():
        g = g_ref[...]

        def rows(r, carry):
            rs = pl.ds(pl.multiple_of(r * ROW_CHUNK, ROW_CHUNK), ROW_CHUNK)
            x = x_ref[rs, :]
            h_ref[rs, :] = _rms(x, g).astype(BF16)
            o_ref[rs, :] = x
            return carry

        lax.fori_loop(0, n_chunks, rows, 0, unroll=2)

    h = h_ref[...]
    a = _dot(h, wg_ref[...].astype(BF16))
    b = _dot(h, wu_ref[...].astype(BF16))
    hid = (a * jax.nn.sigmoid(a)) * b
    o_ref[...] += _dot(hid.astype(BF16), (0.5 * wd_ref[...]).astype(BF16))

    if final:
        @pl.when(j == pl.num_programs(1) - 1)
        def _():
            fin = fin_ref[...]

            def rows(r, carry):
                rs = pl.ds(pl.multiple_of(r * ROW_CHUNK, ROW_CHUNK), ROW_CHUNK)
                o_ref[rs, :] = _rms(o_ref[rs, :], fin)
                return carry

            lax.fori_loop(0, n_chunks, rows, 0, unroll=2)


def _next_tile_after(i, j, first_step, n_tiles):
    return jnp.minimum(i + jnp.where(j >= first_step, 1, 0), n_tiles - 1)


def _ffn(x, g, wg, wu, wd, fin, *, final, tm=1024, tf=256):
    n, d = x.shape
    dff = wg.shape[1]
    nt = n // tm
    return pl.pallas_call(
        functools.partial(_ffn_kernel, final=final),
        grid=(nt, dff // tf),
        in_specs=[
            pl.BlockSpec((tm, d), lambda i, j: (i, 0)),
            pl.BlockSpec((1, d), lambda i, j: (0, 0)),
            pl.BlockSpec((d, tf), lambda i, j: (0, j)),
            pl.BlockSpec((d, tf), lambda i, j: (0, j)),
            pl.BlockSpec((tf, d), lambda i, j: (j, 0)),
            pl.BlockSpec((1, d), lambda i, j: (0, 0)),
        ],
        out_specs=pl.BlockSpec((tm, d), lambda i, j: (i, 0)),
        out_shape=jax.ShapeDtypeStruct((n, d), F32),
        scratch_shapes=[pltpu.VMEM((tm, d), BF16)],
        compiler_params=pltpu.CompilerParams(
            dimension_semantics=("parallel", "arbitrary"), vmem_limit_bytes=VMEM_LIMIT),
        name="ffn_final" if final else "ffn",
    )(x, g, wg, wu, wd, fin)


def _mix_in_kernel(x_ref, g_ref, w_ref, lng_ref, lnb_ref, ws_ref, bs_ref,
                   ua_ref, h_ref, yb_ref, vn_ref, *, sw):
    j = pl.program_id(1)

    @pl.when(j == 0)
    def _():
        h = _rms(x_ref[...], g_ref[...]).astype(BF16)
        h_ref[...] = h
        ua_ref[...] = _dot(h, w_ref[...])

    @pl.when(j == 1)
    def _():
        v = jax.nn.gelu(_dot(h_ref[...], w_ref[...]))
        mu = jnp.mean(v, axis=-1, keepdims=True)
        vc = v - mu
        var = jnp.mean(vc * vc, axis=-1, keepdims=True)
        vn_ref[...] = ((vc * lax.rsqrt(var + NORM_EPS)) * lng_ref[...] + lnb_ref[...]).astype(BF16)

    @pl.when(j == 2)
    def _():
        tm = vn_ref.shape[0]
        u = jax.nn.gelu(_dot(h_ref[...], w_ref[...]))
        r = lax.broadcasted_iota(jnp.int32, (SGU_CHUNK, SGU_CHUNK), 0) // SGU_CAUSAL_BLOCK
        c = lax.broadcasted_iota(jnp.int32, (SGU_CHUNK, SGU_CHUNK), 1) // SGU_CAUSAL_BLOCK
        keep = r >= c
        hd = sw // SGU_HEADS
        for h in range(SGU_HEADS):
            wsm = jnp.where(keep, ws_ref[h], 0.0).astype(BF16)
            cols = slice(h * hd, (h + 1) * hd)
            for q in range(tm // SGU_CHUNK):
                rows = slice(q * SGU_CHUNK, (q + 1) * SGU_CHUNK)
                mixed = _dot(wsm, vn_ref[rows, cols]) + bs_ref[:, cols]
                yb_ref[rows, cols] = (u[rows, cols] * mixed).astype(BF16)


def _mix_in(x, g, w_in, ln_g, ln_b, w_s, bsb, *, sw, tm=512):
    n, d = x.shape
    assert w_in.shape[1] == 3 * sw
    nt = n // tm
    return pl.pallas_call(
        functools.partial(_mix_in_kernel, sw=sw),
        grid=(nt, 3),
        in_specs=[
            pl.BlockSpec((tm, d), lambda i, j: (i, 0)),
            pl.BlockSpec((1, d), lambda i, j: (0, 0)),
            pl.BlockSpec((d, sw), lambda i, j: (0, jnp.where(j == 0, 0, 3 - j))),
            pl.BlockSpec((1, sw), lambda i, j: (0, 0)),
            pl.BlockSpec((1, sw), lambda i, j: (0, 0)),
            pl.BlockSpec(w_s.shape, lambda i, j: (0, 0, 0)),
            pl.BlockSpec(bsb.shape, lambda i, j: (0, 0)),
        ],
        out_specs=[
            pl.BlockSpec((tm, sw), lambda i, j: (i, 0)),
            pl.BlockSpec((tm, d), lambda i, j: (i, 0)),
            pl.BlockSpec((tm, sw), lambda i, j: (i, 0)),
        ],
        out_shape=[
            jax.ShapeDtypeStruct((n, sw), F32),
            jax.ShapeDtypeStruct((n, d), BF16),
            jax.ShapeDtypeStruct((n, sw), BF16),
        ],
        scratch_shapes=[pltpu.VMEM((tm, sw), BF16)],
        compiler_params=pltpu.CompilerParams(
            dimension_semantics=("parallel", "arbitrary"), vmem_limit_bytes=VMEM_LIMIT),
        name="mix_in",
    )(x, g, w_in, ln_g, ln_b, w_s, bsb)


def _dot_split(a, b):
    a_hi = a.astype(BF16)
    b_hi = b.astype(BF16)
    a_lo = (a - a_hi.astype(F32)).astype(BF16)
    b_lo = (b - b_hi.astype(F32)).astype(BF16)
    return _dot(a_hi, b_hi) + (_dot(a_hi, b_lo) + _dot(a_lo, b_hi))


def _expand(x, sel):
    x1 = x.astype(BF16)
    r1 = x - x1.astype(F32)
    x2 = r1.astype(BF16)
    x3 = (r1 - x2.astype(F32)).astype(BF16)
    return _dot(x1, sel) + (_dot(x2, sel) + _dot(x3, sel))


def _s5_kernel(u_ref, c2_ref, cb_ref, dn_ref, bp_ref, upr_ref, upi_ref, rep_ref, til_ref, a_ref, d_ref, o_ref,
               wt_ref, wb_ref, wc_ref, up_ref, ut_ref, vt_ref, v_ref, xs_ref, xt_ref, yt_ref, yn_ref):
    nc = u_ref.shape[0] // S5_CHUNK
    gw = S5_GROUP_WIDTH
    tw = S5_CHUNK * gw
    ns = GROUPS_PER_BLOCK * S5_STATE

    lane = lax.broadcasted_iota(jnp.int32, (gw, tw), 1)
    for g in range(GROUPS_PER_BLOCK):
        pw = _expand(dn_ref[g], rep_ref[...])
        bt = _expand(bp_ref[g], til_ref[...])
        pr, pi = pw[:S5_STATE], pw[S5_STATE:]
        br, bi = bt[:S5_STATE], bt[S5_STATE:]
        wbf = jnp.concatenate([pr * br - pi * bi, pr * bi + pi * br], axis=0)
        wb_ref[g] = wbf.astype(BF16)
        hrow = _dot_split(c2_ref[g], wbf)
        ca, cb = c2_ref[g], cb_ref[g]
        for t in range(S5_CHUNK):
            shift = (gw * (t + 1)) % tw
            rolled = pltpu.roll(hrow, shift, axis=1) if shift else hrow
            wt_ref[g, t * gw:(t + 1) * gw, :] = jnp.where(lane < gw * (t + 1), rolled, 0.0).astype(BF16)
            wc_ref[g, t * gw:(t + 1) * gw, :] = (
                ca * upr_ref[g, t:t + 1, :] + cb * upi_ref[g, t:t + 1, :]).astype(BF16)

    for s in range(S5_CHUNK):
        piece = u_ref[pl.ds(s, nc, stride=S5_CHUNK), :]
        up_ref[s] = piece
        ut_ref[s] = piece.T.astype(BF16)

    def chunk_inputs(g):
        return jnp.concatenate(
            [ut_ref[s, g * gw:(g + 1) * gw, :] for s in range(S5_CHUNK)], axis=0)

    for g in range(GROUPS_PER_BLOCK):
        vg = _dot(wb_ref[g], chunk_inputs(g))
        vt_ref[g * S5_STATE:(g + 1) * S5_STATE, :] = vg[:S5_STATE]
        vt_ref[ns + g * S5_STATE:ns + (g + 1) * S5_STATE, :] = vg[S5_STATE:]
    v_ref[...] = vt_ref[...].T

    ar = a_ref[0:1, :]
    ai = a_ref[1:2, :]

    def step(c, carry):
        xr, xi = carry
        xs_ref[pl.ds(c, 1), 0:ns] = xr
        xs_ref[pl.ds(c, 1), ns:2 * ns] = xi
        vr = v_ref[pl.ds(c, 1), 0:ns]
        vi = v_ref[pl.ds(c, 1), ns:2 * ns]
        return ar * xr - ai * xi + vr, ar * xi + ai * xr + vi

    zero = jnp.zeros((1, ns), F32)
    lax.fori_loop(0, nc, step, (zero, zero), unroll=8)
    xt_ref[...] = xs_ref[...].T.astype(BF16)

    for g in range(GROUPS_PER_BLOCK):
        xg = jnp.concatenate(
            [xt_ref[g * S5_STATE:(g + 1) * S5_STATE, :],
             xt_ref[ns + g * S5_STATE:ns + (g + 1) * S5_STATE, :]], axis=0)
        yg = _dot(wt_ref[g], chunk_inputs(g)) + _dot(wc_ref[g], xg)
        for t in range(S5_CHUNK):
            yt_ref[t, g * gw:(g + 1) * gw, :] = yg[t * gw:(t + 1) * gw, :]

    for t in range(S5_CHUNK):
        y = yt_ref[t].T + d_ref[...] * up_ref[t]
        yn_ref[pl.ds(t, nc, stride=S5_CHUNK), :] = jax.nn.gelu(y)
    o_ref[...] = yn_ref[...].astype(BF16)


def _s5_core(u, c2, cb, dn, bp, upr, upi, rep, til, a16, dsk):
    n, sw = u.shape
    nb = sw // LANE
    nc = n // S5_CHUNK
    gb = GROUPS_PER_BLOCK
    ns2 = 2 * gb * S5_STATE
    tw = S5_CHUNK * S5_GROUP_WIDTH
    per_group = lambda a: pl.BlockSpec((gb,) + a.shape[1:], lambda j: (j, 0, 0))
    whole = lambda a: pl.BlockSpec(a.shape, lambda j: (0, 0))
    return pl.pallas_call(
        _s5_kernel,
        grid=(nb,),
        in_specs=[
            pl.BlockSpec((n, LANE), lambda j: (0, j)),
            per_group(c2), per_group(cb), per_group(dn), per_group(bp), per_group(upr), per_group(upi),
            whole(rep), whole(til),
            pl.BlockSpec((None, 2, ns2 // 2), lambda j: (j, 0, 0)),
            pl.BlockSpec((1, LANE), lambda j: (0, j)),
        ],
        out_specs=pl.BlockSpec((n, LANE), lambda j: (0, j)),
        out_shape=jax.ShapeDtypeStruct((n, sw), BF16),
        scratch_shapes=[
            pltpu.VMEM((gb, tw, tw), BF16),
            pltpu.VMEM((gb, 2 * S5_STATE, tw), BF16),
            pltpu.VMEM((gb, tw, 2 * S5_STATE), BF16),
            pltpu.VMEM((S5_CHUNK, nc, LANE), F32),
            pltpu.VMEM((S5_CHUNK, LANE, nc), BF16),
            pltpu.VMEM((ns2, nc), F32),
            pltpu.VMEM((nc, ns2), F32),
            pltpu.VMEM((nc, ns2), F32),
            pltpu.VMEM((ns2, nc), BF16),
            pltpu.VMEM((S5_CHUNK, LANE, nc), F32),
            pltpu.VMEM((n, LANE), F32),
        ],
        compiler_params=pltpu.CompilerParams(
            dimension_semantics=("parallel",), vmem_limit_bytes=VMEM_LIMIT),
        name="s5_core",
    )(u, c2, cb, dn, bp, upr, upi, rep, til, a16, dsk)


def _s5_weights(a_re, a_im, log_dt, b_re, b_im, c_re, c_im, d_skip):
    g, p = a_re.shape
    tc = S5_CHUNK
    dt = jnp.exp(log_dt)[:, None]
    decay = jnp.exp(a_re * dt)
    ab_re = decay * jnp.cos(a_im * dt)
    ab_im = decay * jnp.sin(a_im * dt)
    denom = a_re * a_re + a_im * a_im
    num_re = ab_re - 1.0
    num_im = ab_im
    k_re = (num_re * a_re + num_im * a_im) / denom
    k_im = (num_im * a_re - num_re * a_im) / denom
    bp_re = k_re[..., None] * b_re - k_im[..., None] * b_im
    bp_im = k_re[..., None] * b_im + k_im[..., None] * b_re

    def cmul(xr, xi, yr, yi):
        return xr * yr - xi * yi, xr * yi + xi * yr

    pw_re, pw_im = [jnp.ones_like(ab_re)], [jnp.zeros_like(ab_im)]
    for _ in range(tc):
        nr, ni = cmul(pw_re[-1], pw_im[-1], ab_re, ab_im)
        pw_re.append(nr)
        pw_im.append(ni)
    dn = jnp.concatenate([jnp.stack(pw_re[tc - 1::-1], axis=2),
                          jnp.stack(pw_im[tc - 1::-1], axis=2)], axis=1)
    bp = jnp.concatenate([bp_re, bp_im], axis=1)
    up_re = jnp.stack(pw_re[1:], axis=1)
    up_im = jnp.stack(pw_im[1:], axis=1)
    upr = jnp.concatenate([up_re, up_re], axis=-1)
    upi = jnp.concatenate([up_im, up_im], axis=-1)
    c2 = jnp.concatenate([c_re, -c_im], axis=-1)
    cb = jnp.concatenate([-c_im, -c_re], axis=-1)
    eye = jnp.eye(S5_GROUP_WIDTH, dtype=BF16)
    rep = jnp.repeat(eye, S5_GROUP_WIDTH, axis=1)
    til = jnp.tile(eye, (1, tc))
    gb = GROUPS_PER_BLOCK
    a16 = jnp.stack([pw_re[tc].reshape(g // gb, gb * p), pw_im[tc].reshape(g // gb, gb * p)], axis=1)
    return (c2, cb, dn, bp, upr, upi, rep, til, a16, d_skip.reshape(1, -1))


def _merge_kernel(x_ref, h_ref, yap_ref, yb_ref, wglu_ref, bglu_ref, wga_ref, wgb_ref,
                  bga_ref, bgb_ref, wa_ref, wb_ref, *rest, nj):
    wo_refs = rest[:nj]
    o_ref, ya_ref, m_ref = rest[nj:]
    j = pl.program_id(1)

    @pl.when(j == 0)
    def _():
        yp = yap_ref[...]
        z = _dot(yp, wglu_ref[...]) + bglu_ref[...]
        ya_ref[...] = (yp.astype(F32) * jax.nn.sigmoid(z)).astype(BF16)

    @pl.when(j < nj)
    def _():
        h = h_ref[...]
        ga = jax.nn.sigmoid(_dot(h, wga_ref[...]) + bga_ref[...])
        gb = jax.nn.sigmoid(_dot(h, wgb_ref[...]) + bgb_ref[...])
        pa = _dot(ya_ref[...], wa_ref[...])
        pb = _dot(yb_ref[...], wb_ref[...])
        m_ref[j] = (ga * pa + gb * pb).astype(BF16)

    for k in range(nj):
        @pl.when(j == nj + k)
        def _(k=k):
            m = jnp.concatenate([m_ref[q] for q in range(nj)], axis=1)
            o_ref[...] = x_ref[...] + _dot(m, wo_refs[k][...])


def _merge(x, h, yap, yb, wglu, bglu, wgate, bgate, wa, wb, wo, *, tm=1024, tn=512):
    n, d = x.shape
    sw = yap.shape[1]
    nj = d // tn
    nt = n // tm
    second = lambda j: jnp.maximum(j - nj, 0)

    def first(j, back):
        return jnp.where(j < nj, j, jnp.where(j < nj + back, nj - 1, 0))

    once = pl.Buffered(1)
    return pl.pallas_call(
        functools.partial(_merge_kernel, nj=nj),
        grid=(nt, 2 * nj),
        in_specs=[
            pl.BlockSpec((tm, tn), lambda i, j: (i, second(j))),
            pl.BlockSpec((tm, d), lambda i, j: (_next_tile_after(i, j, nj, nt), 0)),
            pl.BlockSpec((tm, sw), lambda i, j: (_next_tile_after(i, j, 1, nt), 0)),
            pl.BlockSpec((tm, sw), lambda i, j: (_next_tile_after(i, j, nj, nt), 0)),
            pl.BlockSpec((sw, sw), lambda i, j: (0, 0), pipeline_mode=once),
            pl.BlockSpec((1, sw), lambda i, j: (0, 0)),
            pl.BlockSpec((d, tn), lambda i, j: (0, first(j, 1))),
            pl.BlockSpec((d, tn), lambda i, j: (0, first(j, 2) + nj)),
            pl.BlockSpec((1, tn), lambda i, j: (0, first(j, 1))),
            pl.BlockSpec((1, tn), lambda i, j: (0, first(j, 2) + nj)),
            pl.BlockSpec((sw, tn), lambda i, j: (0, first(j, 3))),
            pl.BlockSpec((sw, tn), lambda i, j: (0, first(j, 3))),
        ] + [pl.BlockSpec((d, tn), lambda i, j, k=k: (0, k), pipeline_mode=once) for k in range(nj)],
        out_specs=pl.BlockSpec((tm, tn), lambda i, j: (i, second(j))),
        out_shape=jax.ShapeDtypeStruct((n, d), F32),
        scratch_shapes=[pltpu.VMEM((tm, sw), BF16), pltpu.VMEM((nj, tm, tn), BF16)],
        compiler_params=pltpu.CompilerParams(
            dimension_semantics=("parallel", "arbitrary"), vmem_limit_bytes=VMEM_LIMIT),
        name="merge",
    )(x, h, yap, yb, wglu, bglu, wgate, wgate, bgate, bgate, wa, wb, *([wo] * nj))


def kernel(x, ffn1_norm, ffn1_w_gate, ffn1_w_up, ffn1_w_down, mix_norm, w_in, s5_a_re, s5_a_im, s5_log_dt, s5_b_re, s5_b_im, s5_c_re, s5_c_im, s5_d, s5_w_glu, s5_b_glu, sgu_ln_g, sgu_ln_b, sgu_w_s, sgu_b_s, w_branch_a, w_branch_b, w_gate, b_gate, w_out, ffn2_norm, ffn2_w_gate, ffn2_w_up, ffn2_w_down, final_norm):
    bsz, seq, d = x.shape
    depth = ffn1_norm.shape[0]
    sw = s5_w_glu.shape[1]
    assert bsz == 1 and sw == w_branch_b.shape[1] and sw % LANE == 0
    assert seq % (S5_CHUNK * LANE) == 0 and sgu_w_s.shape[2] == SGU_CHUNK
    bf = lambda w: w.astype(BF16)
    row = lambda v: v.reshape(1, -1)
    xs = x.reshape(seq, d)
    fin = row(final_norm)
    for i in range(depth):
        xs = _ffn(xs, row(ffn1_norm[i]), ffn1_w_gate[i], ffn1_w_up[i], ffn1_w_down[i], fin, final=False)
        bsb = jnp.repeat(jnp.transpose(sgu_b_s[i]), sw // SGU_HEADS, axis=1)
        ua, h, yb = _mix_in(xs, row(mix_norm[i]), bf(w_in[i]), row(sgu_ln_g[i]), row(sgu_ln_b[i]),
                            sgu_w_s[i], bsb, sw=sw)
        yap = _s5_core(ua, *_s5_weights(s5_a_re[i], s5_a_im[i], s5_log_dt[i], s5_b_re[i], s5_b_im[i],
                                        s5_c_re[i], s5_c_im[i], s5_d[i]))
        xs = _merge(xs, h, yap, yb, bf(s5_w_glu[i]), row(s5_b_glu[i]), bf(w_gate[i]), row(b_gate[i]),
                    bf(w_branch_a[i]), bf(w_branch_b[i]), bf(w_out[i]))
        xs = _ffn(xs, row(ffn2_norm[i]), ffn2_w_gate[i], ffn2_w_up[i], ffn2_w_down[i], fin,
                  final=(i == depth - 1))
    return xs.reshape(bsz, seq, d)
```

```python
import functools

import jax
import jax.numpy as jnp
from jax import lax
from jax.experimental import pallas as pl
from jax.experimental.pallas import tpu as pltpu

F32 = jnp.float32
BF16 = jnp.bfloat16

NORM_EPS = 1e-6
S5_GROUP_WIDTH = 16
S5_STATE = 64
SGU_HEADS = 8
SGU_CHUNK = 128
SGU_CAUSAL_BLOCK = 64
S5_CHUNK = 16
LANE = 128
GROUPS_PER_BLOCK = LANE // S5_GROUP_WIDTH
VMEM_LIMIT = 60 * 1024 * 1024
ROW_CHUNK = 256


def _rms(x, g):
    ms = jnp.mean(x * x, axis=-1, keepdims=True)
    return (x * lax.rsqrt(ms + NORM_EPS)) * g


def _dot(a, b):
    return jnp.dot(a, b, preferred_element_type=F32)


def _ffn_kernel(x_ref, g_ref, wg_ref, wu_ref, wd_ref, post_ref, o_ref, *rest, final):
    h_ref = rest[0]
    j = pl.program_id(1)
    last = pl.num_programs(1) - 1
    n_chunks = x_ref.shape[0] // ROW_CHUNK

    @pl.when(j == 0)
    def _():
        g = g_ref[...]

        def rows(r, carry):
            rs = pl.ds(pl.multiple_of(r * ROW_CHUNK, ROW_CHUNK), ROW_CHUNK)
            x = x_ref[rs, :]
            h_ref[rs, :] = _rms(x, g).astype(BF16)
            o_ref[rs, :] = x
            return carry

        lax.fori_loop(0, n_chunks, rows, 0, unroll=2)

    def accumulate():
        h = h_ref[...]
        a = _dot(h, wg_ref[...].astype(BF16))
        b = _dot(h, wu_ref[...].astype(BF16))
        hid = (a * jax.nn.sigmoid(a)) * b
        o_ref[...] += _dot(hid.astype(BF16), (0.5 * wd_ref[...]).astype(BF16))

    pl.when(j < last)(accumulate)

    @pl.when(j == last)
    def _():
        accumulate()
        y = _rms(o_ref[...], post_ref[...])
        if final:
            o_ref[...] = y
        else:
            rest[0][...] = y.astype(BF16)


def _next_tile_after(i, j, first_step, n_tiles):
    return jnp.minimum(i + jnp.where(j >= first_step, 1, 0), n_tiles - 1)


def _ffn(x, g, wg, wu, wd, post, *, final, tm=1024, tf=256):
    n, d = x.shape
    dff = wg.shape[1]
    nt = n // tm
    row_tile = pl.BlockSpec((tm, d), lambda i, j: (i, 0))
    if final:
        out_specs, out_shape = row_tile, jax.ShapeDtypeStruct((n, d), F32)
    else:
        out_specs = [row_tile, row_tile]
        out_shape = [jax.ShapeDtypeStruct((n, d), F32), jax.ShapeDtypeStruct((n, d), BF16)]
    return pl.pallas_call(
        functools.partial(_ffn_kernel, final=final),
        grid=(nt, dff // tf),
        in_specs=[
            row_tile,
            pl.BlockSpec((1, d), lambda i, j: (0, 0)),
            pl.BlockSpec((d, tf), lambda i, j: (0, j)),
            pl.BlockSpec((d, tf), lambda i, j: (0, j)),
            pl.BlockSpec((tf, d), lambda i, j: (j, 0)),
            pl.BlockSpec((1, d), lambda i, j: (0, 0)),
        ],
        out_specs=out_specs,
        out_shape=out_shape,
        scratch_shapes=[pltpu.VMEM((tm, d), BF16)] if final else [],
        compiler_params=pltpu.CompilerParams(
            dimension_semantics=("parallel", "arbitrary"), vmem_limit_bytes=VMEM_LIMIT),
        name="ffn_final" if final else "ffn",
    )(x, g, wg, wu, wd, post)


def _mix_in_kernel(h_ref, w_ref, lng_ref, lnb_ref, ws_ref, bs_ref, ua_ref, yb_ref, vn_ref, *, sw):
    j = pl.program_id(1)

    @pl.when(j == 0)
    def _():
        ua_ref[...] = _dot(h_ref[...], w_ref[...])

    @pl.when(j == 1)
    def _():
        v = jax.nn.gelu(_dot(h_ref[...], w_ref[...]))
        mu = jnp.mean(v, axis=-1, keepdims=True)
        vc = v - mu
        var = jnp.mean(vc * vc, axis=-1, keepdims=True)
        vn_ref[...] = ((vc * lax.rsqrt(var + NORM_EPS)) * lng_ref[...] + lnb_ref[...]).astype(BF16)

    @pl.when(j == 2)
    def _():
        tm = vn_ref.shape[0]
        u = jax.nn.gelu(_dot(h_ref[...], w_ref[...]))
        r = lax.broadcasted_iota(jnp.int32, (SGU_CHUNK, SGU_CHUNK), 0) // SGU_CAUSAL_BLOCK
        c = lax.broadcasted_iota(jnp.int32, (SGU_CHUNK, SGU_CHUNK), 1) // SGU_CAUSAL_BLOCK
        keep = r >= c
        hd = sw // SGU_HEADS
        for h in range(SGU_HEADS):
            wsm = jnp.where(keep, ws_ref[h], 0.0).astype(BF16)
            cols = slice(h * hd, (h + 1) * hd)
            for q in range(tm // SGU_CHUNK):
                rows = slice(q * SGU_CHUNK, (q + 1) * SGU_CHUNK)
                mixed = _dot(wsm, vn_ref[rows, cols]) + bs_ref[:, cols]
                yb_ref[rows, cols] = (u[rows, cols] * mixed).astype(BF16)


def _mix_in(h, w_in, ln_g, ln_b, w_s, bsb, *, sw, tm=1024):
    n, d = h.shape
    assert w_in.shape[1] == 3 * sw
    nt = n // tm
    return pl.pallas_call(
        functools.partial(_mix_in_kernel, sw=sw),
        grid=(nt, 3),
        in_specs=[
            pl.BlockSpec((tm, d), lambda i, j: (i, 0)),
            pl.BlockSpec((d, sw), lambda i, j: (0, jnp.where(j == 0, 0, 3 - j))),
            pl.BlockSpec((1, sw), lambda i, j: (0, 0)),
            pl.BlockSpec((1, sw), lambda i, j: (0, 0)),
            pl.BlockSpec(w_s.shape, lambda i, j: (0, 0, 0)),
            pl.BlockSpec(bsb.shape, lambda i, j: (0, 0)),
        ],
        out_specs=[
            pl.BlockSpec((tm, sw), lambda i, j: (i, 0)),
            pl.BlockSpec((tm, sw), lambda i, j: (i, 0)),
        ],
        out_shape=[
            jax.ShapeDtypeStruct((n, sw), F32),
            jax.ShapeDtypeStruct((n, sw), BF16),
        ],
        scratch_shapes=[pltpu.VMEM((tm, sw), BF16)],
        compiler_params=pltpu.CompilerParams(
            dimension_semantics=("parallel", "arbitrary"), vmem_limit_bytes=VMEM_LIMIT),
        name="mix_in",
    )(h, w_in, ln_g, ln_b, w_s, bsb)


def _dot_split(a, b):
    a_hi = a.astype(BF16)
    b_hi = b.astype(BF16)
    a_lo = (a - a_hi.astype(F32)).astype(BF16)
    b_lo = (b - b_hi.astype(F32)).astype(BF16)
    return _dot(a_hi, b_hi) + (_dot(a_hi, b_lo) + _dot(a_lo, b_hi))


def _expand(x, sel):
    x1 = x.astype(BF16)
    r1 = x - x1.astype(F32)
    x2 = r1.astype(BF16)
    x3 = (r1 - x2.astype(F32)).astype(BF16)
    return _dot(x1, sel) + (_dot(x2, sel) + _dot(x3, sel))


def _s5_kernel(u_ref, c2_ref, cb_ref, dn_ref, bp_ref, upr_ref, upi_ref, rep_ref, til_ref, a_ref, d_ref, o_ref,
               wt_ref, wb_ref, wc_ref, up_ref, ut_ref, vt_ref, v_ref, xs_ref, xt_ref, yt_ref, yn_ref):
    nc = u_ref.shape[0] // S5_CHUNK
    gw = S5_GROUP_WIDTH
    tw = S5_CHUNK * gw
    ns = GROUPS_PER_BLOCK * S5_STATE

    lane = lax.broadcasted_iota(jnp.int32, (gw, tw), 1)
    for g in range(GROUPS_PER_BLOCK):
        pw = _expand(dn_ref[g], rep_ref[...])
        bt = _expand(bp_ref[g], til_ref[...])
        pr, pi = pw[:S5_STATE], pw[S5_STATE:]
        br, bi = bt[:S5_STATE], bt[S5_STATE:]
        wbf = jnp.concatenate([pr * br - pi * bi, pr * bi + pi * br], axis=0)
        wb_ref[g] = wbf.astype(BF16)
        hrow = _dot_split(c2_ref[g], wbf)
        ca, cb = c2_ref[g], cb_ref[g]
        for t in range(S5_CHUNK):
            shift = (gw * (t + 1)) % tw
            rolled = pltpu.roll(hrow, shift, axis=1) if shift else hrow
            wt_ref[g, t * gw:(t + 1) * gw, :] = jnp.where(lane < gw * (t + 1), rolled, 0.0).astype(BF16)
            wc_ref[g, t * gw:(t + 1) * gw, :] = (
                ca * upr_ref[g, t:t + 1, :] + cb * upi_ref[g, t:t + 1, :]).astype(BF16)

    for s in range(S5_CHUNK):
        piece = u_ref[pl.ds(s, nc, stride=S5_CHUNK), :]
        up_ref[s] = piece
        ut_ref[s] = piece.T.astype(BF16)

    def chunk_inputs(g):
        return jnp.concatenate(
            [ut_ref[s, g * gw:(g + 1) * gw, :] for s in range(S5_CHUNK)], axis=0)

    for g in range(GROUPS_PER_BLOCK):
        vg = _dot(wb_ref[g], chunk_inputs(g))
        vt_ref[g * S5_STATE:(g + 1) * S5_STATE, :] = vg[:S5_STATE]
        vt_ref[ns + g * S5_STATE:ns + (g + 1) * S5_STATE, :] = vg[S5_STATE:]
    v_ref[...] = vt_ref[...].T

    ar = a_ref[0:1, :]
    ai = a_ref[1:2, :]

    def step(c, carry):
        xr, xi = carry
        xs_ref[pl.ds(c, 1), 0:ns] = xr
        xs_ref[pl.ds(c, 1), ns:2 * ns] = xi
        vr = v_ref[pl.ds(c, 1), 0:ns]
        vi = v_ref[pl.ds(c, 1), ns:2 * ns]
        return ar * xr - ai * xi + vr, ar * xi + ai * xr + vi

    zero = jnp.zeros((1, ns), F32)
    lax.fori_loop(0, nc, step, (zero, zero), unroll=8)
    xt_ref[...] = xs_ref[...].T.astype(BF16)

    for g in range(GROUPS_PER_BLOCK):
        xg = jnp.concatenate(
            [xt_ref[g * S5_STATE:(g + 1) * S5_STATE, :],
             xt_ref[ns + g * S5_STATE:ns + (g + 1) * S5_STATE, :]], axis=0)
        yg = _dot(wt_ref[g], chunk_inputs(g)) + _dot(wc_ref[g], xg)
        for t in range(S5_CHUNK):
            yt_ref[t, g * gw:(g + 1) * gw, :] = yg[t * gw:(t + 1) * gw, :]

    for t in range(S5_CHUNK):
        y = yt_ref[t].T + d_ref[...] * up_ref[t]
        yn_ref[pl.ds(t, nc, stride=S5_CHUNK), :] = jax.nn.gelu(y)
    o_ref[...] = yn_ref[...].astype(BF16)


def _s5_core(u, c2, cb, dn, bp, upr, upi, rep, til, a16, dsk):
    n, sw = u.shape
    nb = sw // LANE
    nc = n // S5_CHUNK
    gb = GROUPS_PER_BLOCK
    ns2 = 2 * gb * S5_STATE
    tw = S5_CHUNK * S5_GROUP_WIDTH
    per_group = lambda a: pl.BlockSpec((gb,) + a.shape[1:], lambda j: (j, 0, 0))
    whole = lambda a: pl.BlockSpec(a.shape, lambda j: (0, 0))
    return pl.pallas_call(
        _s5_kernel,
        grid=(nb,),
        in_specs=[
            pl.BlockSpec((n, LANE), lambda j: (0, j)),
            per_group(c2), per_group(cb), per_group(dn), per_group(bp), per_group(upr), per_group(upi),
            whole(rep), whole(til),
            pl.BlockSpec((None, 2, ns2 // 2), lambda j: (j, 0, 0)),
            pl.BlockSpec((1, LANE), lambda j: (0, j)),
        ],
        out_specs=pl.BlockSpec((n, LANE), lambda j: (0, j)),
        out_shape=jax.ShapeDtypeStruct((n, sw), BF16),
        scratch_shapes=[
            pltpu.VMEM((gb, tw, tw), BF16),
            pltpu.VMEM((gb, 2 * S5_STATE, tw), BF16),
            pltpu.VMEM((gb, tw, 2 * S5_STATE), BF16),
            pltpu.VMEM((S5_CHUNK, nc, LANE), F32),
            pltpu.VMEM((S5_CHUNK, LANE, nc), BF16),
            pltpu.VMEM((ns2, nc), F32),
            pltpu.VMEM((nc, ns2), F32),
            pltpu.VMEM((nc, ns2), F32),
            pltpu.VMEM((ns2, nc), BF16),
            pltpu.VMEM((S5_CHUNK, LANE, nc), F32),
            pltpu.VMEM((n, LANE), F32),
        ],
        compiler_params=pltpu.CompilerParams(
            dimension_semantics=("parallel",), vmem_limit_bytes=VMEM_LIMIT),
        name="s5_core",
    )(u, c2, cb, dn, bp, upr, upi, rep, til, a16, dsk)


def _s5_weights(a_re, a_im, log_dt, b_re, b_im, c_re, c_im, d_skip):
    g, p = a_re.shape
    tc = S5_CHUNK
    dt = jnp.exp(log_dt)[:, None]
    decay = jnp.exp(a_re * dt)
    ab_re = decay * jnp.cos(a_im * dt)
    ab_im = decay * jnp.sin(a_im * dt)
    denom = a_re * a_re + a_im * a_im
    num_re = ab_re - 1.0
    num_im = ab_im
    k_re = (num_re * a_re + num_im * a_im) / denom
    k_im = (num_im * a_re - num_re * a_im) / denom
    bp_re = k_re[..., None] * b_re - k_im[..., None] * b_im
    bp_im = k_re[..., None] * b_im + k_im[..., None] * b_re

    def cmul(xr, xi, yr, yi):
        return xr * yr - xi * yi, xr * yi + xi * yr

    pw_re, pw_im = [jnp.ones_like(ab_re)], [jnp.zeros_like(ab_im)]
    for _ in range(tc):
        nr, ni = cmul(pw_re[-1], pw_im[-1], ab_re, ab_im)
        pw_re.append(nr)
        pw_im.append(ni)
    dn = jnp.concatenate([jnp.stack(pw_re[tc - 1::-1], axis=2),
                          jnp.stack(pw_im[tc - 1::-1], axis=2)], axis=1)
    bp = jnp.concatenate([bp_re, bp_im], axis=1)
    up_re = jnp.stack(pw_re[1:], axis=1)
    up_im = jnp.stack(pw_im[1:], axis=1)
    upr = jnp.concatenate([up_re, up_re], axis=-1)
    upi = jnp.concatenate([up_im, up_im], axis=-1)
    c2 = jnp.concatenate([c_re, -c_im], axis=-1)
    cb = jnp.concatenate([-c_im, -c_re], axis=-1)
    eye = jnp.eye(S5_GROUP_WIDTH, dtype=BF16)
    rep = jnp.repeat(eye, S5_GROUP_WIDTH, axis=1)
    til = jnp.tile(eye, (1, tc))
    gb = GROUPS_PER_BLOCK
    a16 = jnp.stack([pw_re[tc].reshape(g // gb, gb * p), pw_im[tc].reshape(g // gb, gb * p)], axis=1)
    return (c2, cb, dn, bp, upr, upi, rep, til, a16, d_skip.reshape(1, -1))


def _merge_kernel(x_ref, h_ref, yap_ref, yb_ref, wglu_ref, bglu_ref, wga_ref, wgb_ref,
                  bga_ref, bgb_ref, wa_ref, wb_ref, *rest, n1, n2):
    wo_refs = rest[:n2]
    o_ref, ya_ref, m_ref = rest[n2:]
    j = pl.program_id(1)

    @pl.when(j == 0)
    def _():
        yp = yap_ref[...]
        z = _dot(yp, wglu_ref[...]) + bglu_ref[...]
        ya_ref[...] = (yp.astype(F32) * jax.nn.sigmoid(z)).astype(BF16)

    @pl.when(j < n1)
    def _():
        h = h_ref[...]
        ga = jax.nn.sigmoid(_dot(h, wga_ref[...].astype(BF16)) + bga_ref[...])
        gb = jax.nn.sigmoid(_dot(h, wgb_ref[...].astype(BF16)) + bgb_ref[...])
        pa = _dot(ya_ref[...], wa_ref[...].astype(BF16))
        pb = _dot(yb_ref[...], wb_ref[...].astype(BF16))
        m_ref[j] = (ga * pa + gb * pb).astype(BF16)

    for k in range(n2):
        @pl.when(j == n1 + k)
        def _(k=k):
            m = jnp.concatenate([m_ref[q] for q in range(n1)], axis=1)
            o_ref[...] = x_ref[...] + _dot(m, wo_refs[k][...])


def _merge(x, h, yap, yb, wglu, bglu, wgate, bgate, wa, wb, wo, *, tm=1024, t1=256, t2=512):
    n, d = x.shape
    sw = yap.shape[1]
    n1, n2 = d // t1, d // t2
    nt = n // tm
    second = lambda j: jnp.maximum(j - n1, 0)

    def first(j, back):
        return jnp.where(j < n1, j, jnp.where(j < n1 + back, n1 - 1, 0))

    once = pl.Buffered(1)
    return pl.pallas_call(
        functools.partial(_merge_kernel, n1=n1, n2=n2),
        grid=(nt, n1 + n2),
        in_specs=[
            pl.BlockSpec((tm, t2), lambda i, j: (i, second(j))),
            pl.BlockSpec((tm, d), lambda i, j: (_next_tile_after(i, j, n1, nt), 0)),
            pl.BlockSpec((tm, sw), lambda i, j: (_next_tile_after(i, j, 1, nt), 0)),
            pl.BlockSpec((tm, sw), lambda i, j: (_next_tile_after(i, j, n1, nt), 0)),
            pl.BlockSpec((sw, sw), lambda i, j: (0, 0), pipeline_mode=once),
            pl.BlockSpec((1, sw), lambda i, j: (0, 0)),
            pl.BlockSpec((d, t1), lambda i, j: (0, first(j, 1))),
            pl.BlockSpec((d, t1), lambda i, j: (0, first(j, 2) + n1)),
            pl.BlockSpec((1, t1), lambda i, j: (0, first(j, 1))),
            pl.BlockSpec((1, t1), lambda i, j: (0, first(j, 2) + n1)),
            pl.BlockSpec((sw, t1), lambda i, j: (0, first(j, 3))),
            pl.BlockSpec((sw, t1), lambda i, j: (0, first(j, 3))),
        ] + [pl.BlockSpec((d, t2), lambda i, j, k=k: (0, k), pipeline_mode=once) for k in range(n2)],
        out_specs=pl.BlockSpec((tm, t2), lambda i, j: (i, second(j))),
        out_shape=jax.ShapeDtypeStruct((n, d), F32),
        scratch_shapes=[pltpu.VMEM((tm, sw), BF16), pltpu.VMEM((n1, tm, t1), BF16)],
        compiler_params=pltpu.CompilerParams(
            dimension_semantics=("parallel", "arbitrary"), vmem_limit_bytes=VMEM_LIMIT),
        name="merge",
    )(x, h, yap, yb, wglu, bglu, wgate, wgate, bgate, bgate, wa, wb, *([wo] * n2))


def kernel(x, ffn1_norm, ffn1_w_gate, ffn1_w_up, ffn1_w_down, mix_norm, w_in, s5_a_re, s5_a_im, s5_log_dt, s5_b_re, s5_b_im, s5_c_re, s5_c_im, s5_d, s5_w_glu, s5_b_glu, sgu_ln_g, sgu_ln_b, sgu_w_s, sgu_b_s, w_branch_a, w_branch_b, w_gate, b_gate, w_out, ffn2_norm, ffn2_w_gate, ffn2_w_up, ffn2_w_down, final_norm):
    bsz, seq, d = x.shape
    depth = ffn1_norm.shape[0]
    sw = s5_w_glu.shape[1]
    assert bsz == 1 and sw == w_branch_b.shape[1] and sw % LANE == 0
    assert seq % (S5_CHUNK * LANE) == 0 and sgu_w_s.shape[2] == SGU_CHUNK
    bf = lambda w: w.astype(BF16)
    row = lambda v: v.reshape(1, -1)
    xs = x.reshape(seq, d)
    fin = row(final_norm)
    for i in range(depth):
        xs, h = _ffn(xs, row(ffn1_norm[i]), ffn1_w_gate[i], ffn1_w_up[i], ffn1_w_down[i],
                     row(mix_norm[i]), final=False)
        bsb = jnp.repeat(jnp.transpose(sgu_b_s[i]), sw // SGU_HEADS, axis=1)
        ua, yb = _mix_in(h, bf(w_in[i]), row(sgu_ln_g[i]), row(sgu_ln_b[i]), sgu_w_s[i], bsb, sw=sw)
        yap = _s5_core(ua, *_s5_weights(s5_a_re[i], s5_a_im[i], s5_log_dt[i], s5_b_re[i], s5_b_im[i],
                                        s5_c_re[i], s5_c_im[i], s5_d[i]))
        xs = _merge(xs, h, yap, yb, bf(s5_w_glu[i]), row(s5_b_glu[i]), w_gate[i], row(b_gate[i]),
                    w_branch_a[i], w_branch_b[i], bf(w_out[i]))
        if i == depth - 1:
            xs = _ffn(xs, row(ffn2_norm[i]), ffn2_w_gate[i], ffn2_w_up[i], ffn2_w_down[i], fin, final=True)
        else:
            xs, _ = _ffn(xs, row(ffn2_norm[i]), ffn2_w_gate[i], ffn2_w_up[i], ffn2_w_down[i], fin,
                         final=False)
    return xs.reshape(bsz, seq, d)
```

```python
import functools

import jax
import jax.numpy as jnp
from jax import lax
from jax.experimental import pallas as pl
from jax.experimental.pallas import tpu as pltpu

F32 = jnp.float32
BF16 = jnp.bfloat16

NORM_EPS = 1e-6
S5_GROUP_WIDTH = 16
S5_STATE = 64
SGU_HEADS = 8
SGU_CHUNK = 128
SGU_CAUSAL_BLOCK = 64
S5_CHUNK = 16
LANE = 128
GROUPS_PER_BLOCK = LANE // S5_GROUP_WIDTH
VMEM_LIMIT = 60 * 1024 * 1024
ROW_CHUNK = 256


def _rms(x, g):
    ms = jnp.mean(x * x, axis=-1, keepdims=True)
    return (x * lax.rsqrt(ms + NORM_EPS)) * g


def _dot(a, b):
    return jnp.dot(a, b, preferred_element_type=F32)


def _ffn_kernel(x_ref, g_ref, wg_ref, wu_ref, wd_ref, post_ref, o_ref, *rest, final):
    h_ref = rest[0]
    j = pl.program_id(1)
    last = pl.num_programs(1) - 1
    n_chunks = x_ref.shape[0] // ROW_CHUNK

    @pl.when(j == 0)
    def _():
        g = g_ref[...]

        def rows(r, carry):
            rs = pl.ds(pl.multiple_of(r * ROW_CHUNK, ROW_CHUNK), ROW_CHUNK)
            x = x_ref[rs, :]
            h_ref[rs, :] = _rms(x, g).astype(BF16)
            o_ref[rs, :] = x
            return carry

        lax.fori_loop(0, n_chunks, rows, 0, unroll=2)

    def accumulate():
        h = h_ref[...]
        a = _dot(h, wg_ref[...].astype(BF16))
        b = _dot(h, wu_ref[...].astype(BF16))
        hid = (a * jax.nn.sigmoid(a)) * b
        o_ref[...] += _dot(hid.astype(BF16), (0.5 * wd_ref[...]).astype(BF16))

    pl.when(j < last)(accumulate)

    @pl.when(j == last)
    def _():
        accumulate()
        y = _rms(o_ref[...], post_ref[...])
        if final:
            o_ref[...] = y
        else:
            rest[0][...] = y.astype(BF16)


def _next_tile_after(i, j, first_step, n_tiles):
    return jnp.minimum(i + jnp.where(j >= first_step, 1, 0), n_tiles - 1)


def _ffn(x, g, wg, wu, wd, post, *, final, tm=1024, tf=256):
    n, d = x.shape
    dff = wg.shape[1]
    nt = n // tm
    row_tile = pl.BlockSpec((tm, d), lambda i, j: (i, 0))
    if final:
        out_specs, out_shape = row_tile, jax.ShapeDtypeStruct((n, d), F32)
    else:
        out_specs = [row_tile, row_tile]
        out_shape = [jax.ShapeDtypeStruct((n, d), F32), jax.ShapeDtypeStruct((n, d), BF16)]
    return pl.pallas_call(
        functools.partial(_ffn_kernel, final=final),
        grid=(nt, dff // tf),
        in_specs=[
            row_tile,
            pl.BlockSpec((1, d), lambda i, j: (0, 0)),
            pl.BlockSpec((d, tf), lambda i, j: (0, j)),
            pl.BlockSpec((d, tf), lambda i, j: (0, j)),
            pl.BlockSpec((tf, d), lambda i, j: (j, 0)),
            pl.BlockSpec((1, d), lambda i, j: (0, 0)),
        ],
        out_specs=out_specs,
        out_shape=out_shape,
        scratch_shapes=[pltpu.VMEM((tm, d), BF16)] if final else [],
        compiler_params=pltpu.CompilerParams(
            dimension_semantics=("parallel", "arbitrary"), vmem_limit_bytes=VMEM_LIMIT),
        name="ffn_final" if final else "ffn",
    )(x, g, wg, wu, wd, post)


def _mix_in_kernel(h_ref, w_ref, lng_ref, lnb_ref, ws_ref, bs_ref, ua_ref, yb_ref, vn_ref, *, sw):
    j = pl.program_id(1)

    @pl.when(j == 0)
    def _():
        ua_ref[...] = _dot(h_ref[...], w_ref[...])

    @pl.when(j == 1)
    def _():
        v = jax.nn.gelu(_dot(h_ref[...], w_ref[...]))
        mu = jnp.mean(v, axis=-1, keepdims=True)
        vc = v - mu
        var = jnp.mean(vc * vc, axis=-1, keepdims=True)
        vn_ref[...] = ((vc * lax.rsqrt(var + NORM_EPS)) * lng_ref[...] + lnb_ref[...]).astype(BF16)

    @pl.when(j == 2)
    def _():
        tm = vn_ref.shape[0]
        u = jax.nn.gelu(_dot(h_ref[...], w_ref[...]))
        r = lax.broadcasted_iota(jnp.int32, (SGU_CHUNK, SGU_CHUNK), 0) // SGU_CAUSAL_BLOCK
        c = lax.broadcasted_iota(jnp.int32, (SGU_CHUNK, SGU_CHUNK), 1) // SGU_CAUSAL_BLOCK
        keep = r >= c
        hd = sw // SGU_HEADS
        for h in range(SGU_HEADS):
            wsm = jnp.where(keep, ws_ref[h], 0.0).astype(BF16)
            cols = slice(h * hd, (h + 1) * hd)
            for q in range(tm // SGU_CHUNK):
                rows = slice(q * SGU_CHUNK, (q + 1) * SGU_CHUNK)
                mixed = _dot(wsm, vn_ref[rows, cols]) + bs_ref[:, cols]
                yb_ref[rows, cols] = (u[rows, cols] * mixed).astype(BF16)


def _mix_in(h, w_in, ln_g, ln_b, w_s, bsb, *, sw, tm=1024):
    n, d = h.shape
    assert w_in.shape[1] == 3 * sw
    nt = n // tm
    return pl.pallas_call(
        functools.partial(_mix_in_kernel, sw=sw),
        grid=(nt, 3),
        in_specs=[
            pl.BlockSpec((tm, d), lambda i, j: (i, 0)),
            pl.BlockSpec((d, sw), lambda i, j: (0, jnp.where(j == 0, 0, 3 - j))),
            pl.BlockSpec((1, sw), lambda i, j: (0, 0)),
            pl.BlockSpec((1, sw), lambda i, j: (0, 0)),
            pl.BlockSpec(w_s.shape, lambda i, j: (0, 0, 0)),
            pl.BlockSpec(bsb.shape, lambda i, j: (0, 0)),
        ],
        out_specs=[
            pl.BlockSpec((tm, sw), lambda i, j: (i, 0)),
            pl.BlockSpec((tm, sw), lambda i, j: (i, 0)),
        ],
        out_shape=[
            jax.ShapeDtypeStruct((n, sw), F32),
            jax.ShapeDtypeStruct((n, sw), BF16),
        ],
        scratch_shapes=[pltpu.VMEM((tm, sw), BF16)],
        compiler_params=pltpu.CompilerParams(
            dimension_semantics=("parallel", "arbitrary"), vmem_limit_bytes=VMEM_LIMIT),
        name="mix_in",
    )(h, w_in, ln_g, ln_b, w_s, bsb)


def _dot_split(a, b):
    a_hi = a.astype(BF16)
    b_hi = b.astype(BF16)
    a_lo = (a - a_hi.astype(F32)).astype(BF16)
    b_lo = (b - b_hi.astype(F32)).astype(BF16)
    return _dot(a_hi, b_hi) + (_dot(a_hi, b_lo) + _dot(a_lo, b_hi))


def _expand(x, sel):
    x1 = x.astype(BF16)
    r1 = x - x1.astype(F32)
    x2 = r1.astype(BF16)
    x3 = (r1 - x2.astype(F32)).astype(BF16)
    return _dot(x1, sel) + (_dot(x2, sel) + _dot(x3, sel))


def _expand_t(x, sel):
    tn = (((0,), (0,)), ((), ()))
    x1 = x.astype(BF16)
    r1 = x - x1.astype(F32)
    x2 = r1.astype(BF16)
    x3 = (r1 - x2.astype(F32)).astype(BF16)
    dot_t = lambda v: lax.dot_general(v, sel, tn, preferred_element_type=F32)
    return dot_t(x1) + (dot_t(x2) + dot_t(x3))


def _s5_kernel(u_ref, tab_ref, c2_ref, cb_ref, bre_ref, bim_ref, rep_ref, til_ref, a_ref, d_ref, o_ref,
               dn_ref, upr_ref, upi_ref, wt_ref, wb_ref, wc_ref,
               up_ref, ut_ref, vt_ref, v_ref, xs_ref, xt_ref, yt_ref, yn_ref):
    nc = u_ref.shape[0] // S5_CHUNK
    gw = S5_GROUP_WIDTH
    gb = GROUPS_PER_BLOCK
    tw = S5_CHUNK * gw
    ns = gb * S5_STATE
    sl = 2 * S5_STATE

    tab = tab_ref[...]
    ar2, an, ap = tab[:, 0:sl], tab[:, sl:2 * sl], tab[:, 2 * sl:3 * sl]
    q, qs = tab[:, 3 * sl:4 * sl], tab[:, 4 * sl:5 * sl]
    lo = lax.broadcasted_iota(jnp.int32, (gb, sl), 1) < S5_STATE
    x = jnp.where(lo, 1.0, 0.0)
    xs = jnp.where(lo, 0.0, 1.0)
    dn_ref[pl.ds((S5_CHUNK - 1) * gb, gb), :] = q
    for j in range(1, S5_CHUNK + 1):
        x, xs = x * ar2 + xs * an, xs * ar2 + x * ap
        upr_ref[pl.ds((j - 1) * gb, gb), :] = jnp.where(lo, x, xs)
        upi_ref[pl.ds((j - 1) * gb, gb), :] = jnp.where(lo, xs, x)
        if j < S5_CHUNK:
            q, qs = q * ar2 + qs * an, qs * ar2 + q * ap
            dn_ref[pl.ds((S5_CHUNK - 1 - j) * gb, gb), :] = q

    lane = lax.broadcasted_iota(jnp.int32, (gw, tw), 1)
    for g in range(gb):
        pw = _expand_t(dn_ref[pl.ds(g, S5_CHUNK, stride=gb), :], rep_ref[...])
        br = _expand(bre_ref[g], til_ref[...])
        bi = _expand(bim_ref[g], til_ref[...])
        pr, pi = pw[:S5_STATE], pw[S5_STATE:]
        wbf = jnp.concatenate([pr * br - pi * bi, pr * bi + pi * br], axis=0)
        wb_ref[g] = wbf.astype(BF16)
        hrow = _dot_split(c2_ref[g], wbf)
        ca, cb = c2_ref[g], cb_ref[g]
        for t in range(S5_CHUNK):
            shift = (gw * (t + 1)) % tw
            rolled = pltpu.roll(hrow, shift, axis=1) if shift else hrow
            wt_ref[g, t * gw:(t + 1) * gw, :] = jnp.where(lane < gw * (t + 1), rolled, 0.0).astype(BF16)
            row = t * gb + g
            wc_ref[g, t * gw:(t + 1) * gw, :] = (
                ca * upr_ref[row:row + 1, :] + cb * upi_ref[row:row + 1, :]).astype(BF16)

    for s in range(S5_CHUNK):
        piece = u_ref[pl.ds(s, nc, stride=S5_CHUNK), :]
        up_ref[s] = piece
        ut_ref[s] = piece.T.astype(BF16)

    def chunk_inputs(g):
        return jnp.concatenate(
            [ut_ref[s, g * gw:(g + 1) * gw, :] for s in range(S5_CHUNK)], axis=0)

    for g in range(GROUPS_PER_BLOCK):
        vg = _dot(wb_ref[g], chunk_inputs(g))
        vt_ref[g * S5_STATE:(g + 1) * S5_STATE, :] = vg[:S5_STATE]
        vt_ref[ns + g * S5_STATE:ns + (g + 1) * S5_STATE, :] = vg[S5_STATE:]
    v_ref[...] = vt_ref[...].T

    ar = a_ref[0:1, :]
    ai = a_ref[1:2, :]

    def step(c, carry):
        xr, xi = carry
        xs_ref[pl.ds(c, 1), 0:ns] = xr
        xs_ref[pl.ds(c, 1), ns:2 * ns] = xi
        vr = v_ref[pl.ds(c, 1), 0:ns]
        vi = v_ref[pl.ds(c, 1), ns:2 * ns]
        return ar * xr - ai * xi + vr, ar * xi + ai * xr + vi

    zero = jnp.zeros((1, ns), F32)
    lax.fori_loop(0, nc, step, (zero, zero), unroll=8)
    xt_ref[...] = xs_ref[...].T.astype(BF16)

    for g in range(GROUPS_PER_BLOCK):
        xg = jnp.concatenate(
            [xt_ref[g * S5_STATE:(g + 1) * S5_STATE, :],
             xt_ref[ns + g * S5_STATE:ns + (g + 1) * S5_STATE, :]], axis=0)
        yg = _dot(wt_ref[g], chunk_inputs(g)) + _dot(wc_ref[g], xg)
        for t in range(S5_CHUNK):
            yt_ref[t, g * gw:(g + 1) * gw, :] = yg[t * gw:(t + 1) * gw, :]

    for t in range(S5_CHUNK):
        y = yt_ref[t].T + d_ref[...] * up_ref[t]
        yn_ref[pl.ds(t, nc, stride=S5_CHUNK), :] = jax.nn.gelu(y)
    o_ref[...] = yn_ref[...].astype(BF16)


def _s5_core(u, tab, c2, cb, b_re, b_im, rep, til, a16, dsk):
    n, sw = u.shape
    nb = sw // LANE
    nc = n // S5_CHUNK
    gb = GROUPS_PER_BLOCK
    ns2 = 2 * gb * S5_STATE
    tw = S5_CHUNK * S5_GROUP_WIDTH
    per_group = lambda a: pl.BlockSpec((gb,) + a.shape[1:], lambda j: (j,) + (0,) * (a.ndim - 1))
    whole = lambda a: pl.BlockSpec(a.shape, lambda j: (0, 0))
    table_rows = pltpu.VMEM((S5_CHUNK * gb, 2 * S5_STATE), F32)
    return pl.pallas_call(
        _s5_kernel,
        grid=(nb,),
        in_specs=[
            pl.BlockSpec((n, LANE), lambda j: (0, j)),
            per_group(tab), per_group(c2), per_group(cb), per_group(b_re), per_group(b_im),
            whole(rep), whole(til),
            pl.BlockSpec((None, 2, ns2 // 2), lambda j: (j, 0, 0)),
            pl.BlockSpec((1, LANE), lambda j: (0, j)),
        ],
        out_specs=pl.BlockSpec((n, LANE), lambda j: (0, j)),
        out_shape=jax.ShapeDtypeStruct((n, sw), BF16),
        scratch_shapes=[
            table_rows, table_rows, table_rows,
            pltpu.VMEM((gb, tw, tw), BF16),
            pltpu.VMEM((gb, 2 * S5_STATE, tw), BF16),
            pltpu.VMEM((gb, tw, 2 * S5_STATE), BF16),
            pltpu.VMEM((S5_CHUNK, nc, LANE), F32),
            pltpu.VMEM((S5_CHUNK, LANE, nc), BF16),
            pltpu.VMEM((ns2, nc), F32),
            pltpu.VMEM((nc, ns2), F32),
            pltpu.VMEM((nc, ns2), F32),
            pltpu.VMEM((ns2, nc), BF16),
            pltpu.VMEM((S5_CHUNK, LANE, nc), F32),
            pltpu.VMEM((n, LANE), F32),
        ],
        compiler_params=pltpu.CompilerParams(
            dimension_semantics=("parallel",), vmem_limit_bytes=VMEM_LIMIT),
        name="s5_core",
    )(u, tab, c2, cb, b_re, b_im, rep, til, a16, dsk)


def _s5_weights(a_re, a_im, log_dt, b_re, b_im, c_re, c_im, d_skip):
    g, p = a_re.shape
    dt = jnp.exp(log_dt)[:, None]
    decay = jnp.exp(a_re * dt)
    ab_re = decay * jnp.cos(a_im * dt)
    ab_im = decay * jnp.sin(a_im * dt)
    denom = a_re * a_re + a_im * a_im
    num_re = ab_re - 1.0
    num_im = ab_im
    k_re = (num_re * a_re + num_im * a_im) / denom
    k_im = (num_im * a_re - num_re * a_im) / denom
    tab = jnp.concatenate([ab_re, ab_re, -ab_im, ab_im, ab_im, -ab_im, k_re, k_im, k_im, k_re], axis=1)
    r, i = ab_re, ab_im
    for _ in range(S5_CHUNK.bit_length() - 1):
        r, i = r * r - i * i, 2.0 * (r * i)
    gb = GROUPS_PER_BLOCK
    a16 = jnp.stack([r.reshape(g // gb, gb * p), i.reshape(g // gb, gb * p)], axis=1)
    c2 = jnp.concatenate([c_re, -c_im], axis=-1)
    cb = jnp.concatenate([-c_im, -c_re], axis=-1)
    eye = jnp.eye(S5_GROUP_WIDTH, dtype=BF16)
    rep = jnp.repeat(eye, S5_GROUP_WIDTH, axis=1)
    til = jnp.tile(eye, (1, S5_CHUNK))
    return (tab, c2, cb, b_re, b_im, rep, til, a16, d_skip.reshape(1, -1))


def _merge_kernel(x_ref, h_ref, yap_ref, yb_ref, wglu_ref, bglu_ref, wga_ref, wgb_ref,
                  bga_ref, bgb_ref, wa_ref, wb_ref, *rest, n1, n2):
    wo_refs = rest[:n2]
    o_ref, ya_ref, m_ref = rest[n2:]
    j = pl.program_id(1)

    @pl.when(j == 0)
    def _():
        yp = yap_ref[...]
        z = _dot(yp, wglu_ref[...]) + bglu_ref[...]
        ya_ref[...] = (yp.astype(F32) * jax.nn.sigmoid(z)).astype(BF16)

    @pl.when(j < n1)
    def _():
        h = h_ref[...]
        ga = jax.nn.sigmoid(_dot(h, wga_ref[...].astype(BF16)) + bga_ref[...])
        gb = jax.nn.sigmoid(_dot(h, wgb_ref[...].astype(BF16)) + bgb_ref[...])
        pa = _dot(ya_ref[...], wa_ref[...].astype(BF16))
        pb = _dot(yb_ref[...], wb_ref[...].astype(BF16))
        m_ref[j] = (ga * pa + gb * pb).astype(BF16)

    for k in range(n2):
        @pl.when(j == n1 + k)
        def _(k=k):
            m = jnp.concatenate([m_ref[q] for q in range(n1)], axis=1)
            o_ref[...] = x_ref[...] + _dot(m, wo_refs[k][...])


def _merge(x, h, yap, yb, wglu, bglu, wgate, bgate, wa, wb, wo, *, tm=1024, t1=256, t2=512):
    n, d = x.shape
    sw = yap.shape[1]
    n1, n2 = d // t1, d // t2
    nt = n // tm
    second = lambda j: jnp.maximum(j - n1, 0)

    def first(j, back):
        return jnp.where(j < n1, j, jnp.where(j < n1 + back, n1 - 1, 0))

    once = pl.Buffered(1)
    return pl.pallas_call(
        functools.partial(_merge_kernel, n1=n1, n2=n2),
        grid=(nt, n1 + n2),
        in_specs=[
            pl.BlockSpec((tm, t2), lambda i, j: (i, second(j))),
            pl.BlockSpec((tm, d), lambda i, j: (_next_tile_after(i, j, n1, nt), 0)),
            pl.BlockSpec((tm, sw), lambda i, j: (_next_tile_after(i, j, 1, nt), 0)),
            pl.BlockSpec((tm, sw), lambda i, j: (_next_tile_after(i, j, n1, nt), 0)),
            pl.BlockSpec((sw, sw), lambda i, j: (0, 0), pipeline_mode=once),
            pl.BlockSpec((1, sw), lambda i, j: (0, 0)),
            pl.BlockSpec((d, t1), lambda i, j: (0, first(j, 1))),
            pl.BlockSpec((d, t1), lambda i, j: (0, first(j, 2) + n1)),
            pl.BlockSpec((1, t1), lambda i, j: (0, first(j, 1))),
            pl.BlockSpec((1, t1), lambda i, j: (0, first(j, 2) + n1)),
            pl.BlockSpec((sw, t1), lambda i, j: (0, first(j, 3))),
            pl.BlockSpec((sw, t1), lambda i, j: (0, first(j, 3))),
        ] + [pl.BlockSpec((d, t2), lambda i, j, k=k: (0, k), pipeline_mode=once) for k in range(n2)],
        out_specs=pl.BlockSpec((tm, t2), lambda i, j: (i, second(j))),
        out_shape=jax.ShapeDtypeStruct((n, d), F32),
        scratch_shapes=[pltpu.VMEM((tm, sw), BF16), pltpu.VMEM((n1, tm, t1), BF16)],
        compiler_params=pltpu.CompilerParams(
            dimension_semantics=("parallel", "arbitrary"), vmem_limit_bytes=VMEM_LIMIT),
        name="merge",
    )(x, h, yap, yb, wglu, bglu, wgate, wgate, bgate, bgate, wa, wb, *([wo] * n2))


def kernel(x, ffn1_norm, ffn1_w_gate, ffn1_w_up, ffn1_w_down, mix_norm, w_in, s5_a_re, s5_a_im, s5_log_dt, s5_b_re, s5_b_im, s5_c_re, s5_c_im, s5_d, s5_w_glu, s5_b_glu, sgu_ln_g, sgu_ln_b, sgu_w_s, sgu_b_s, w_branch_a, w_branch_b, w_gate, b_gate, w_out, ffn2_norm, ffn2_w_gate, ffn2_w_up, ffn2_w_down, final_norm):
    bsz, seq, d = x.shape
    depth = ffn1_norm.shape[0]
    sw = s5_w_glu.shape[1]
    assert bsz == 1 and sw == w_branch_b.shape[1] and sw % LANE == 0
    assert seq % (S5_CHUNK * LANE) == 0 and sgu_w_s.shape[2] == SGU_CHUNK
    bf = lambda w: w.astype(BF16)
    row = lambda v: v.reshape(1, -1)
    xs = x.reshape(seq, d)
    fin = row(final_norm)
    for i in range(depth):
        xs, h = _ffn(xs, row(ffn1_norm[i]), ffn1_w_gate[i], ffn1_w_up[i], ffn1_w_down[i],
                     row(mix_norm[i]), final=False)
        bsb = jnp.repeat(jnp.transpose(sgu_b_s[i]), sw // SGU_HEADS, axis=1)
        ua, yb = _mix_in(h, bf(w_in[i]), row(sgu_ln_g[i]), row(sgu_ln_b[i]), sgu_w_s[i], bsb, sw=sw)
        yap = _s5_core(ua, *_s5_weights(s5_a_re[i], s5_a_im[i], s5_log_dt[i], s5_b_re[i], s5_b_im[i],
                                        s5_c_re[i], s5_c_im[i], s5_d[i]))
        xs = _merge(xs, h, yap, yb, bf(s5_w_glu[i]), row(s5_b_glu[i]), w_gate[i], row(b_gate[i]),
                    w_branch_a[i], w_branch_b[i], bf(w_out[i]))
        if i == depth - 1:
            xs = _ffn(xs, row(ffn2_norm[i]), ffn2_w_gate[i], ffn2_w_up[i], ffn2_w_down[i], fin, final=True)
        else:
            xs, _ = _ffn(xs, row(ffn2_norm[i]), ffn2_w_gate[i], ffn2_w_up[i], ffn2_w_down[i], fin,
                         final=False)
    return xs.reshape(bsz, seq, d)
```

```python
import functools

import jax
import jax.numpy as jnp
from jax import lax
from jax.experimental import pallas as pl
from jax.experimental.pallas import tpu as pltpu

F32 = jnp.float32
BF16 = jnp.bfloat16

NORM_EPS = 1e-6
S5_GROUP_WIDTH = 16
S5_STATE = 64
SGU_HEADS = 8
SGU_CHUNK = 128
SGU_CAUSAL_BLOCK = 64
S5_CHUNK = 16
LANE = 128
GROUPS_PER_BLOCK = LANE // S5_GROUP_WIDTH
VMEM_LIMIT = 60 * 1024 * 1024
ROW_CHUNK = 256


def _rms(x, g):
    ms = jnp.mean(x * x, axis=-1, keepdims=True)
    return (x * lax.rsqrt(ms + NORM_EPS)) * g


def _dot(a, b):
    return jnp.dot(a, b, preferred_element_type=F32)


def _ffn_kernel(x_ref, g_ref, wg_ref, wu_ref, wd_ref, post_ref, o_ref, *rest, final):
    h_ref = rest[0]
    j = pl.program_id(1)
    last = pl.num_programs(1) - 1
    n_chunks = x_ref.shape[0] // ROW_CHUNK

    @pl.when(j == 0)
    def _():
        g = g_ref[...]

        def rows(r, carry):
            rs = pl.ds(pl.multiple_of(r * ROW_CHUNK, ROW_CHUNK), ROW_CHUNK)
            x = x_ref[rs, :]
            h_ref[rs, :] = _rms(x, g).astype(BF16)
            o_ref[rs, :] = x
            return carry

        lax.fori_loop(0, n_chunks, rows, 0, unroll=2)

    def accumulate():
        h = h_ref[...]
        a = _dot(h, wg_ref[...].astype(BF16))
        b = _dot(h, wu_ref[...].astype(BF16))
        hid = (a * jax.nn.sigmoid(a)) * b
        o_ref[...] += _dot(hid.astype(BF16), (0.5 * wd_ref[...]).astype(BF16))

    pl.when(j < last)(accumulate)

    @pl.when(j == last)
    def _():
        accumulate()
        y = _rms(o_ref[...], post_ref[...])
        if final:
            o_ref[...] = y
        else:
            rest[0][...] = y.astype(BF16)


def _next_tile_after(i, j, first_step, n_tiles):
    return jnp.minimum(i + jnp.where(j >= first_step, 1, 0), n_tiles - 1)


def _ffn(x, g, wg, wu, wd, post, *, final, tm=1024):
    n, d = x.shape
    dff = wg.shape[1]
    nt = n // tm
    row_tile = pl.BlockSpec((tm, d), lambda i, j: (i, 0))
    tf = 512 if final else 256
    x_tile = pl.BlockSpec((tm, d), lambda i, j: (i, 0), pipeline_mode=pl.Buffered(1)) if final else row_tile
    if final:
        out_specs, out_shape = row_tile, jax.ShapeDtypeStruct((n, d), F32)
    else:
        out_specs = [row_tile, row_tile]
        out_shape = [jax.ShapeDtypeStruct((n, d), F32), jax.ShapeDtypeStruct((n, d), BF16)]
    return pl.pallas_call(
        functools.partial(_ffn_kernel, final=final),
        grid=(nt, dff // tf),
        in_specs=[
            x_tile,
            pl.BlockSpec((1, d), lambda i, j: (0, 0)),
            pl.BlockSpec((d, tf), lambda i, j: (0, j)),
            pl.BlockSpec((d, tf), lambda i, j: (0, j)),
            pl.BlockSpec((tf, d), lambda i, j: (j, 0)),
            pl.BlockSpec((1, d), lambda i, j: (0, 0)),
        ],
        out_specs=out_specs,
        out_shape=out_shape,
        scratch_shapes=[pltpu.VMEM((tm, d), BF16)] if final else [],
        compiler_params=pltpu.CompilerParams(
            dimension_semantics=("parallel", "arbitrary"), vmem_limit_bytes=VMEM_LIMIT),
        name="ffn_final" if final else "ffn",
    )(x, g, wg, wu, wd, post)


def _mix_in_kernel(h_ref, w_ref, lng_ref, lnb_ref, ws_ref, bs_ref, ua_ref, yb_ref, vn_ref, *, sw):
    j = pl.program_id(1)

    @pl.when(j == 0)
    def _():
        ua_ref[...] = _dot(h_ref[...], w_ref[...].astype(BF16))

    @pl.when(j == 1)
    def _():
        v = jax.nn.gelu(_dot(h_ref[...], w_ref[...].astype(BF16)))
        mu = jnp.mean(v, axis=-1, keepdims=True)
        vc = v - mu
        var = jnp.mean(vc * vc, axis=-1, keepdims=True)
        vn_ref[...] = ((vc * lax.rsqrt(var + NORM_EPS)) * lng_ref[...] + lnb_ref[...]).astype(BF16)

    @pl.when(j == 2)
    def _():
        tm = vn_ref.shape[0]
        u = jax.nn.gelu(_dot(h_ref[...], w_ref[...].astype(BF16)))
        r = lax.broadcasted_iota(jnp.int32, (SGU_CHUNK, SGU_CHUNK), 0) // SGU_CAUSAL_BLOCK
        c = lax.broadcasted_iota(jnp.int32, (SGU_CHUNK, SGU_CHUNK), 1) // SGU_CAUSAL_BLOCK
        keep = r >= c
        hd = sw // SGU_HEADS
        for h in range(SGU_HEADS):
            wsm = jnp.where(keep, ws_ref[h], 0.0).astype(BF16)
            cols = slice(h * hd, (h + 1) * hd)
            for q in range(tm // SGU_CHUNK):
                rows = slice(q * SGU_CHUNK, (q + 1) * SGU_CHUNK)
                mixed = _dot(wsm, vn_ref[rows, cols]) + bs_ref[:, cols]
                yb_ref[rows, cols] = (u[rows, cols] * mixed).astype(BF16)


def _mix_in(h, w_in, ln_g, ln_b, w_s, bsb, *, sw, tm=1024):
    n, d = h.shape
    assert w_in.shape[1] == 3 * sw
    nt = n // tm
    return pl.pallas_call(
        functools.partial(_mix_in_kernel, sw=sw),
        grid=(nt, 3),
        in_specs=[
            pl.BlockSpec((tm, d), lambda i, j: (i, 0)),
            pl.BlockSpec((d, sw), lambda i, j: (0, jnp.where(j == 0, 0, 3 - j))),
            pl.BlockSpec((1, sw), lambda i, j: (0, 0)),
            pl.BlockSpec((1, sw), lambda i, j: (0, 0)),
            pl.BlockSpec(w_s.shape, lambda i, j: (0, 0, 0)),
            pl.BlockSpec(bsb.shape, lambda i, j: (0, 0)),
        ],
        out_specs=[
            pl.BlockSpec((tm, sw), lambda i, j: (i, 0)),
            pl.BlockSpec((tm, sw), lambda i, j: (i, 0)),
        ],
        out_shape=[
            jax.ShapeDtypeStruct((n, sw), F32),
            jax.ShapeDtypeStruct((n, sw), BF16),
        ],
        scratch_shapes=[pltpu.VMEM((tm, sw), BF16)],
        compiler_params=pltpu.CompilerParams(
            dimension_semantics=("parallel", "arbitrary"), vmem_limit_bytes=VMEM_LIMIT),
        name="mix_in",
    )(h, w_in, ln_g, ln_b, w_s, bsb)


def _dot_split(a, b):
    a_hi = a.astype(BF16)
    b_hi = b.astype(BF16)
    a_lo = (a - a_hi.astype(F32)).astype(BF16)
    b_lo = (b - b_hi.astype(F32)).astype(BF16)
    return _dot(a_hi, b_hi) + (_dot(a_hi, b_lo) + _dot(a_lo, b_hi))


def _expand(x, sel):
    x1 = x.astype(BF16)
    r1 = x - x1.astype(F32)
    x2 = r1.astype(BF16)
    x3 = (r1 - x2.astype(F32)).astype(BF16)
    return _dot(x1, sel) + (_dot(x2, sel) + _dot(x3, sel))


def _expand_t(x, sel):
    tn = (((0,), (0,)), ((), ()))
    x1 = x.astype(BF16)
    r1 = x - x1.astype(F32)
    x2 = r1.astype(BF16)
    x3 = (r1 - x2.astype(F32)).astype(BF16)
    dot_t = lambda v: lax.dot_general(v, sel, tn, preferred_element_type=F32)
    return dot_t(x1) + (dot_t(x2) + dot_t(x3))


def _s5_kernel(u_ref, tab_ref, c2_ref, cb_ref, bre_ref, bim_ref, rep_ref, til_ref, a_ref, d_ref, o_ref,
               dn_ref, upr_ref, upi_ref, wt_ref, wb_ref, wc_ref,
               up_ref, ut_ref, vt_ref, v_ref, xs_ref, xt_ref, yt_ref, yn_ref):
    nc = u_ref.shape[0] // S5_CHUNK
    gw = S5_GROUP_WIDTH
    gb = GROUPS_PER_BLOCK
    tw = S5_CHUNK * gw
    ns = gb * S5_STATE
    sl = 2 * S5_STATE

    tab = tab_ref[...]
    ar2, an, ap = tab[:, 0:sl], tab[:, sl:2 * sl], tab[:, 2 * sl:3 * sl]
    q, qs = tab[:, 3 * sl:4 * sl], tab[:, 4 * sl:5 * sl]
    lo = lax.broadcasted_iota(jnp.int32, (gb, sl), 1) < S5_STATE
    x = jnp.where(lo, 1.0, 0.0)
    xs = jnp.where(lo, 0.0, 1.0)
    dn_ref[pl.ds((S5_CHUNK - 1) * gb, gb), :] = q
    for j in range(1, S5_CHUNK + 1):
        x, xs = x * ar2 + xs * an, xs * ar2 + x * ap
        upr_ref[pl.ds((j - 1) * gb, gb), :] = jnp.where(lo, x, xs)
        upi_ref[pl.ds((j - 1) * gb, gb), :] = jnp.where(lo, xs, x)
        if j < S5_CHUNK:
            q, qs = q * ar2 + qs * an, qs * ar2 + q * ap
            dn_ref[pl.ds((S5_CHUNK - 1 - j) * gb, gb), :] = q

    lane = lax.broadcasted_iota(jnp.int32, (gw, tw), 1)
    for g in range(gb):
        pw = _expand_t(dn_ref[pl.ds(g, S5_CHUNK, stride=gb), :], rep_ref[...])
        br = _expand(bre_ref[g], til_ref[...])
        bi = _expand(bim_ref[g], til_ref[...])
        pr, pi = pw[:S5_STATE], pw[S5_STATE:]
        wbf = jnp.concatenate([pr * br - pi * bi, pr * bi + pi * br], axis=0)
        wb_ref[g] = wbf.astype(BF16)
        hrow = _dot_split(c2_ref[g], wbf)
        ca, cb = c2_ref[g], cb_ref[g]
        for t in range(S5_CHUNK):
            shift = (gw * (t + 1)) % tw
            rolled = pltpu.roll(hrow, shift, axis=1) if shift else hrow
            wt_ref[g, t * gw:(t + 1) * gw, :] = jnp.where(lane < gw * (t + 1), rolled, 0.0).astype(BF16)
            row = t * gb + g
            wc_ref[g, t * gw:(t + 1) * gw, :] = (
                ca * upr_ref[row:row + 1, :] + cb * upi_ref[row:row + 1, :]).astype(BF16)

    for s in range(S5_CHUNK):
        piece = u_ref[pl.ds(s, nc, stride=S5_CHUNK), :]
        up_ref[s] = piece
        ut_ref[s] = piece.astype(BF16).T

    def chunk_inputs(g):
        return jnp.concatenate(
            [ut_ref[s, g * gw:(g + 1) * gw, :] for s in range(S5_CHUNK)], axis=0)

    for g in range(GROUPS_PER_BLOCK):
        vg = _dot(wb_ref[g], chunk_inputs(g))
        vt_ref[g * S5_STATE:(g + 1) * S5_STATE, :] = vg[:S5_STATE]
        vt_ref[ns + g * S5_STATE:ns + (g + 1) * S5_STATE, :] = vg[S5_STATE:]
    v_ref[...] = vt_ref[...].T

    ar = a_ref[0:1, :]
    ai = a_ref[1:2, :]

    def step(c, carry):
        xr, xi = carry
        xs_ref[pl.ds(c, 1), 0:ns] = xr
        xs_ref[pl.ds(c, 1), ns:2 * ns] = xi
        vr = v_ref[pl.ds(c, 1), 0:ns]
        vi = v_ref[pl.ds(c, 1), ns:2 * ns]
        return ar * xr - ai * xi + vr, ar * xi + ai * xr + vi

    zero = jnp.zeros((1, ns), F32)
    lax.fori_loop(0, nc, step, (zero, zero), unroll=8)
    xt_ref[...] = xs_ref[...].astype(BF16).T

    for g in range(GROUPS_PER_BLOCK):
        xg = jnp.concatenate(
            [xt_ref[g * S5_STATE:(g + 1) * S5_STATE, :],
             xt_ref[ns + g * S5_STATE:ns + (g + 1) * S5_STATE, :]], axis=0)
        yg = _dot(wt_ref[g], chunk_inputs(g)) + _dot(wc_ref[g], xg)
        for t in range(S5_CHUNK):
            yt_ref[t, g * gw:(g + 1) * gw, :] = yg[t * gw:(t + 1) * gw, :]

    for t in range(S5_CHUNK):
        y = yt_ref[t].T + d_ref[...] * up_ref[t]
        yn_ref[pl.ds(t, nc, stride=S5_CHUNK), :] = jax.nn.gelu(y)
    o_ref[...] = yn_ref[...].astype(BF16)


def _s5_core(u, tab, c2, cb, b_re, b_im, rep, til, a16, dsk):
    n, sw = u.shape
    nb = sw // LANE
    nc = n // S5_CHUNK
    gb = GROUPS_PER_BLOCK
    ns2 = 2 * gb * S5_STATE
    tw = S5_CHUNK * S5_GROUP_WIDTH
    per_group = lambda a: pl.BlockSpec((gb,) + a.shape[1:], lambda j: (j,) + (0,) * (a.ndim - 1))
    whole = lambda a: pl.BlockSpec(a.shape, lambda j: (0, 0))
    table_rows = pltpu.VMEM((S5_CHUNK * gb, 2 * S5_STATE), F32)
    return pl.pallas_call(
        _s5_kernel,
        grid=(nb,),
        in_specs=[
            pl.BlockSpec((n, LANE), lambda j: (0, j)),
            per_group(tab), per_group(c2), per_group(cb), per_group(b_re), per_group(b_im),
            whole(rep), whole(til),
            pl.BlockSpec((None, 2, ns2 // 2), lambda j: (j, 0, 0)),
            pl.BlockSpec((1, LANE), lambda j: (0, j)),
        ],
        out_specs=pl.BlockSpec((n, LANE), lambda j: (0, j)),
        out_shape=jax.ShapeDtypeStruct((n, sw), BF16),
        scratch_shapes=[
            table_rows, table_rows, table_rows,
            pltpu.VMEM((gb, tw, tw), BF16),
            pltpu.VMEM((gb, 2 * S5_STATE, tw), BF16),
            pltpu.VMEM((gb, tw, 2 * S5_STATE), BF16),
            pltpu.VMEM((S5_CHUNK, nc, LANE), F32),
            pltpu.VMEM((S5_CHUNK, LANE, nc), BF16),
            pltpu.VMEM((ns2, nc), F32),
            pltpu.VMEM((nc, ns2), F32),
            pltpu.VMEM((nc, ns2), F32),
            pltpu.VMEM((ns2, nc), BF16),
            pltpu.VMEM((S5_CHUNK, LANE, nc), F32),
            pltpu.VMEM((n, LANE), F32),
        ],
        compiler_params=pltpu.CompilerParams(
            dimension_semantics=("parallel",), vmem_limit_bytes=VMEM_LIMIT),
        name="s5_core",
    )(u, tab, c2, cb, b_re, b_im, rep, til, a16, dsk)


def _s5_weights(a_re, a_im, log_dt, b_re, b_im, c_re, c_im, d_skip):
    g, p = a_re.shape
    dt = jnp.exp(log_dt)[:, None]
    decay = jnp.exp(a_re * dt)
    ab_re = decay * jnp.cos(a_im * dt)
    ab_im = decay * jnp.sin(a_im * dt)
    denom = a_re * a_re + a_im * a_im
    num_re = ab_re - 1.0
    num_im = ab_im
    k_re = (num_re * a_re + num_im * a_im) / denom
    k_im = (num_im * a_re - num_re * a_im) / denom
    tab = jnp.concatenate([ab_re, ab_re, -ab_im, ab_im, ab_im, -ab_im, k_re, k_im, k_im, k_re], axis=1)
    r, i = ab_re, ab_im
    for _ in range(S5_CHUNK.bit_length() - 1):
        r, i = r * r - i * i, 2.0 * (r * i)
    gb = GROUPS_PER_BLOCK
    a16 = jnp.stack([r.reshape(g // gb, gb * p), i.reshape(g // gb, gb * p)], axis=1)
    c2 = jnp.concatenate([c_re, -c_im], axis=-1)
    cb = jnp.concatenate([-c_im, -c_re], axis=-1)
    eye = jnp.eye(S5_GROUP_WIDTH, dtype=BF16)
    rep = jnp.repeat(eye, S5_GROUP_WIDTH, axis=1)
    til = jnp.tile(eye, (1, S5_CHUNK))
    return (tab, c2, cb, b_re, b_im, rep, til, a16, d_skip.reshape(1, -1))


def _merge_kernel(x_ref, h_ref, yap_ref, yb_ref, wglu_ref, bglu_ref, wga_ref, wgb_ref,
                  bga_ref, bgb_ref, wa_ref, wb_ref, *rest, n1, n2):
    wo_refs = rest[:n2]
    o_ref, ya_ref, m_ref = rest[n2:]
    j = pl.program_id(1)

    @pl.when(j == 0)
    def _():
        yp = yap_ref[...]
        z = _dot(yp, wglu_ref[...]) + bglu_ref[...]
        ya_ref[...] = (yp.astype(F32) * jax.nn.sigmoid(z)).astype(BF16)

    @pl.when(j < n1)
    def _():
        h = h_ref[...]
        ga = jax.nn.sigmoid(_dot(h, wga_ref[...].astype(BF16)) + bga_ref[...])
        gb = jax.nn.sigmoid(_dot(h, wgb_ref[...].astype(BF16)) + bgb_ref[...])
        pa = _dot(ya_ref[...], wa_ref[...].astype(BF16))
        pb = _dot(yb_ref[...], wb_ref[...].astype(BF16))
        m_ref[j] = (ga * pa + gb * pb).astype(BF16)

    for k in range(n2):
        @pl.when(j == n1 + k)
        def _(k=k):
            m = jnp.concatenate([m_ref[q] for q in range(n1)], axis=1)
            o_ref[...] = x_ref[...] + _dot(m, wo_refs[k][...])


def _merge(x, h, yap, yb, wglu, bglu, wgate, bgate, wa, wb, wo, *, tm=1024, t1=256, t2=512):
    n, d = x.shape
    sw = yap.shape[1]
    n1, n2 = d // t1, d // t2
    nt = n // tm
    second = lambda j: jnp.maximum(j - n1, 0)

    def first(j, back):
        return jnp.where(j < n1, j, jnp.where(j < n1 + back, n1 - 1, 0))

    once = pl.Buffered(1)
    return pl.pallas_call(
        functools.partial(_merge_kernel, n1=n1, n2=n2),
        grid=(nt, n1 + n2),
        in_specs=[
            pl.BlockSpec((tm, t2), lambda i, j: (i, second(j))),
            pl.BlockSpec((tm, d), lambda i, j: (_next_tile_after(i, j, n1, nt), 0)),
            pl.BlockSpec((tm, sw), lambda i, j: (_next_tile_after(i, j, 1, nt), 0)),
            pl.BlockSpec((tm, sw), lambda i, j: (_next_tile_after(i, j, n1, nt), 0)),
            pl.BlockSpec((sw, sw), lambda i, j: (0, 0), pipeline_mode=once),
            pl.BlockSpec((1, sw), lambda i, j: (0, 0)),
            pl.BlockSpec((d, t1), lambda i, j: (0, first(j, 1))),
            pl.BlockSpec((d, t1), lambda i, j: (0, first(j, 2) + n1)),
            pl.BlockSpec((1, t1), lambda i, j: (0, first(j, 1))),
            pl.BlockSpec((1, t1), lambda i, j: (0, first(j, 2) + n1)),
            pl.BlockSpec((sw, t1), lambda i, j: (0, first(j, 3))),
            pl.BlockSpec((sw, t1), lambda i, j: (0, first(j, 3))),
        ] + [pl.BlockSpec((d, t2), lambda i, j, k=k: (0, k), pipeline_mode=once) for k in range(n2)],
        out_specs=pl.BlockSpec((tm, t2), lambda i, j: (i, second(j))),
        out_shape=jax.ShapeDtypeStruct((n, d), F32),
        scratch_shapes=[pltpu.VMEM((tm, sw), BF16), pltpu.VMEM((n1, tm, t1), BF16)],
        compiler_params=pltpu.CompilerParams(
            dimension_semantics=("parallel", "arbitrary"), vmem_limit_bytes=VMEM_LIMIT),
        name="merge",
    )(x, h, yap, yb, wglu, bglu, wgate, wgate, bgate, bgate, wa, wb, *([wo] * n2))


def kernel(x, ffn1_norm, ffn1_w_gate, ffn1_w_up, ffn1_w_down, mix_norm, w_in, s5_a_re, s5_a_im, s5_log_dt, s5_b_re, s5_b_im, s5_c_re, s5_c_im, s5_d, s5_w_glu, s5_b_glu, sgu_ln_g, sgu_ln_b, sgu_w_s, sgu_b_s, w_branch_a, w_branch_b, w_gate, b_gate, w_out, ffn2_norm, ffn2_w_gate, ffn2_w_up, ffn2_w_down, final_norm):
    bsz, seq, d = x.shape
    depth = ffn1_norm.shape[0]
    sw = s5_w_glu.shape[1]
    assert bsz == 1 and sw == w_branch_b.shape[1] and sw % LANE == 0
    assert seq % (S5_CHUNK * LANE) == 0 and sgu_w_s.shape[2] == SGU_CHUNK
    bf = lambda w: w.astype(BF16)
    row = lambda v: v.reshape(1, -1)
    xs = x.reshape(seq, d)
    fin = row(final_norm)
    for i in range(depth):
        xs, h = _ffn(xs, row(ffn1_norm[i]), ffn1_w_gate[i], ffn1_w_up[i], ffn1_w_down[i],
                     row(mix_norm[i]), final=False)
        bsb = jnp.repeat(jnp.transpose(sgu_b_s[i]), sw // SGU_HEADS, axis=1)
        ua, yb = _mix_in(h, w_in[i], row(sgu_ln_g[i]), row(sgu_ln_b[i]), sgu_w_s[i], bsb, sw=sw)
        yap = _s5_core(ua, *_s5_weights(s5_a_re[i], s5_a_im[i], s5_log_dt[i], s5_b_re[i], s5_b_im[i],
                                        s5_c_re[i], s5_c_im[i], s5_d[i]))
        xs = _merge(xs, h, yap, yb, bf(s5_w_glu[i]), row(s5_b_glu[i]), w_gate[i], row(b_gate[i]),
                    w_branch_a[i], w_branch_b[i], bf(w_out[i]))
        if i == depth - 1:
            xs = _ffn(xs, row(ffn2_norm[i]), ffn2_w_gate[i], ffn2_w_up[i], ffn2_w_down[i], fin, final=True)
        else:
            xs, _ = _ffn(xs, row(ffn2_norm[i]), ffn2_w_gate[i], ffn2_w_up[i], ffn2_w_down[i], fin,
                         final=False)
    return xs.reshape(bsz, seq, d)
```

```python
import functools

import jax
import jax.numpy as jnp
from jax import lax
from jax.experimental import pallas as pl
from jax.experimental.pallas import tpu as pltpu

F32 = jnp.float32
BF16 = jnp.bfloat16

NORM_EPS = 1e-6
S5_GROUP_WIDTH = 16
S5_STATE = 64
SGU_HEADS = 8
SGU_CHUNK = 128
SGU_CAUSAL_BLOCK = 64
S5_CHUNK = 16
LANE = 128
GROUPS_PER_BLOCK = LANE // S5_GROUP_WIDTH
VMEM_LIMIT = 60 * 1024 * 1024
NORM_PARTS = 4
NORM_FIRST_STEP = 2


def _rms(x, g):
    ms = jnp.mean(x * x, axis=-1, keepdims=True)
    return (x * lax.rsqrt(ms + NORM_EPS)) * g


def _dot(a, b):
    return jnp.dot(a, b, preferred_element_type=F32)


def _ffn_kernel(x_hbm, g_ref, wg_ref, wu_ref, wd_ref, post_ref, o_ref, *rest, final, tm):
    hn_ref = None if final else rest[0]
    xbuf, hbuf, sem = rest[-3:]
    i = pl.program_id(0)
    j = pl.program_id(1)
    nt = pl.num_programs(0)
    last = pl.num_programs(1) - 1
    slot = lax.rem(i, 2)
    part = tm // NORM_PARTS

    def x_copy(tile):
        rows = pl.ds(pl.multiple_of(tile * tm, tm), tm)
        return pltpu.make_async_copy(x_hbm.at[rows], xbuf, sem)

    def next_x():
        return x_copy(jnp.minimum(i + 1, nt - 1))

    def accumulate(seed=False):
        h = hbuf[slot]
        a = _dot(h, wg_ref[...].astype(BF16))
        b = _dot(h, wu_ref[...].astype(BF16))
        hid = (a * jax.nn.sigmoid(a)) * b
        p = _dot(hid.astype(BF16), (0.5 * wd_ref[...]).astype(BF16))
        if seed:
            o_ref[...] = xbuf[...] + p
        else:
            o_ref[...] += p

    @pl.when(jnp.logical_and(i == 0, j == 0))
    def _():
        first = x_copy(0)
        first.start()
        first.wait()
        hbuf[0] = _rms(xbuf[...], g_ref[...]).astype(BF16)

    @pl.when(j == 0)
    def _():
        accumulate(seed=True)
        next_x().start()

    for k in range(NORM_PARTS):
        @pl.when(j == NORM_FIRST_STEP + k)
        def _(k=k):
            if k == 0:
                next_x().wait()
            rows = slice(k * part, (k + 1) * part)
            hbuf[1 - slot, rows, :] = _rms(xbuf[rows, :], g_ref[...]).astype(BF16)
            accumulate()

    plain = jnp.logical_or(jnp.logical_and(j > 0, j < NORM_FIRST_STEP),
                           jnp.logical_and(j >= NORM_FIRST_STEP + NORM_PARTS, j < last))
    pl.when(plain)(accumulate)

    @pl.when(j == last)
    def _():
        accumulate()
        y = _rms(o_ref[...], post_ref[...])
        if final:
            o_ref[...] = y
        else:
            hn_ref[...] = y.astype(BF16)


def _next_tile_after(i, j, first_step, n_tiles):
    return jnp.minimum(i + jnp.where(j >= first_step, 1, 0), n_tiles - 1)


def _ffn(x, g, wg, wu, wd, post, *, final, tm=1024, tf=256):
    n, d = x.shape
    dff = wg.shape[1]
    nt = n // tm
    row_tile = pl.BlockSpec((tm, d), lambda i, j: (i, 0))
    if final:
        out_specs, out_shape = row_tile, jax.ShapeDtypeStruct((n, d), F32)
    else:
        out_specs = [row_tile, row_tile]
        out_shape = [jax.ShapeDtypeStruct((n, d), F32), jax.ShapeDtypeStruct((n, d), BF16)]
    return pl.pallas_call(
        functools.partial(_ffn_kernel, final=final, tm=tm),
        grid=(nt, dff // tf),
        in_specs=[
            pl.BlockSpec(memory_space=pl.ANY),
            pl.BlockSpec((1, d), lambda i, j: (0, 0)),
            pl.BlockSpec((d, tf), lambda i, j: (0, j)),
            pl.BlockSpec((d, tf), lambda i, j: (0, j)),
            pl.BlockSpec((tf, d), lambda i, j: (j, 0)),
            pl.BlockSpec((1, d), lambda i, j: (0, 0)),
        ],
        out_specs=out_specs,
        out_shape=out_shape,
        scratch_shapes=[
            pltpu.VMEM((tm, d), F32),
            pltpu.VMEM((2, tm, d), BF16),
            pltpu.SemaphoreType.DMA(()),
        ],
        compiler_params=pltpu.CompilerParams(
            dimension_semantics=("arbitrary", "arbitrary"), vmem_limit_bytes=VMEM_LIMIT),
        name="ffn_final" if final else "ffn",
    )(x, g, wg, wu, wd, post)


def _mix_in_kernel(h_ref, w_ref, lng_ref, lnb_ref, ws_ref, bs_ref, ua_ref, yb_ref, vn_ref, *, sw):
    j = pl.program_id(1)

    @pl.when(j == 0)
    def _():
        ua_ref[...] = _dot(h_ref[...], w_ref[...])

    @pl.when(j == 1)
    def _():
        v = jax.nn.gelu(_dot(h_ref[...], w_ref[...]))
        mu = jnp.mean(v, axis=-1, keepdims=True)
        vc = v - mu
        var = jnp.mean(vc * vc, axis=-1, keepdims=True)
        vn_ref[...] = ((vc * lax.rsqrt(var + NORM_EPS)) * lng_ref[...] + lnb_ref[...]).astype(BF16)

    @pl.when(j == 2)
    def _():
        tm = vn_ref.shape[0]
        u = jax.nn.gelu(_dot(h_ref[...], w_ref[...]))
        r = lax.broadcasted_iota(jnp.int32, (SGU_CHUNK, SGU_CHUNK), 0) // SGU_CAUSAL_BLOCK
        c = lax.broadcasted_iota(jnp.int32, (SGU_CHUNK, SGU_CHUNK), 1) // SGU_CAUSAL_BLOCK
        keep = r >= c
        hd = sw // SGU_HEADS
        for h in range(SGU_HEADS):
            wsm = jnp.where(keep, ws_ref[h], 0.0).astype(BF16)
            cols = slice(h * hd, (h + 1) * hd)
            for q in range(tm // SGU_CHUNK):
                rows = slice(q * SGU_CHUNK, (q + 1) * SGU_CHUNK)
                mixed = _dot(wsm, vn_ref[rows, cols]) + bs_ref[:, cols]
                yb_ref[rows, cols] = (u[rows, cols] * mixed).astype(BF16)


def _mix_in(h, w_in, ln_g, ln_b, w_s, bsb, *, sw, tm=1024):
    n, d = h.shape
    assert w_in.shape[1] == 3 * sw
    nt = n // tm
    return pl.pallas_call(
        functools.partial(_mix_in_kernel, sw=sw),
        grid=(nt, 3),
        in_specs=[
            pl.BlockSpec((tm, d), lambda i, j: (i, 0)),
            pl.BlockSpec((d, sw), lambda i, j: (0, jnp.where(j == 0, 0, 3 - j))),
            pl.BlockSpec((1, sw), lambda i, j: (0, 0)),
            pl.BlockSpec((1, sw), lambda i, j: (0, 0)),
            pl.BlockSpec(w_s.shape, lambda i, j: (0, 0, 0)),
            pl.BlockSpec(bsb.shape, lambda i, j: (0, 0)),
        ],
        out_specs=[
            pl.BlockSpec((tm, sw), lambda i, j: (i, 0)),
            pl.BlockSpec((tm, sw), lambda i, j: (i, 0)),
        ],
        out_shape=[
            jax.ShapeDtypeStruct((n, sw), F32),
            jax.ShapeDtypeStruct((n, sw), BF16),
        ],
        scratch_shapes=[pltpu.VMEM((tm, sw), BF16)],
        compiler_params=pltpu.CompilerParams(
            dimension_semantics=("parallel", "arbitrary"), vmem_limit_bytes=VMEM_LIMIT),
        name="mix_in",
    )(h, w_in, ln_g, ln_b, w_s, bsb)


def _dot_split(a, b):
    a_hi = a.astype(BF16)
    b_hi = b.astype(BF16)
    a_lo = (a - a_hi.astype(F32)).astype(BF16)
    b_lo = (b - b_hi.astype(F32)).astype(BF16)
    return _dot(a_hi, b_hi) + (_dot(a_hi, b_lo) + _dot(a_lo, b_hi))


def _expand(x, sel):
    x1 = x.astype(BF16)
    r1 = x - x1.astype(F32)
    x2 = r1.astype(BF16)
    x3 = (r1 - x2.astype(F32)).astype(BF16)
    return _dot(x1, sel) + (_dot(x2, sel) + _dot(x3, sel))


def _expand_t(x, sel):
    tn = (((0,), (0,)), ((), ()))
    x1 = x.astype(BF16)
    r1 = x - x1.astype(F32)
    x2 = r1.astype(BF16)
    x3 = (r1 - x2.astype(F32)).astype(BF16)
    dot_t = lambda v: lax.dot_general(v, sel, tn, preferred_element_type=F32)
    return dot_t(x1) + (dot_t(x2) + dot_t(x3))


def _s5_kernel(u_ref, tab_ref, c2_ref, cb_ref, bre_ref, bim_ref, rep_ref, til_ref, a_ref, d_ref, o_ref,
               dn_ref, upr_ref, upi_ref, wt_ref, wb_ref, wc_ref,
               up_ref, ut_ref, vt_ref, v_ref, xs_ref, xt_ref, yt_ref, yn_ref):
    nc = u_ref.shape[0] // S5_CHUNK
    gw = S5_GROUP_WIDTH
    gb = GROUPS_PER_BLOCK
    tw = S5_CHUNK * gw
    ns = gb * S5_STATE
    sl = 2 * S5_STATE

    tab = tab_ref[...]
    ar2, an, ap = tab[:, 0:sl], tab[:, sl:2 * sl], tab[:, 2 * sl:3 * sl]
    q, qs = tab[:, 3 * sl:4 * sl], tab[:, 4 * sl:5 * sl]
    lo = lax.broadcasted_iota(jnp.int32, (gb, sl), 1) < S5_STATE
    x = jnp.where(lo, 1.0, 0.0)
    xs = jnp.where(lo, 0.0, 1.0)
    dn_ref[pl.ds((S5_CHUNK - 1) * gb, gb), :] = q
    for j in range(1, S5_CHUNK + 1):
        x, xs = x * ar2 + xs * an, xs * ar2 + x * ap
        upr_ref[pl.ds((j - 1) * gb, gb), :] = jnp.where(lo, x, xs)
        upi_ref[pl.ds((j - 1) * gb, gb), :] = jnp.where(lo, xs, x)
        if j < S5_CHUNK:
            q, qs = q * ar2 + qs * an, qs * ar2 + q * ap
            dn_ref[pl.ds((S5_CHUNK - 1 - j) * gb, gb), :] = q

    lane = lax.broadcasted_iota(jnp.int32, (gw, tw), 1)
    for g in range(gb):
        pw = _expand_t(dn_ref[pl.ds(g, S5_CHUNK, stride=gb), :], rep_ref[...])
        br = _expand(bre_ref[g], til_ref[...])
        bi = _expand(bim_ref[g], til_ref[...])
        pr, pi = pw[:S5_STATE], pw[S5_STATE:]
        wbf = jnp.concatenate([pr * br - pi * bi, pr * bi + pi * br], axis=0)
        wb_ref[g] = wbf.astype(BF16)
        hrow = _dot_split(c2_ref[g], wbf)
        ca, cb = c2_ref[g], cb_ref[g]
        for t in range(S5_CHUNK):
            shift = (gw * (t + 1)) % tw
            rolled = pltpu.roll(hrow, shift, axis=1) if shift else hrow
            wt_ref[g, t * gw:(t + 1) * gw, :] = jnp.where(lane < gw * (t + 1), rolled, 0.0).astype(BF16)
            row = t * gb + g
            wc_ref[g, t * gw:(t + 1) * gw, :] = (
                ca * upr_ref[row:row + 1, :] + cb * upi_ref[row:row + 1, :]).astype(BF16)

    for s in range(S5_CHUNK):
        piece = u_ref[pl.ds(s, nc, stride=S5_CHUNK), :]
        up_ref[s] = piece
        ut_ref[s] = piece.astype(BF16).T

    def chunk_inputs(g):
        return jnp.concatenate(
            [ut_ref[s, g * gw:(g + 1) * gw, :] for s in range(S5_CHUNK)], axis=0)

    for g in range(GROUPS_PER_BLOCK):
        vg = _dot(wb_ref[g], chunk_inputs(g))
        vt_ref[g * S5_STATE:(g + 1) * S5_STATE, :] = vg[:S5_STATE]
        vt_ref[ns + g * S5_STATE:ns + (g + 1) * S5_STATE, :] = vg[S5_STATE:]
    v_ref[...] = vt_ref[...].T

    ar = a_ref[0:1, :]
    ai = a_ref[1:2, :]

    def step(c, carry):
        xr, xi = carry
        xs_ref[pl.ds(c, 1), 0:ns] = xr
        xs_ref[pl.ds(c, 1), ns:2 * ns] = xi
        vr = v_ref[pl.ds(c, 1), 0:ns]
        vi = v_ref[pl.ds(c, 1), ns:2 * ns]
        return ar * xr - ai * xi + vr, ar * xi + ai * xr + vi

    zero = jnp.zeros((1, ns), F32)
    lax.fori_loop(0, nc, step, (zero, zero), unroll=8)
    xt_ref[...] = xs_ref[...].astype(BF16).T

    for g in range(GROUPS_PER_BLOCK):
        xg = jnp.concatenate(
            [xt_ref[g * S5_STATE:(g + 1) * S5_STATE, :],
             xt_ref[ns + g * S5_STATE:ns + (g + 1) * S5_STATE, :]], axis=0)
        yg = _dot(wt_ref[g], chunk_inputs(g)) + _dot(wc_ref[g], xg)
        for t in range(S5_CHUNK):
            yt_ref[t, g * gw:(g + 1) * gw, :] = yg[t * gw:(t + 1) * gw, :]

    for t in range(S5_CHUNK):
        y = yt_ref[t].T + d_ref[...] * up_ref[t]
        yn_ref[pl.ds(t, nc, stride=S5_CHUNK), :] = jax.nn.gelu(y)
    o_ref[...] = yn_ref[...].astype(BF16)


def _s5_core(u, tab, c2, cb, b_re, b_im, rep, til, a16, dsk):
    n, sw = u.shape
    nb = sw // LANE
    nc = n // S5_CHUNK
    gb = GROUPS_PER_BLOCK
    ns2 = 2 * gb * S5_STATE
    tw = S5_CHUNK * S5_GROUP_WIDTH
    per_group = lambda a: pl.BlockSpec((gb,) + a.shape[1:], lambda j: (j,) + (0,) * (a.ndim - 1))
    whole = lambda a: pl.BlockSpec(a.shape, lambda j: (0, 0))
    table_rows = pltpu.VMEM((S5_CHUNK * gb, 2 * S5_STATE), F32)
    return pl.pallas_call(
        _s5_kernel,
        grid=(nb,),
        in_specs=[
            pl.BlockSpec((n, LANE), lambda j: (0, j)),
            per_group(tab), per_group(c2), per_group(cb), per_group(b_re), per_group(b_im),
            whole(rep), whole(til),
            pl.BlockSpec((None, 2, ns2 // 2), lambda j: (j, 0, 0)),
            pl.BlockSpec((1, LANE), lambda j: (0, j)),
        ],
        out_specs=pl.BlockSpec((n, LANE), lambda j: (0, j)),
        out_shape=jax.ShapeDtypeStruct((n, sw), BF16),
        scratch_shapes=[
            table_rows, table_rows, table_rows,
            pltpu.VMEM((gb, tw, tw), BF16),
            pltpu.VMEM((gb, 2 * S5_STATE, tw), BF16),
            pltpu.VMEM((gb, tw, 2 * S5_STATE), BF16),
            pltpu.VMEM((S5_CHUNK, nc, LANE), F32),
            pltpu.VMEM((S5_CHUNK, LANE, nc), BF16),
            pltpu.VMEM((ns2, nc), F32),
            pltpu.VMEM((nc, ns2), F32),
            pltpu.VMEM((nc, ns2), F32),
            pltpu.VMEM((ns2, nc), BF16),
            pltpu.VMEM((S5_CHUNK, LANE, nc), F32),
            pltpu.VMEM((n, LANE), F32),
        ],
        compiler_params=pltpu.CompilerParams(
            dimension_semantics=("parallel",), vmem_limit_bytes=VMEM_LIMIT),
        name="s5_core",
    )(u, tab, c2, cb, b_re, b_im, rep, til, a16, dsk)


def _s5_weights(a_re, a_im, log_dt, b_re, b_im, c_re, c_im, d_skip):
    g, p = a_re.shape
    dt = jnp.exp(log_dt)[:, None]
    decay = jnp.exp(a_re * dt)
    ab_re = decay * jnp.cos(a_im * dt)
    ab_im = decay * jnp.sin(a_im * dt)
    denom = a_re * a_re + a_im * a_im
    num_re = ab_re - 1.0
    num_im = ab_im
    k_re = (num_re * a_re + num_im * a_im) / denom
    k_im = (num_im * a_re - num_re * a_im) / denom
    tab = jnp.concatenate([ab_re, ab_re, -ab_im, ab_im, ab_im, -ab_im, k_re, k_im, k_im, k_re], axis=1)
    r, i = ab_re, ab_im
    for _ in range(S5_CHUNK.bit_length() - 1):
        r, i = r * r - i * i, 2.0 * (r * i)
    gb = GROUPS_PER_BLOCK
    a16 = jnp.stack([r.reshape(g // gb, gb * p), i.reshape(g // gb, gb * p)], axis=1)
    c2 = jnp.concatenate([c_re, -c_im], axis=-1)
    cb = jnp.concatenate([-c_im, -c_re], axis=-1)
    eye = jnp.eye(S5_GROUP_WIDTH, dtype=BF16)
    rep = jnp.repeat(eye, S5_GROUP_WIDTH, axis=1)
    til = jnp.tile(eye, (1, S5_CHUNK))
    return (tab, c2, cb, b_re, b_im, rep, til, a16, d_skip.reshape(1, -1))


def _merge_kernel(x_ref, h_ref, yap_ref, yb_ref, wglu_ref, bglu_ref, wga_ref, wgb_ref,
                  bga_ref, bgb_ref, wa_ref, wb_ref, *rest, n1, n2):
    wo_refs = rest[:n2]
    o_ref, ya_ref, m_ref = rest[n2:]
    j = pl.program_id(1)

    @pl.when(j == 0)
    def _():
        yp = yap_ref[...]
        z = _dot(yp, wglu_ref[...]) + bglu_ref[...]
        ya_ref[...] = (yp.astype(F32) * jax.nn.sigmoid(z)).astype(BF16)

    @pl.when(j < n1)
    def _():
        h = h_ref[...]
        ga = jax.nn.sigmoid(_dot(h, wga_ref[...].astype(BF16)) + bga_ref[...])
        gb = jax.nn.sigmoid(_dot(h, wgb_ref[...].astype(BF16)) + bgb_ref[...])
        pa = _dot(ya_ref[...], wa_ref[...].astype(BF16))
        pb = _dot(yb_ref[...], wb_ref[...].astype(BF16))
        m_ref[j] = (ga * pa + gb * pb).astype(BF16)

    for k in range(n2):
        @pl.when(j == n1 + k)
        def _(k=k):
            m = jnp.concatenate([m_ref[q] for q in range(n1)], axis=1)
            o_ref[...] = x_ref[...] + _dot(m, wo_refs[k][...])


def _merge(x, h, yap, yb, wglu, bglu, wgate, bgate, wa, wb, wo, *, tm=1024, t1=256, t2=512):
    n, d = x.shape
    sw = yap.shape[1]
    n1, n2 = d // t1, d // t2
    nt = n // tm
    second = lambda j: jnp.maximum(j - n1, 0)

    def first(j, back):
        return jnp.where(j < n1, j, jnp.where(j < n1 + back, n1 - 1, 0))

    once = pl.Buffered(1)
    return pl.pallas_call(
        functools.partial(_merge_kernel, n1=n1, n2=n2),
        grid=(nt, n1 + n2),
        in_specs=[
            pl.BlockSpec((tm, t2), lambda i, j: (i, second(j))),
            pl.BlockSpec((tm, d), lambda i, j: (_next_tile_after(i, j, n1, nt), 0)),
            pl.BlockSpec((tm, sw), lambda i, j: (_next_tile_after(i, j, 1, nt), 0)),
            pl.BlockSpec((tm, sw), lambda i, j: (_next_tile_after(i, j, n1, nt), 0)),
            pl.BlockSpec((sw, sw), lambda i, j: (0, 0), pipeline_mode=once),
            pl.BlockSpec((1, sw), lambda i, j: (0, 0)),
            pl.BlockSpec((d, t1), lambda i, j: (0, first(j, 1))),
            pl.BlockSpec((d, t1), lambda i, j: (0, first(j, 2) + n1)),
            pl.BlockSpec((1, t1), lambda i, j: (0, first(j, 1))),
            pl.BlockSpec((1, t1), lambda i, j: (0, first(j, 2) + n1)),
            pl.BlockSpec((sw, t1), lambda i, j: (0, first(j, 3))),
            pl.BlockSpec((sw, t1), lambda i, j: (0, first(j, 3))),
        ] + [pl.BlockSpec((d, t2), lambda i, j, k=k: (0, k), pipeline_mode=once) for k in range(n2)],
        out_specs=pl.BlockSpec((tm, t2), lambda i, j: (i, second(j))),
        out_shape=jax.ShapeDtypeStruct((n, d), F32),
        scratch_shapes=[pltpu.VMEM((tm, sw), BF16), pltpu.VMEM((n1, tm, t1), BF16)],
        compiler_params=pltpu.CompilerParams(
            dimension_semantics=("parallel", "arbitrary"), vmem_limit_bytes=VMEM_LIMIT),
        name="merge",
    )(x, h, yap, yb, wglu, bglu, wgate, wgate, bgate, bgate, wa, wb, *([wo] * n2))


def kernel(x, ffn1_norm, ffn1_w_gate, ffn1_w_up, ffn1_w_down, mix_norm, w_in, s5_a_re, s5_a_im, s5_log_dt, s5_b_re, s5_b_im, s5_c_re, s5_c_im, s5_d, s5_w_glu, s5_b_glu, sgu_ln_g, sgu_ln_b, sgu_w_s, sgu_b_s, w_branch_a, w_branch_b, w_gate, b_gate, w_out, ffn2_norm, ffn2_w_gate, ffn2_w_up, ffn2_w_down, final_norm):
    bsz, seq, d = x.shape
    depth = ffn1_norm.shape[0]
    sw = s5_w_glu.shape[1]
    assert bsz == 1 and sw == w_branch_b.shape[1] and sw % LANE == 0
    assert seq % (S5_CHUNK * LANE) == 0 and sgu_w_s.shape[2] == SGU_CHUNK
    bf = lambda w: w.astype(BF16)
    row = lambda v: v.reshape(1, -1)
    xs = x.reshape(seq, d)
    fin = row(final_norm)
    for i in range(depth):
        xs, h = _ffn(xs, row(ffn1_norm[i]), ffn1_w_gate[i], ffn1_w_up[i], ffn1_w_down[i],
                     row(mix_norm[i]), final=False)
        bsb = jnp.repeat(jnp.transpose(sgu_b_s[i]), sw // SGU_HEADS, axis=1)
        ua, yb = _mix_in(h, bf(w_in[i]), row(sgu_ln_g[i]), row(sgu_ln_b[i]), sgu_w_s[i], bsb, sw=sw)
        yap = _s5_core(ua, *_s5_weights(s5_a_re[i], s5_a_im[i], s5_log_dt[i], s5_b_re[i], s5_b_im[i],
                                        s5_c_re[i], s5_c_im[i], s5_d[i]))
        xs = _merge(xs, h, yap, yb, bf(s5_w_glu[i]), row(s5_b_glu[i]), w_gate[i], row(b_gate[i]),
                    w_branch_a[i], w_branch_b[i], bf(w_out[i]))
        if i == depth - 1:
            xs = _ffn(xs, row(ffn2_norm[i]), ffn2_w_gate[i], ffn2_w_up[i], ffn2_w_down[i], fin, final=True)
        else:
            xs, _ = _ffn(xs, row(ffn2_norm[i]), ffn2_w_gate[i], ffn2_w_up[i], ffn2_w_down[i], fin,
                         final=False)
    return xs.reshape(bsz, seq, d)
```

```python
import functools

import jax
import jax.numpy as jnp
from jax import lax
from jax.experimental import pallas as pl
from jax.experimental.pallas import tpu as pltpu

F32 = jnp.float32
BF16 = jnp.bfloat16

NORM_EPS = 1e-6
S5_GROUP_WIDTH = 16
S5_STATE = 64
SGU_HEADS = 8
SGU_CHUNK = 128
SGU_CAUSAL_BLOCK = 64
S5_CHUNK = 16
LANE = 128
GROUPS_PER_BLOCK = LANE // S5_GROUP_WIDTH
VMEM_LIMIT = 60 * 1024 * 1024
NORM_PARTS = 4
NORM_FIRST_STEP = 6


def _rms(x, g):
    ms = jnp.mean(x * x, axis=-1, keepdims=True)
    return (x * lax.rsqrt(ms + NORM_EPS)) * g


def _dot(a, b):
    return jnp.dot(a, b, preferred_element_type=F32)


def _ffn_kernel(x_hbm, g_ref, wg_ref, wu_ref, wd_ref, post_ref, o_ref, *rest, final, tm):
    hn_ref = None if final else rest[0]
    xbuf, hbuf, sem = rest[-3:]
    i = pl.program_id(0)
    j = pl.program_id(1)
    nt = pl.num_programs(0)
    last = pl.num_programs(1) - 1
    slot = lax.rem(i, 2)
    part = tm // NORM_PARTS

    def x_copy(tile):
        rows = pl.ds(pl.multiple_of(tile * tm, tm), tm)
        return pltpu.make_async_copy(x_hbm.at[rows], xbuf, sem)

    def next_x():
        return x_copy(jnp.minimum(i + 1, nt - 1))

    def accumulate(seed=False):
        h = hbuf[slot]
        a = _dot(h, wg_ref[...].astype(BF16))
        b = _dot(h, wu_ref[...].astype(BF16))
        hid = (a * jax.nn.sigmoid(a)) * b
        p = _dot(hid.astype(BF16), (0.5 * wd_ref[...]).astype(BF16))
        if seed:
            o_ref[...] = xbuf[...] + p
        else:
            o_ref[...] += p

    @pl.when(jnp.logical_and(i == 0, j == 0))
    def _():
        first = x_copy(0)
        first.start()
        first.wait()
        hbuf[0] = _rms(xbuf[...], g_ref[...]).astype(BF16)

    @pl.when(j == 0)
    def _():
        accumulate(seed=True)

    @pl.when(j == NORM_FIRST_STEP - 3)
    def _():
        next_x().start()

    for k in range(NORM_PARTS):
        @pl.when(j == NORM_FIRST_STEP + k)
        def _(k=k):
            if k == 0:
                next_x().wait()
            rows = slice(k * part, (k + 1) * part)
            hbuf[1 - slot, rows, :] = _rms(xbuf[rows, :], g_ref[...]).astype(BF16)
            accumulate()

    plain = jnp.logical_or(jnp.logical_and(j > 0, j < NORM_FIRST_STEP),
                           jnp.logical_and(j >= NORM_FIRST_STEP + NORM_PARTS, j < last))
    pl.when(plain)(accumulate)

    @pl.when(j == last)
    def _():
        accumulate()
        y = _rms(o_ref[...], post_ref[...])
        if final:
            o_ref[...] = y
        else:
            hn_ref[...] = y.astype(BF16)


def _next_tile_after(i, j, first_step, n_tiles):
    return jnp.minimum(i + jnp.where(j >= first_step, 1, 0), n_tiles - 1)


def _ffn(x, g, wg, wu, wd, post, *, final, tm=1024, tf=256):
    n, d = x.shape
    dff = wg.shape[1]
    nt = n // tm
    row_tile = pl.BlockSpec((tm, d), lambda i, j: (i, 0))
    if final:
        out_specs, out_shape = row_tile, jax.ShapeDtypeStruct((n, d), F32)
    else:
        out_specs = [row_tile, row_tile]
        out_shape = [jax.ShapeDtypeStruct((n, d), F32), jax.ShapeDtypeStruct((n, d), BF16)]
    return pl.pallas_call(
        functools.partial(_ffn_kernel, final=final, tm=tm),
        grid=(nt, dff // tf),
        in_specs=[
            pl.BlockSpec(memory_space=pl.ANY),
            pl.BlockSpec((1, d), lambda i, j: (0, 0)),
            pl.BlockSpec((d, tf), lambda i, j: (0, j)),
            pl.BlockSpec((d, tf), lambda i, j: (0, j)),
            pl.BlockSpec((tf, d), lambda i, j: (j, 0)),
            pl.BlockSpec((1, d), lambda i, j: (0, 0)),
        ],
        out_specs=out_specs,
        out_shape=out_shape,
        scratch_shapes=[
            pltpu.VMEM((tm, d), F32),
            pltpu.VMEM((2, tm, d), BF16),
            pltpu.SemaphoreType.DMA(()),
        ],
        compiler_params=pltpu.CompilerParams(
            dimension_semantics=("arbitrary", "arbitrary"), vmem_limit_bytes=VMEM_LIMIT),
        name="ffn_final" if final else "ffn",
    )(x, g, wg, wu, wd, post)


def _mix_in_kernel(h_ref, w_ref, lng_ref, lnb_ref, ws_ref, bs_ref, ua_ref, yb_ref, vn_ref, *, sw):
    j = pl.program_id(1)

    @pl.when(j == 0)
    def _():
        ua_ref[...] = _dot(h_ref[...], w_ref[...])

    @pl.when(j == 1)
    def _():
        v = jax.nn.gelu(_dot(h_ref[...], w_ref[...]))
        mu = jnp.mean(v, axis=-1, keepdims=True)
        vc = v - mu
        var = jnp.mean(vc * vc, axis=-1, keepdims=True)
        vn_ref[...] = ((vc * lax.rsqrt(var + NORM_EPS)) * lng_ref[...] + lnb_ref[...]).astype(BF16)

    @pl.when(j == 2)
    def _():
        tm = vn_ref.shape[0]
        u = jax.nn.gelu(_dot(h_ref[...], w_ref[...]))
        r = lax.broadcasted_iota(jnp.int32, (SGU_CHUNK, SGU_CHUNK), 0) // SGU_CAUSAL_BLOCK
        c = lax.broadcasted_iota(jnp.int32, (SGU_CHUNK, SGU_CHUNK), 1) // SGU_CAUSAL_BLOCK
        keep = r >= c
        hd = sw // SGU_HEADS
        for h in range(SGU_HEADS):
            wsm = jnp.where(keep, ws_ref[h], 0.0).astype(BF16)
            cols = slice(h * hd, (h + 1) * hd)
            for q in range(tm // SGU_CHUNK):
                rows = slice(q * SGU_CHUNK, (q + 1) * SGU_CHUNK)
                mixed = _dot(wsm, vn_ref[rows, cols]) + bs_ref[:, cols]
                yb_ref[rows, cols] = (u[rows, cols] * mixed).astype(BF16)


def _mix_in(h, w_in, ln_g, ln_b, w_s, bsb, *, sw, tm=1024):
    n, d = h.shape
    assert w_in.shape[1] == 3 * sw
    nt = n // tm
    return pl.pallas_call(
        functools.partial(_mix_in_kernel, sw=sw),
        grid=(nt, 3),
        in_specs=[
            pl.BlockSpec((tm, d), lambda i, j: (i, 0)),
            pl.BlockSpec((d, sw), lambda i, j: (0, jnp.where(j == 0, 0, 3 - j))),
            pl.BlockSpec((1, sw), lambda i, j: (0, 0)),
            pl.BlockSpec((1, sw), lambda i, j: (0, 0)),
            pl.BlockSpec(w_s.shape, lambda i, j: (0, 0, 0)),
            pl.BlockSpec(bsb.shape, lambda i, j: (0, 0)),
        ],
        out_specs=[
            pl.BlockSpec((tm, sw), lambda i, j: (i, 0)),
            pl.BlockSpec((tm, sw), lambda i, j: (i, 0)),
        ],
        out_shape=[
            jax.ShapeDtypeStruct((n, sw), F32),
            jax.ShapeDtypeStruct((n, sw), BF16),
        ],
        scratch_shapes=[pltpu.VMEM((tm, sw), BF16)],
        compiler_params=pltpu.CompilerParams(
            dimension_semantics=("parallel", "arbitrary"), vmem_limit_bytes=VMEM_LIMIT),
        name="mix_in",
    )(h, w_in, ln_g, ln_b, w_s, bsb)


def _dot_split(a, b):
    a_hi = a.astype(BF16)
    b_hi = b.astype(BF16)
    a_lo = (a - a_hi.astype(F32)).astype(BF16)
    b_lo = (b - b_hi.astype(F32)).astype(BF16)
    return _dot(a_hi, b_hi) + (_dot(a_hi, b_lo) + _dot(a_lo, b_hi))


def _expand(x, sel):
    x1 = x.astype(BF16)
    r1 = x - x1.astype(F32)
    x2 = r1.astype(BF16)
    x3 = (r1 - x2.astype(F32)).astype(BF16)
    return _dot(x1, sel) + (_dot(x2, sel) + _dot(x3, sel))


def _expand_t(x, sel):
    tn = (((0,), (0,)), ((), ()))
    x1 = x.astype(BF16)
    r1 = x - x1.astype(F32)
    x2 = r1.astype(BF16)
    x3 = (r1 - x2.astype(F32)).astype(BF16)
    dot_t = lambda v: lax.dot_general(v, sel, tn, preferred_element_type=F32)
    return dot_t(x1) + (dot_t(x2) + dot_t(x3))


def _s5_kernel(u_ref, tab_ref, c2_ref, cb_ref, bre_ref, bim_ref, rep_ref, til_ref, a_ref, d_ref, o_ref,
               dn_ref, upr_ref, upi_ref, wt_ref, wb_ref, wc_ref,
               up_ref, ut_ref, vt_ref, v_ref, xs_ref, xt_ref, yt_ref, yn_ref):
    nc = u_ref.shape[0] // S5_CHUNK
    gw = S5_GROUP_WIDTH
    gb = GROUPS_PER_BLOCK
    tw = S5_CHUNK * gw
    ns = gb * S5_STATE
    sl = 2 * S5_STATE

    tab = tab_ref[...]
    ar2, an, ap = tab[:, 0:sl], tab[:, sl:2 * sl], tab[:, 2 * sl:3 * sl]
    q, qs = tab[:, 3 * sl:4 * sl], tab[:, 4 * sl:5 * sl]
    lo = lax.broadcasted_iota(jnp.int32, (gb, sl), 1) < S5_STATE
    x = jnp.where(lo, 1.0, 0.0)
    xs = jnp.where(lo, 0.0, 1.0)
    dn_ref[pl.ds((S5_CHUNK - 1) * gb, gb), :] = q
    for j in range(1, S5_CHUNK + 1):
        x, xs = x * ar2 + xs * an, xs * ar2 + x * ap
        upr_ref[pl.ds((j - 1) * gb, gb), :] = jnp.where(lo, x, xs)
        upi_ref[pl.ds((j - 1) * gb, gb), :] = jnp.where(lo, xs, x)
        if j < S5_CHUNK:
            q, qs = q * ar2 + qs * an, qs * ar2 + q * ap
            dn_ref[pl.ds((S5_CHUNK - 1 - j) * gb, gb), :] = q

    lane = lax.broadcasted_iota(jnp.int32, (gw, tw), 1)
    for g in range(gb):
        pw = _expand_t(dn_ref[pl.ds(g, S5_CHUNK, stride=gb), :], rep_ref[...])
        br = _expand(bre_ref[g], til_ref[...])
        bi = _expand(bim_ref[g], til_ref[...])
        pr, pi = pw[:S5_STATE], pw[S5_STATE:]
        wbf = jnp.concatenate([pr * br - pi * bi, pr * bi + pi * br], axis=0)
        wb_ref[g] = wbf.astype(BF16)
        hrow = _dot_split(c2_ref[g], wbf)
        ca, cb = c2_ref[g], cb_ref[g]
        for t in range(S5_CHUNK):
            shift = (gw * (t + 1)) % tw
            rolled = pltpu.roll(hrow, shift, axis=1) if shift else hrow
            wt_ref[g, t * gw:(t + 1) * gw, :] = jnp.where(lane < gw * (t + 1), rolled, 0.0).astype(BF16)
            row = t * gb + g
            wc_ref[g, t * gw:(t + 1) * gw, :] = (
                ca * upr_ref[row:row + 1, :] + cb * upi_ref[row:row + 1, :]).astype(BF16)

    for s in range(S5_CHUNK):
        piece = u_ref[pl.ds(s, nc, stride=S5_CHUNK), :]
        up_ref[s] = piece
        ut_ref[s] = piece.astype(BF16).T

    def chunk_inputs(g):
        return jnp.concatenate(
            [ut_ref[s, g * gw:(g + 1) * gw, :] for s in range(S5_CHUNK)], axis=0)

    for g in range(GROUPS_PER_BLOCK):
        vg = _dot(wb_ref[g], chunk_inputs(g))
        vt_ref[g * S5_STATE:(g + 1) * S5_STATE, :] = vg[:S5_STATE]
        vt_ref[ns + g * S5_STATE:ns + (g + 1) * S5_STATE, :] = vg[S5_STATE:]
    v_ref[...] = vt_ref[...].T

    ar = a_ref[0:1, :]
    ai = a_ref[1:2, :]

    def step(c, carry):
        xr, xi = carry
        xs_ref[pl.ds(c, 1), 0:ns] = xr
        xs_ref[pl.ds(c, 1), ns:2 * ns] = xi
        vr = v_ref[pl.ds(c, 1), 0:ns]
        vi = v_ref[pl.ds(c, 1), ns:2 * ns]
        return ar * xr - ai * xi + vr, ar * xi + ai * xr + vi

    zero = jnp.zeros((1, ns), F32)
    lax.fori_loop(0, nc, step, (zero, zero), unroll=8)
    xt_ref[...] = xs_ref[...].astype(BF16).T

    for g in range(GROUPS_PER_BLOCK):
        xg = jnp.concatenate(
            [xt_ref[g * S5_STATE:(g + 1) * S5_STATE, :],
             xt_ref[ns + g * S5_STATE:ns + (g + 1) * S5_STATE, :]], axis=0)
        yg = _dot(wt_ref[g], chunk_inputs(g)) + _dot(wc_ref[g], xg)
        for t in range(S5_CHUNK):
            yt_ref[t, g * gw:(g + 1) * gw, :] = yg[t * gw:(t + 1) * gw, :]

    for t in range(S5_CHUNK):
        y = yt_ref[t].T + d_ref[...] * up_ref[t]
        yn_ref[pl.ds(t, nc, stride=S5_CHUNK), :] = jax.nn.gelu(y)
    o_ref[...] = yn_ref[...].astype(BF16)


def _s5_core(u, tab, c2, cb, b_re, b_im, rep, til, a16, dsk):
    n, sw = u.shape
    nb = sw // LANE
    nc = n // S5_CHUNK
    gb = GROUPS_PER_BLOCK
    ns2 = 2 * gb * S5_STATE
    tw = S5_CHUNK * S5_GROUP_WIDTH
    per_group = lambda a: pl.BlockSpec((gb,) + a.shape[1:], lambda j: (j,) + (0,) * (a.ndim - 1))
    whole = lambda a: pl.BlockSpec(a.shape, lambda j: (0, 0))
    table_rows = pltpu.VMEM((S5_CHUNK * gb, 2 * S5_STATE), F32)
    return pl.pallas_call(
        _s5_kernel,
        grid=(nb,),
        in_specs=[
            pl.BlockSpec((n, LANE), lambda j: (0, j)),
            per_group(tab), per_group(c2), per_group(cb), per_group(b_re), per_group(b_im),
            whole(rep), whole(til),
            pl.BlockSpec((None, 2, ns2 // 2), lambda j: (j, 0, 0)),
            pl.BlockSpec((1, LANE), lambda j: (0, j)),
        ],
        out_specs=pl.BlockSpec((n, LANE), lambda j: (0, j)),
        out_shape=jax.ShapeDtypeStruct((n, sw), BF16),
        scratch_shapes=[
            table_rows, table_rows, table_rows,
            pltpu.VMEM((gb, tw, tw), BF16),
            pltpu.VMEM((gb, 2 * S5_STATE, tw), BF16),
            pltpu.VMEM((gb, tw, 2 * S5_STATE), BF16),
            pltpu.VMEM((S5_CHUNK, nc, LANE), F32),
            pltpu.VMEM((S5_CHUNK, LANE, nc), BF16),
            pltpu.VMEM((ns2, nc), F32),
            pltpu.VMEM((nc, ns2), F32),
            pltpu.VMEM((nc, ns2), F32),
            pltpu.VMEM((ns2, nc), BF16),
            pltpu.VMEM((S5_CHUNK, LANE, nc), F32),
            pltpu.VMEM((n, LANE), F32),
        ],
        compiler_params=pltpu.CompilerParams(
            dimension_semantics=("parallel",), vmem_limit_bytes=VMEM_LIMIT),
        name="s5_core",
    )(u, tab, c2, cb, b_re, b_im, rep, til, a16, dsk)


def _s5_weights(a_re, a_im, log_dt, b_re, b_im, c_re, c_im, d_skip):
    g, p = a_re.shape
    dt = jnp.exp(log_dt)[:, None]
    decay = jnp.exp(a_re * dt)
    ab_re = decay * jnp.cos(a_im * dt)
    ab_im = decay * jnp.sin(a_im * dt)
    denom = a_re * a_re + a_im * a_im
    num_re = ab_re - 1.0
    num_im = ab_im
    k_re = (num_re * a_re + num_im * a_im) / denom
    k_im = (num_im * a_re - num_re * a_im) / denom
    tab = jnp.concatenate([ab_re, ab_re, -ab_im, ab_im, ab_im, -ab_im, k_re, k_im, k_im, k_re], axis=1)
    r, i = ab_re, ab_im
    for _ in range(S5_CHUNK.bit_length() - 1):
        r, i = r * r - i * i, 2.0 * (r * i)
    gb = GROUPS_PER_BLOCK
    a16 = jnp.stack([r.reshape(g // gb, gb * p), i.reshape(g // gb, gb * p)], axis=1)
    c2 = jnp.concatenate([c_re, -c_im], axis=-1)
    cb = jnp.concatenate([-c_im, -c_re], axis=-1)
    eye = jnp.eye(S5_GROUP_WIDTH, dtype=BF16)
    rep = jnp.repeat(eye, S5_GROUP_WIDTH, axis=1)
    til = jnp.tile(eye, (1, S5_CHUNK))
    return (tab, c2, cb, b_re, b_im, rep, til, a16, d_skip.reshape(1, -1))


def _merge_kernel(x_ref, h_ref, yap_ref, yb_ref, wglu_ref, bglu_ref, wga_ref, wgb_ref,
                  bga_ref, bgb_ref, wa_ref, wb_ref, *rest, n1, n2):
    wo_refs = rest[:n2]
    o_ref, ya_ref, m_ref = rest[n2:]
    j = pl.program_id(1)

    @pl.when(j == 0)
    def _():
        yp = yap_ref[...]
        z = _dot(yp, wglu_ref[...]) + bglu_ref[...]
        ya_ref[...] = (yp.astype(F32) * jax.nn.sigmoid(z)).astype(BF16)

    @pl.when(j < n1)
    def _():
        h = h_ref[...]
        ga = jax.nn.sigmoid(_dot(h, wga_ref[...].astype(BF16)) + bga_ref[...])
        gb = jax.nn.sigmoid(_dot(h, wgb_ref[...].astype(BF16)) + bgb_ref[...])
        pa = _dot(ya_ref[...], wa_ref[...].astype(BF16))
        pb = _dot(yb_ref[...], wb_ref[...].astype(BF16))
        m_ref[j] = (ga * pa + gb * pb).astype(BF16)

    for k in range(n2):
        @pl.when(j == n1 + k)
        def _(k=k):
            m = jnp.concatenate([m_ref[q] for q in range(n1)], axis=1)
            o_ref[...] = x_ref[...] + _dot(m, wo_refs[k][...])


def _merge(x, h, yap, yb, wglu, bglu, wgate, bgate, wa, wb, wo, *, tm=1024, t1=256, t2=512):
    n, d = x.shape
    sw = yap.shape[1]
    n1, n2 = d // t1, d // t2
    nt = n // tm
    second = lambda j: jnp.maximum(j - n1, 0)

    def first(j, back):
        return jnp.where(j < n1, j, jnp.where(j < n1 + back, n1 - 1, 0))

    once = pl.Buffered(1)
    return pl.pallas_call(
        functools.partial(_merge_kernel, n1=n1, n2=n2),
        grid=(nt, n1 + n2),
        in_specs=[
            pl.BlockSpec((tm, t2), lambda i, j: (i, second(j))),
            pl.BlockSpec((tm, d), lambda i, j: (_next_tile_after(i, j, n1, nt), 0)),
            pl.BlockSpec((tm, sw), lambda i, j: (_next_tile_after(i, j, 1, nt), 0)),
            pl.BlockSpec((tm, sw), lambda i, j: (_next_tile_after(i, j, n1, nt), 0)),
            pl.BlockSpec((sw, sw), lambda i, j: (0, 0), pipeline_mode=once),
            pl.BlockSpec((1, sw), lambda i, j: (0, 0)),
            pl.BlockSpec((d, t1), lambda i, j: (0, first(j, 1))),
            pl.BlockSpec((d, t1), lambda i, j: (0, first(j, 2) + n1)),
            pl.BlockSpec((1, t1), lambda i, j: (0, first(j, 1))),
            pl.BlockSpec((1, t1), lambda i, j: (0, first(j, 2) + n1)),
            pl.BlockSpec((sw, t1), lambda i, j: (0, first(j, 3))),
            pl.BlockSpec((sw, t1), lambda i, j: (0, first(j, 3))),
        ] + [pl.BlockSpec((d, t2), lambda i, j, k=k: (0, k), pipeline_mode=once) for k in range(n2)],
        out_specs=pl.BlockSpec((tm, t2), lambda i, j: (i, second(j))),
        out_shape=jax.ShapeDtypeStruct((n, d), F32),
        scratch_shapes=[pltpu.VMEM((tm, sw), BF16), pltpu.VMEM((n1, tm, t1), BF16)],
        compiler_params=pltpu.CompilerParams(
            dimension_semantics=("parallel", "arbitrary"), vmem_limit_bytes=VMEM_LIMIT),
        name="merge",
    )(x, h, yap, yb, wglu, bglu, wgate, wgate, bgate, bgate, wa, wb, *([wo] * n2))


def kernel(x, ffn1_norm, ffn1_w_gate, ffn1_w_up, ffn1_w_down, mix_norm, w_in, s5_a_re, s5_a_im, s5_log_dt, s5_b_re, s5_b_im, s5_c_re, s5_c_im, s5_d, s5_w_glu, s5_b_glu, sgu_ln_g, sgu_ln_b, sgu_w_s, sgu_b_s, w_branch_a, w_branch_b, w_gate, b_gate, w_out, ffn2_norm, ffn2_w_gate, ffn2_w_up, ffn2_w_down, final_norm):
    bsz, seq, d = x.shape
    depth = ffn1_norm.shape[0]
    sw = s5_w_glu.shape[1]
    assert bsz == 1 and sw == w_branch_b.shape[1] and sw % LANE == 0
    assert seq % (S5_CHUNK * LANE) == 0 and sgu_w_s.shape[2] == SGU_CHUNK
    bf = lambda w: w.astype(BF16)
    row = lambda v: v.reshape(1, -1)
    xs = x.reshape(seq, d)
    fin = row(final_norm)
    for i in range(depth):
        xs, h = _ffn(xs, row(ffn1_norm[i]), ffn1_w_gate[i], ffn1_w_up[i], ffn1_w_down[i],
                     row(mix_norm[i]), final=False)
        bsb = jnp.repeat(jnp.transpose(sgu_b_s[i]), sw // SGU_HEADS, axis=1)
        ua, yb = _mix_in(h, bf(w_in[i]), row(sgu_ln_g[i]), row(sgu_ln_b[i]), sgu_w_s[i], bsb, sw=sw)
        yap = _s5_core(ua, *_s5_weights(s5_a_re[i], s5_a_im[i], s5_log_dt[i], s5_b_re[i], s5_b_im[i],
                                        s5_c_re[i], s5_c_im[i], s5_d[i]))
        xs = _merge(xs, h, yap, yb, bf(s5_w_glu[i]), row(s5_b_glu[i]), w_gate[i], row(b_gate[i]),
                    w_branch_a[i], w_branch_b[i], bf(w_out[i]))
        if i == depth - 1:
            xs = _ffn(xs, row(ffn2_norm[i]), ffn2_w_gate[i], ffn2_w_up[i], ffn2_w_down[i], fin, final=True)
        else:
            xs, _ = _ffn(xs, row(ffn2_norm[i]), ffn2_w_gate[i], ffn2_w_up[i], ffn2_w_down[i], fin,
                         final=False)
    return xs.reshape(bsz, seq, d)
```

```python
import functools

import jax
import jax.numpy as jnp
from jax import lax
from jax.experimental import pallas as pl
from jax.experimental.pallas import tpu as pltpu

F32 = jnp.float32
BF16 = jnp.bfloat16

NORM_EPS = 1e-6
S5_GROUP_WIDTH = 16
S5_STATE = 64
SGU_HEADS = 8
SGU_CHUNK = 128
SGU_CAUSAL_BLOCK = 64
S5_CHUNK = 16
LANE = 128
GROUPS_PER_BLOCK = LANE // S5_GROUP_WIDTH
VMEM_LIMIT = 60 * 1024 * 1024
NORM_PARTS = 4
NORM_FIRST_STEP = 6


def _rms(x, g):
    ms = jnp.mean(x * x, axis=-1, keepdims=True)
    return (x * lax.rsqrt(ms + NORM_EPS)) * g


def _dot(a, b):
    return jnp.dot(a, b, preferred_element_type=F32)


def _ffn_kernel(x_hbm, g_ref, wg_ref, wu_ref, wd_ref, post_ref, o_ref, *rest, final, tm):
    hn_ref = None if final else rest[0]
    xbuf, hbuf, sem = rest[-3:]
    i = pl.program_id(0)
    j = pl.program_id(1)
    nt = pl.num_programs(0)
    last = pl.num_programs(1) - 1
    slot = lax.rem(i, 2)
    part = tm // NORM_PARTS

    def x_copy(tile, k):
        src = pl.ds(pl.multiple_of(tile * tm + k * part, part), part)
        return pltpu.make_async_copy(x_hbm.at[src], xbuf.at[pl.ds(k * part, part)], sem.at[k])

    def next_x(k):
        return x_copy(jnp.minimum(i + 1, nt - 1), k)

    def accumulate(seed=False):
        h = hbuf[slot]
        a = _dot(h, wg_ref[...].astype(BF16))
        b = _dot(h, wu_ref[...].astype(BF16))
        hid = (a * jax.nn.sigmoid(a)) * b
        p = _dot(hid.astype(BF16), (0.5 * wd_ref[...]).astype(BF16))
        if seed:
            o_ref[...] = xbuf[...] + p
        else:
            o_ref[...] += p

    @pl.when(jnp.logical_and(i == 0, j == 0))
    def _():
        for k in range(NORM_PARTS):
            x_copy(0, k).start()
        for k in range(NORM_PARTS):
            x_copy(0, k).wait()
        hbuf[0] = _rms(xbuf[...], g_ref[...]).astype(BF16)

    @pl.when(j == 0)
    def _():
        accumulate(seed=True)

    for k in range(NORM_PARTS):
        @pl.when(j == NORM_FIRST_STEP - NORM_PARTS + k)
        def _(k=k):
            next_x(k).start()

    for k in range(NORM_PARTS):
        @pl.when(j == NORM_FIRST_STEP + k)
        def _(k=k):
            next_x(k).wait()
            rows = slice(k * part, (k + 1) * part)
            hbuf[1 - slot, rows, :] = _rms(xbuf[rows, :], g_ref[...]).astype(BF16)
            accumulate()

    plain = jnp.logical_or(jnp.logical_and(j > 0, j < NORM_FIRST_STEP),
                           jnp.logical_and(j >= NORM_FIRST_STEP + NORM_PARTS, j < last))
    pl.when(plain)(accumulate)

    @pl.when(j == last)
    def _():
        accumulate()
        y = _rms(o_ref[...], post_ref[...])
        if final:
            o_ref[...] = y
        else:
            hn_ref[...] = y.astype(BF16)


def _next_tile_after(i, j, first_step, n_tiles):
    return jnp.minimum(i + jnp.where(j >= first_step, 1, 0), n_tiles - 1)


def _ffn(x, g, wg, wu, wd, post, *, final, tm=1024, tf=256):
    n, d = x.shape
    dff = wg.shape[1]
    nt = n // tm
    row_tile = pl.BlockSpec((tm, d), lambda i, j: (i, 0))
    if final:
        out_specs, out_shape = row_tile, jax.ShapeDtypeStruct((n, d), F32)
    else:
        out_specs = [row_tile, row_tile]
        out_shape = [jax.ShapeDtypeStruct((n, d), F32), jax.ShapeDtypeStruct((n, d), BF16)]
    return pl.pallas_call(
        functools.partial(_ffn_kernel, final=final, tm=tm),
        grid=(nt, dff // tf),
        in_specs=[
            pl.BlockSpec(memory_space=pl.ANY),
            pl.BlockSpec((1, d), lambda i, j: (0, 0)),
            pl.BlockSpec((d, tf), lambda i, j: (0, j)),
            pl.BlockSpec((d, tf), lambda i, j: (0, j)),
            pl.BlockSpec((tf, d), lambda i, j: (j, 0)),
            pl.BlockSpec((1, d), lambda i, j: (0, 0)),
        ],
        out_specs=out_specs,
        out_shape=out_shape,
        scratch_shapes=[
            pltpu.VMEM((tm, d), F32),
            pltpu.VMEM((2, tm, d), BF16),
            pltpu.SemaphoreType.DMA((NORM_PARTS,)),
        ],
        compiler_params=pltpu.CompilerParams(
            dimension_semantics=("arbitrary", "arbitrary"), vmem_limit_bytes=VMEM_LIMIT),
        name="ffn_final" if final else "ffn",
    )(x, g, wg, wu, wd, post)


def _mix_in_kernel(h_ref, w_ref, lng_ref, lnb_ref, ws_ref, bs_ref, ua_ref, yb_ref, vn_ref, *, sw):
    j = pl.program_id(1)

    @pl.when(j == 0)
    def _():
        ua_ref[...] = _dot(h_ref[...], w_ref[...])

    @pl.when(j == 1)
    def _():
        v = jax.nn.gelu(_dot(h_ref[...], w_ref[...]))
        mu = jnp.mean(v, axis=-1, keepdims=True)
        vc = v - mu
        var = jnp.mean(vc * vc, axis=-1, keepdims=True)
        vn_ref[...] = ((vc * lax.rsqrt(var + NORM_EPS)) * lng_ref[...] + lnb_ref[...]).astype(BF16)

    @pl.when(j == 2)
    def _():
        tm = vn_ref.shape[0]
        u = jax.nn.gelu(_dot(h_ref[...], w_ref[...]))
        r = lax.broadcasted_iota(jnp.int32, (SGU_CHUNK, SGU_CHUNK), 0) // SGU_CAUSAL_BLOCK
        c = lax.broadcasted_iota(jnp.int32, (SGU_CHUNK, SGU_CHUNK), 1) // SGU_CAUSAL_BLOCK
        keep = r >= c
        hd = sw // SGU_HEADS
        for h in range(SGU_HEADS):
            wsm = jnp.where(keep, ws_ref[h], 0.0).astype(BF16)
            cols = slice(h * hd, (h + 1) * hd)
            for q in range(tm // SGU_CHUNK):
                rows = slice(q * SGU_CHUNK, (q + 1) * SGU_CHUNK)
                mixed = _dot(wsm, vn_ref[rows, cols]) + bs_ref[:, cols]
                yb_ref[rows, cols] = (u[rows, cols] * mixed).astype(BF16)


def _mix_in(h, w_in, ln_g, ln_b, w_s, bsb, *, sw, tm=1024):
    n, d = h.shape
    assert w_in.shape[1] == 3 * sw
    nt = n // tm
    return pl.pallas_call(
        functools.partial(_mix_in_kernel, sw=sw),
        grid=(nt, 3),
        in_specs=[
            pl.BlockSpec((tm, d), lambda i, j: (i, 0)),
            pl.BlockSpec((d, sw), lambda i, j: (0, jnp.where(j == 0, 0, 3 - j))),
            pl.BlockSpec((1, sw), lambda i, j: (0, 0)),
            pl.BlockSpec((1, sw), lambda i, j: (0, 0)),
            pl.BlockSpec(w_s.shape, lambda i, j: (0, 0, 0)),
            pl.BlockSpec(bsb.shape, lambda i, j: (0, 0)),
        ],
        out_specs=[
            pl.BlockSpec((tm, sw), lambda i, j: (i, 0)),
            pl.BlockSpec((tm, sw), lambda i, j: (i, 0)),
        ],
        out_shape=[
            jax.ShapeDtypeStruct((n, sw), F32),
            jax.ShapeDtypeStruct((n, sw), BF16),
        ],
        scratch_shapes=[pltpu.VMEM((tm, sw), BF16)],
        compiler_params=pltpu.CompilerParams(
            dimension_semantics=("parallel", "arbitrary"), vmem_limit_bytes=VMEM_LIMIT),
        name="mix_in",
    )(h, w_in, ln_g, ln_b, w_s, bsb)


def _dot_split(a, b):
    a_hi = a.astype(BF16)
    b_hi = b.astype(BF16)
    a_lo = (a - a_hi.astype(F32)).astype(BF16)
    b_lo = (b - b_hi.astype(F32)).astype(BF16)
    return _dot(a_hi, b_hi) + (_dot(a_hi, b_lo) + _dot(a_lo, b_hi))


def _expand(x, sel):
    x1 = x.astype(BF16)
    r1 = x - x1.astype(F32)
    x2 = r1.astype(BF16)
    x3 = (r1 - x2.astype(F32)).astype(BF16)
    return _dot(x1, sel) + (_dot(x2, sel) + _dot(x3, sel))


def _expand_t(x, sel):
    tn = (((0,), (0,)), ((), ()))
    x1 = x.astype(BF16)
    r1 = x - x1.astype(F32)
    x2 = r1.astype(BF16)
    x3 = (r1 - x2.astype(F32)).astype(BF16)
    dot_t = lambda v: lax.dot_general(v, sel, tn, preferred_element_type=F32)
    return dot_t(x1) + (dot_t(x2) + dot_t(x3))


def _s5_kernel(u_ref, tab_ref, c2_ref, cb_ref, bre_ref, bim_ref, rep_ref, til_ref, a_ref, d_ref, o_ref,
               dn_ref, upr_ref, upi_ref, wt_ref, wb_ref, wc_ref,
               up_ref, ut_ref, vt_ref, v_ref, xs_ref, xt_ref, yt_ref, yn_ref):
    nc = u_ref.shape[0] // S5_CHUNK
    gw = S5_GROUP_WIDTH
    gb = GROUPS_PER_BLOCK
    tw = S5_CHUNK * gw
    ns = gb * S5_STATE
    sl = 2 * S5_STATE

    tab = tab_ref[...]
    ar2, an, ap = tab[:, 0:sl], tab[:, sl:2 * sl], tab[:, 2 * sl:3 * sl]
    q, qs = tab[:, 3 * sl:4 * sl], tab[:, 4 * sl:5 * sl]
    lo = lax.broadcasted_iota(jnp.int32, (gb, sl), 1) < S5_STATE
    x = jnp.where(lo, 1.0, 0.0)
    xs = jnp.where(lo, 0.0, 1.0)
    dn_ref[pl.ds((S5_CHUNK - 1) * gb, gb), :] = q
    for j in range(1, S5_CHUNK + 1):
        x, xs = x * ar2 + xs * an, xs * ar2 + x * ap
        upr_ref[pl.ds((j - 1) * gb, gb), :] = jnp.where(lo, x, xs)
        upi_ref[pl.ds((j - 1) * gb, gb), :] = jnp.where(lo, xs, x)
        if j < S5_CHUNK:
            q, qs = q * ar2 + qs * an, qs * ar2 + q * ap
            dn_ref[pl.ds((S5_CHUNK - 1 - j) * gb, gb), :] = q

    lane = lax.broadcasted_iota(jnp.int32, (gw, tw), 1)
    for g in range(gb):
        pw = _expand_t(dn_ref[pl.ds(g, S5_CHUNK, stride=gb), :], rep_ref[...])
        br = _expand(bre_ref[g], til_ref[...])
        bi = _expand(bim_ref[g], til_ref[...])
        pr, pi = pw[:S5_STATE], pw[S5_STATE:]
        wbf = jnp.concatenate([pr * br - pi * bi, pr * bi + pi * br], axis=0)
        wb_ref[g] = wbf.astype(BF16)
        hrow = _dot_split(c2_ref[g], wbf)
        ca, cb = c2_ref[g], cb_ref[g]
        for t in range(S5_CHUNK):
            shift = (gw * (t + 1)) % tw
            rolled = pltpu.roll(hrow, shift, axis=1) if shift else hrow
            wt_ref[g, t * gw:(t + 1) * gw, :] = jnp.where(lane < gw * (t + 1), rolled, 0.0).astype(BF16)
            row = t * gb + g
            wc_ref[g, t * gw:(t + 1) * gw, :] = (
                ca * upr_ref[row:row + 1, :] + cb * upi_ref[row:row + 1, :]).astype(BF16)

    for s in range(S5_CHUNK):
        piece = u_ref[pl.ds(s, nc, stride=S5_CHUNK), :]
        up_ref[s] = piece
        ut_ref[s] = piece.astype(BF16).T

    def chunk_inputs(g):
        return jnp.concatenate(
            [ut_ref[s, g * gw:(g + 1) * gw, :] for s in range(S5_CHUNK)], axis=0)

    for g in range(GROUPS_PER_BLOCK):
        vg = _dot(wb_ref[g], chunk_inputs(g))
        vt_ref[g * S5_STATE:(g + 1) * S5_STATE, :] = vg[:S5_STATE]
        vt_ref[ns + g * S5_STATE:ns + (g + 1) * S5_STATE, :] = vg[S5_STATE:]
    v_ref[...] = vt_ref[...].T

    ar = a_ref[0:1, :]
    ai = a_ref[1:2, :]

    def step(c, carry):
        xr, xi = carry
        xs_ref[pl.ds(c, 1), 0:ns] = xr
        xs_ref[pl.ds(c, 1), ns:2 * ns] = xi
        vr = v_ref[pl.ds(c, 1), 0:ns]
        vi = v_ref[pl.ds(c, 1), ns:2 * ns]
        return ar * xr - ai * xi + vr, ar * xi + ai * xr + vi

    zero = jnp.zeros((1, ns), F32)
    lax.fori_loop(0, nc, step, (zero, zero), unroll=8)
    xt_ref[...] = xs_ref[...].astype(BF16).T

    for g in range(GROUPS_PER_BLOCK):
        xg = jnp.concatenate(
            [xt_ref[g * S5_STATE:(g + 1) * S5_STATE, :],
             xt_ref[ns + g * S5_STATE:ns + (g + 1) * S5_STATE, :]], axis=0)
        yg = _dot(wt_ref[g], chunk_inputs(g)) + _dot(wc_ref[g], xg)
        for t in range(S5_CHUNK):
            yt_ref[t, g * gw:(g + 1) * gw, :] = yg[t * gw:(t + 1) * gw, :]

    for t in range(S5_CHUNK):
        y = yt_ref[t].T + d_ref[...] * up_ref[t]
        yn_ref[pl.ds(t, nc, stride=S5_CHUNK), :] = jax.nn.gelu(y)
    o_ref[...] = yn_ref[...].astype(BF16)


def _s5_core(u, tab, c2, cb, b_re, b_im, rep, til, a16, dsk):
    n, sw = u.shape
    nb = sw // LANE
    nc = n // S5_CHUNK
    gb = GROUPS_PER_BLOCK
    ns2 = 2 * gb * S5_STATE
    tw = S5_CHUNK * S5_GROUP_WIDTH
    per_group = lambda a: pl.BlockSpec((gb,) + a.shape[1:], lambda j: (j,) + (0,) * (a.ndim - 1))
    whole = lambda a: pl.BlockSpec(a.shape, lambda j: (0, 0))
    table_rows = pltpu.VMEM((S5_CHUNK * gb, 2 * S5_STATE), F32)
    return pl.pallas_call(
        _s5_kernel,
        grid=(nb,),
        in_specs=[
            pl.BlockSpec((n, LANE), lambda j: (0, j)),
            per_group(tab), per_group(c2), per_group(cb), per_group(b_re), per_group(b_im),
            whole(rep), whole(til),
            pl.BlockSpec((None, 2, ns2 // 2), lambda j: (j, 0, 0)),
            pl.BlockSpec((1, LANE), lambda j: (0, j)),
        ],
        out_specs=pl.BlockSpec((n, LANE), lambda j: (0, j)),
        out_shape=jax.ShapeDtypeStruct((n, sw), BF16),
        scratch_shapes=[
            table_rows, table_rows, table_rows,
            pltpu.VMEM((gb, tw, tw), BF16),
            pltpu.VMEM((gb, 2 * S5_STATE, tw), BF16),
            pltpu.VMEM((gb, tw, 2 * S5_STATE), BF16),
            pltpu.VMEM((S5_CHUNK, nc, LANE), F32),
            pltpu.VMEM((S5_CHUNK, LANE, nc), BF16),
            pltpu.VMEM((ns2, nc), F32),
            pltpu.VMEM((nc, ns2), F32),
            pltpu.VMEM((nc, ns2), F32),
            pltpu.VMEM((ns2, nc), BF16),
            pltpu.VMEM((S5_CHUNK, LANE, nc), F32),
            pltpu.VMEM((n, LANE), F32),
        ],
        compiler_params=pltpu.CompilerParams(
            dimension_semantics=("parallel",), vmem_limit_bytes=VMEM_LIMIT),
        name="s5_core",
    )(u, tab, c2, cb, b_re, b_im, rep, til, a16, dsk)


def _s5_weights(a_re, a_im, log_dt, b_re, b_im, c_re, c_im, d_skip):
    g, p = a_re.shape
    dt = jnp.exp(log_dt)[:, None]
    decay = jnp.exp(a_re * dt)
    ab_re = decay * jnp.cos(a_im * dt)
    ab_im = decay * jnp.sin(a_im * dt)
    denom = a_re * a_re + a_im * a_im
    num_re = ab_re - 1.0
    num_im = ab_im
    k_re = (num_re * a_re + num_im * a_im) / denom
    k_im = (num_im * a_re - num_re * a_im) / denom
    tab = jnp.concatenate([ab_re, ab_re, -ab_im, ab_im, ab_im, -ab_im, k_re, k_im, k_im, k_re], axis=1)
    r, i = ab_re, ab_im
    for _ in range(S5_CHUNK.bit_length() - 1):
        r, i = r * r - i * i, 2.0 * (r * i)
    gb = GROUPS_PER_BLOCK
    a16 = jnp.stack([r.reshape(g // gb, gb * p), i.reshape(g // gb, gb * p)], axis=1)
    c2 = jnp.concatenate([c_re, -c_im], axis=-1)
    cb = jnp.concatenate([-c_im, -c_re], axis=-1)
    eye = jnp.eye(S5_GROUP_WIDTH, dtype=BF16)
    rep = jnp.repeat(eye, S5_GROUP_WIDTH, axis=1)
    til = jnp.tile(eye, (1, S5_CHUNK))
    return (tab, c2, cb, b_re, b_im, rep, til, a16, d_skip.reshape(1, -1))


def _merge_kernel(x_ref, h_ref, yap_ref, yb_ref, wglu_ref, bglu_ref, wga_ref, wgb_ref,
                  bga_ref, bgb_ref, wa_ref, wb_ref, *rest, n1, n2):
    wo_refs = rest[:n2]
    o_ref, ya_ref, m_ref = rest[n2:]
    j = pl.program_id(1)

    @pl.when(j == 0)
    def _():
        yp = yap_ref[...]
        z = _dot(yp, wglu_ref[...]) + bglu_ref[...]
        ya_ref[...] = (yp.astype(F32) * jax.nn.sigmoid(z)).astype(BF16)

    @pl.when(j < n1)
    def _():
        h = h_ref[...]
        ga = jax.nn.sigmoid(_dot(h, wga_ref[...].astype(BF16)) + bga_ref[...])
        gb = jax.nn.sigmoid(_dot(h, wgb_ref[...].astype(BF16)) + bgb_ref[...])
        pa = _dot(ya_ref[...], wa_ref[...].astype(BF16))
        pb = _dot(yb_ref[...], wb_ref[...].astype(BF16))
        m_ref[j] = (ga * pa + gb * pb).astype(BF16)

    for k in range(n2):
        @pl.when(j == n1 + k)
        def _(k=k):
            m = jnp.concatenate([m_ref[q] for q in range(n1)], axis=1)
            o_ref[...] = x_ref[...] + _dot(m, wo_refs[k][...])


def _merge(x, h, yap, yb, wglu, bglu, wgate, bgate, wa, wb, wo, *, tm=1024, t1=256, t2=512):
    n, d = x.shape
    sw = yap.shape[1]
    n1, n2 = d // t1, d // t2
    nt = n // tm
    second = lambda j: jnp.maximum(j - n1, 0)

    def first(j, back):
        return jnp.where(j < n1, j, jnp.where(j < n1 + back, n1 - 1, 0))

    once = pl.Buffered(1)
    return pl.pallas_call(
        functools.partial(_merge_kernel, n1=n1, n2=n2),
        grid=(nt, n1 + n2),
        in_specs=[
            pl.BlockSpec((tm, t2), lambda i, j: (i, second(j))),
            pl.BlockSpec((tm, d), lambda i, j: (_next_tile_after(i, j, n1, nt), 0)),
            pl.BlockSpec((tm, sw), lambda i, j: (_next_tile_after(i, j, 1, nt), 0)),
            pl.BlockSpec((tm, sw), lambda i, j: (_next_tile_after(i, j, n1, nt), 0)),
            pl.BlockSpec((sw, sw), lambda i, j: (0, 0), pipeline_mode=once),
            pl.BlockSpec((1, sw), lambda i, j: (0, 0)),
            pl.BlockSpec((d, t1), lambda i, j: (0, first(j, 1))),
            pl.BlockSpec((d, t1), lambda i, j: (0, first(j, 2) + n1)),
            pl.BlockSpec((1, t1), lambda i, j: (0, first(j, 1))),
            pl.BlockSpec((1, t1), lambda i, j: (0, first(j, 2) + n1)),
            pl.BlockSpec((sw, t1), lambda i, j: (0, first(j, 3))),
            pl.BlockSpec((sw, t1), lambda i, j: (0, first(j, 3))),
        ] + [pl.BlockSpec((d, t2), lambda i, j, k=k: (0, k), pipeline_mode=once) for k in range(n2)],
        out_specs=pl.BlockSpec((tm, t2), lambda i, j: (i, second(j))),
        out_shape=jax.ShapeDtypeStruct((n, d), F32),
        scratch_shapes=[pltpu.VMEM((tm, sw), BF16), pltpu.VMEM((n1, tm, t1), BF16)],
        compiler_params=pltpu.CompilerParams(
            dimension_semantics=("parallel", "arbitrary"), vmem_limit_bytes=VMEM_LIMIT),
        name="merge",
    )(x, h, yap, yb, wglu, bglu, wgate, wgate, bgate, bgate, wa, wb, *([wo] * n2))


def kernel(x, ffn1_norm, ffn1_w_gate, ffn1_w_up, ffn1_w_down, mix_norm, w_in, s5_a_re, s5_a_im, s5_log_dt, s5_b_re, s5_b_im, s5_c_re, s5_c_im, s5_d, s5_w_glu, s5_b_glu, sgu_ln_g, sgu_ln_b, sgu_w_s, sgu_b_s, w_branch_a, w_branch_b, w_gate, b_gate, w_out, ffn2_norm, ffn2_w_gate, ffn2_w_up, ffn2_w_down, final_norm):
    bsz, seq, d = x.shape
    depth = ffn1_norm.shape[0]
    sw = s5_w_glu.shape[1]
    assert bsz == 1 and sw == w_branch_b.shape[1] and sw % LANE == 0
    assert seq % (S5_CHUNK * LANE) == 0 and sgu_w_s.shape[2] == SGU_CHUNK
    bf = lambda w: w.astype(BF16)
    row = lambda v: v.reshape(1, -1)
    xs = x.reshape(seq, d)
    fin = row(final_norm)
    for i in range(depth):
        xs, h = _ffn(xs, row(ffn1_norm[i]), ffn1_w_gate[i], ffn1_w_up[i], ffn1_w_down[i],
                     row(mix_norm[i]), final=False)
        bsb = jnp.repeat(jnp.transpose(sgu_b_s[i]), sw // SGU_HEADS, axis=1)
        ua, yb = _mix_in(h, bf(w_in[i]), row(sgu_ln_g[i]), row(sgu_ln_b[i]), sgu_w_s[i], bsb, sw=sw)
        yap = _s5_core(ua, *_s5_weights(s5_a_re[i], s5_a_im[i], s5_log_dt[i], s5_b_re[i], s5_b_im[i],
                                        s5_c_re[i], s5_c_im[i], s5_d[i]))
        xs = _merge(xs, h, yap, yb, bf(s5_w_glu[i]), row(s5_b_glu[i]), w_gate[i], row(b_gate[i]),
                    w_branch_a[i], w_branch_b[i], bf(w_out[i]))
        if i == depth - 1:
            xs = _ffn(xs, row(ffn2_norm[i]), ffn2_w_gate[i], ffn2_w_up[i], ffn2_w_down[i], fin, final=True)
        else:
            xs, _ = _ffn(xs, row(ffn2_norm[i]), ffn2_w_gate[i], ffn2_w_up[i], ffn2_w_down[i], fin,
                         final=False)
    return xs.reshape(bsz, seq, d)
```

```python
import functools

import jax
import jax.numpy as jnp
from jax import lax
from jax.experimental import pallas as pl
from jax.experimental.pallas import tpu as pltpu

F32 = jnp.float32
BF16 = jnp.bfloat16

NORM_EPS = 1e-6
S5_GROUP_WIDTH = 16
S5_STATE = 64
SGU_HEADS = 8
SGU_CHUNK = 128
SGU_CAUSAL_BLOCK = 64
S5_CHUNK = 16
LANE = 128
GROUPS_PER_BLOCK = LANE // S5_GROUP_WIDTH
VMEM_LIMIT = 60 * 1024 * 1024
ROW_CHUNK = 512


def _rms(x, g):
    ms = jnp.mean(x * x, axis=-1, keepdims=True)
    return (x * lax.rsqrt(ms + NORM_EPS)) * g


def _dot(a, b):
    return jnp.dot(a, b, preferred_element_type=F32)


def _ffn_kernel(x_ref, g_ref, wg_ref, wu_ref, wd_ref, post_ref, o_ref, *rest, final):
    h_ref = rest[0]
    j = pl.program_id(1)
    last = pl.num_programs(1) - 1
    n_chunks = x_ref.shape[0] // ROW_CHUNK

    @pl.when(j == 0)
    def _():
        g = g_ref[...]

        def rows(r, carry):
            rs = pl.ds(pl.multiple_of(r * ROW_CHUNK, ROW_CHUNK), ROW_CHUNK)
            x = x_ref[rs, :]
            h_ref[rs, :] = _rms(x, g).astype(BF16)
            o_ref[rs, :] = x
            return carry

        lax.fori_loop(0, n_chunks, rows, 0, unroll=2)

    def accumulate():
        h = h_ref[...]
        a = _dot(h, wg_ref[...].astype(BF16))
        b = _dot(h, wu_ref[...].astype(BF16))
        hid = (a * jax.nn.sigmoid(a)) * b
        o_ref[...] += _dot(hid.astype(BF16), (0.5 * wd_ref[...]).astype(BF16))

    pl.when(j < last)(accumulate)

    @pl.when(j == last)
    def _():
        accumulate()
        y = _rms(o_ref[...], post_ref[...])
        if final:
            o_ref[...] = y
        else:
            rest[0][...] = y.astype(BF16)


def _next_tile_after(i, j, first_step, n_tiles):
    return jnp.minimum(i + jnp.where(j >= first_step, 1, 0), n_tiles - 1)


def _ffn(x, g, wg, wu, wd, post, *, final, tm=1024, tf=256):
    n, d = x.shape
    dff = wg.shape[1]
    nt = n // tm
    row_tile = pl.BlockSpec((tm, d), lambda i, j: (i, 0))
    if final:
        out_specs, out_shape = row_tile, jax.ShapeDtypeStruct((n, d), F32)
    else:
        out_specs = [row_tile, row_tile]
        out_shape = [jax.ShapeDtypeStruct((n, d), F32), jax.ShapeDtypeStruct((n, d), BF16)]
    return pl.pallas_call(
        functools.partial(_ffn_kernel, final=final),
        grid=(nt, dff // tf),
        in_specs=[
            row_tile,
            pl.BlockSpec((1, d), lambda i, j: (0, 0)),
            pl.BlockSpec((d, tf), lambda i, j: (0, j)),
            pl.BlockSpec((d, tf), lambda i, j: (0, j)),
            pl.BlockSpec((tf, d), lambda i, j: (j, 0)),
            pl.BlockSpec((1, d), lambda i, j: (0, 0)),
        ],
        out_specs=out_specs,
        out_shape=out_shape,
        scratch_shapes=[pltpu.VMEM((tm, d), BF16)] if final else [],
        compiler_params=pltpu.CompilerParams(
            dimension_semantics=("parallel", "arbitrary"), vmem_limit_bytes=VMEM_LIMIT),
        name="ffn_final" if final else "ffn",
    )(x, g, wg, wu, wd, post)


def _mix_in_kernel(h_ref, w_ref, lng_ref, lnb_ref, ws_ref, bs_ref, ua_ref, yb_ref, vn_ref, *, sw):
    j = pl.program_id(1)

    @pl.when(j == 0)
    def _():
        ua_ref[...] = _dot(h_ref[...], w_ref[...])

    @pl.when(j == 1)
    def _():
        v = jax.nn.gelu(_dot(h_ref[...], w_ref[...]))
        mu = jnp.mean(v, axis=-1, keepdims=True)
        vc = v - mu
        var = jnp.mean(vc * vc, axis=-1, keepdims=True)
        vn_ref[...] = ((vc * lax.rsqrt(var + NORM_EPS)) * lng_ref[...] + lnb_ref[...]).astype(BF16)

    @pl.when(j == 2)
    def _():
        tm = vn_ref.shape[0]
        u = jax.nn.gelu(_dot(h_ref[...], w_ref[...]))
        r = lax.broadcasted_iota(jnp.int32, (SGU_CHUNK, SGU_CHUNK), 0) // SGU_CAUSAL_BLOCK
        c = lax.broadcasted_iota(jnp.int32, (SGU_CHUNK, SGU_CHUNK), 1) // SGU_CAUSAL_BLOCK
        keep = r >= c
        hd = sw // SGU_HEADS
        for h in range(SGU_HEADS):
            wsm = jnp.where(keep, ws_ref[h], 0.0).astype(BF16)
            cols = slice(h * hd, (h + 1) * hd)
            for q in range(tm // SGU_CHUNK):
                rows = slice(q * SGU_CHUNK, (q + 1) * SGU_CHUNK)
                mixed = _dot(wsm, vn_ref[rows, cols]) + bs_ref[:, cols]
                yb_ref[rows, cols] = (u[rows, cols] * mixed).astype(BF16)


def _mix_in(h, w_in, ln_g, ln_b, w_s, bsb, *, sw, tm=1024):
    n, d = h.shape
    assert w_in.shape[1] == 3 * sw
    nt = n // tm
    return pl.pallas_call(
        functools.partial(_mix_in_kernel, sw=sw),
        grid=(nt, 3),
        in_specs=[
            pl.BlockSpec((tm, d), lambda i, j: (i, 0)),
            pl.BlockSpec((d, sw), lambda i, j: (0, jnp.where(j == 0, 0, 3 - j))),
            pl.BlockSpec((1, sw), lambda i, j: (0, 0)),
            pl.BlockSpec((1, sw), lambda i, j: (0, 0)),
            pl.BlockSpec(w_s.shape, lambda i, j: (0, 0, 0)),
            pl.BlockSpec(bsb.shape, lambda i, j: (0, 0)),
        ],
        out_specs=[
            pl.BlockSpec((tm, sw), lambda i, j: (i, 0)),
            pl.BlockSpec((tm, sw), lambda i, j: (i, 0)),
        ],
        out_shape=[
            jax.ShapeDtypeStruct((n, sw), F32),
            jax.ShapeDtypeStruct((n, sw), BF16),
        ],
        scratch_shapes=[pltpu.VMEM((tm, sw), BF16)],
        compiler_params=pltpu.CompilerParams(
            dimension_semantics=("parallel", "arbitrary"), vmem_limit_bytes=VMEM_LIMIT),
        name="mix_in",
    )(h, w_in, ln_g, ln_b, w_s, bsb)


def _dot_split(a, b):
    a_hi = a.astype(BF16)
    b_hi = b.astype(BF16)
    a_lo = (a - a_hi.astype(F32)).astype(BF16)
    b_lo = (b - b_hi.astype(F32)).astype(BF16)
    return _dot(a_hi, b_hi) + (_dot(a_hi, b_lo) + _dot(a_lo, b_hi))


def _expand(x, sel):
    x1 = x.astype(BF16)
    r1 = x - x1.astype(F32)
    x2 = r1.astype(BF16)
    x3 = (r1 - x2.astype(F32)).astype(BF16)
    return _dot(x1, sel) + (_dot(x2, sel) + _dot(x3, sel))


def _expand_t(x, sel):
    tn = (((0,), (0,)), ((), ()))
    x1 = x.astype(BF16)
    r1 = x - x1.astype(F32)
    x2 = r1.astype(BF16)
    x3 = (r1 - x2.astype(F32)).astype(BF16)
    dot_t = lambda v: lax.dot_general(v, sel, tn, preferred_element_type=F32)
    return dot_t(x1) + (dot_t(x2) + dot_t(x3))


def _s5_kernel(u_ref, tab_ref, c2_ref, cb_ref, bre_ref, bim_ref, rep_ref, til_ref, a_ref, d_ref, o_ref,
               dn_ref, upr_ref, upi_ref, wt_ref, wb_ref, wc_ref,
               up_ref, ut_ref, vt_ref, v_ref, xs_ref, xt_ref, yt_ref, yn_ref):
    nc = u_ref.shape[0] // S5_CHUNK
    gw = S5_GROUP_WIDTH
    gb = GROUPS_PER_BLOCK
    tw = S5_CHUNK * gw
    ns = gb * S5_STATE
    sl = 2 * S5_STATE

    tab = tab_ref[...]
    ar2, an, ap = tab[:, 0:sl], tab[:, sl:2 * sl], tab[:, 2 * sl:3 * sl]
    q, qs = tab[:, 3 * sl:4 * sl], tab[:, 4 * sl:5 * sl]
    lo = lax.broadcasted_iota(jnp.int32, (gb, sl), 1) < S5_STATE
    x = jnp.where(lo, 1.0, 0.0)
    xs = jnp.where(lo, 0.0, 1.0)
    dn_ref[pl.ds((S5_CHUNK - 1) * gb, gb), :] = q
    for j in range(1, S5_CHUNK + 1):
        x, xs = x * ar2 + xs * an, xs * ar2 + x * ap
        upr_ref[pl.ds((j - 1) * gb, gb), :] = jnp.where(lo, x, xs)
        upi_ref[pl.ds((j - 1) * gb, gb), :] = jnp.where(lo, xs, x)
        if j < S5_CHUNK:
            q, qs = q * ar2 + qs * an, qs * ar2 + q * ap
            dn_ref[pl.ds((S5_CHUNK - 1 - j) * gb, gb), :] = q

    lane = lax.broadcasted_iota(jnp.int32, (gw, tw), 1)
    for g in range(gb):
        pw = _expand_t(dn_ref[pl.ds(g, S5_CHUNK, stride=gb), :], rep_ref[...])
        br = _expand(bre_ref[g], til_ref[...])
        bi = _expand(bim_ref[g], til_ref[...])
        pr, pi = pw[:S5_STATE], pw[S5_STATE:]
        wbf = jnp.concatenate([pr * br - pi * bi, pr * bi + pi * br], axis=0)
        wb_ref[g] = wbf.astype(BF16)
        hrow = _dot_split(c2_ref[g], wbf)
        ca, cb = c2_ref[g], cb_ref[g]
        for t in range(S5_CHUNK):
            shift = (gw * (t + 1)) % tw
            rolled = pltpu.roll(hrow, shift, axis=1) if shift else hrow
            wt_ref[g, t * gw:(t + 1) * gw, :] = jnp.where(lane < gw * (t + 1), rolled, 0.0).astype(BF16)
            row = t * gb + g
            wc_ref[g, t * gw:(t + 1) * gw, :] = (
                ca * upr_ref[row:row + 1, :] + cb * upi_ref[row:row + 1, :]).astype(BF16)

    for s in range(S5_CHUNK):
        piece = u_ref[pl.ds(s, nc, stride=S5_CHUNK), :]
        up_ref[s] = piece
        ut_ref[s] = piece.T.astype(BF16)

    def chunk_inputs(g):
        return jnp.concatenate(
            [ut_ref[s, g * gw:(g + 1) * gw, :] for s in range(S5_CHUNK)], axis=0)

    for g in range(GROUPS_PER_BLOCK):
        vg = _dot(wb_ref[g], chunk_inputs(g))
        vt_ref[g * S5_STATE:(g + 1) * S5_STATE, :] = vg[:S5_STATE]
        vt_ref[ns + g * S5_STATE:ns + (g + 1) * S5_STATE, :] = vg[S5_STATE:]
    v_ref[...] = vt_ref[...].T

    ar = a_ref[0:1, :]
    ai = a_ref[1:2, :]

    def step(c, carry):
        xr, xi = carry
        xs_ref[pl.ds(c, 1), 0:ns] = xr
        xs_ref[pl.ds(c, 1), ns:2 * ns] = xi
        vr = v_ref[pl.ds(c, 1), 0:ns]
        vi = v_ref[pl.ds(c, 1), ns:2 * ns]
        return ar * xr - ai * xi + vr, ar * xi + ai * xr + vi

    zero = jnp.zeros((1, ns), F32)
    lax.fori_loop(0, nc, step, (zero, zero), unroll=16)
    xt_ref[...] = xs_ref[...].T.astype(BF16)

    for g in range(GROUPS_PER_BLOCK):
        xg = jnp.concatenate(
            [xt_ref[g * S5_STATE:(g + 1) * S5_STATE, :],
             xt_ref[ns + g * S5_STATE:ns + (g + 1) * S5_STATE, :]], axis=0)
        yg = _dot(wt_ref[g], chunk_inputs(g)) + _dot(wc_ref[g], xg)
        for t in range(S5_CHUNK):
            yt_ref[t, g * gw:(g + 1) * gw, :] = yg[t * gw:(t + 1) * gw, :]

    for t in range(S5_CHUNK):
        y = yt_ref[t].T + d_ref[...] * up_ref[t]
        yn_ref[pl.ds(t, nc, stride=S5_CHUNK), :] = jax.nn.gelu(y)
    o_ref[...] = yn_ref[...].astype(BF16)


def _s5_core(u, tab, c2, cb, b_re, b_im, rep, til, a16, dsk):
    n, sw = u.shape
    nb = sw // LANE
    nc = n // S5_CHUNK
    gb = GROUPS_PER_BLOCK
    ns2 = 2 * gb * S5_STATE
    tw = S5_CHUNK * S5_GROUP_WIDTH
    per_group = lambda a: pl.BlockSpec((gb,) + a.shape[1:], lambda j: (j,) + (0,) * (a.ndim - 1))
    whole = lambda a: pl.BlockSpec(a.shape, lambda j: (0, 0))
    table_rows = pltpu.VMEM((S5_CHUNK * gb, 2 * S5_STATE), F32)
    return pl.pallas_call(
        _s5_kernel,
        grid=(nb,),
        in_specs=[
            pl.BlockSpec((n, LANE), lambda j: (0, j)),
            per_group(tab), per_group(c2), per_group(cb), per_group(b_re), per_group(b_im),
            whole(rep), whole(til),
            pl.BlockSpec((None, 2, ns2 // 2), lambda j: (j, 0, 0)),
            pl.BlockSpec((1, LANE), lambda j: (0, j)),
        ],
        out_specs=pl.BlockSpec((n, LANE), lambda j: (0, j)),
        out_shape=jax.ShapeDtypeStruct((n, sw), BF16),
        scratch_shapes=[
            table_rows, table_rows, table_rows,
            pltpu.VMEM((gb, tw, tw), BF16),
            pltpu.VMEM((gb, 2 * S5_STATE, tw), BF16),
            pltpu.VMEM((gb, tw, 2 * S5_STATE), BF16),
            pltpu.VMEM((S5_CHUNK, nc, LANE), F32),
            pltpu.VMEM((S5_CHUNK, LANE, nc), BF16),
            pltpu.VMEM((ns2, nc), F32),
            pltpu.VMEM((nc, ns2), F32),
            pltpu.VMEM((nc, ns2), F32),
            pltpu.VMEM((ns2, nc), BF16),
            pltpu.VMEM((S5_CHUNK, LANE, nc), F32),
            pltpu.VMEM((n, LANE), F32),
        ],
        compiler_params=pltpu.CompilerParams(
            dimension_semantics=("parallel",), vmem_limit_bytes=VMEM_LIMIT),
        name="s5_core",
    )(u, tab, c2, cb, b_re, b_im, rep, til, a16, dsk)


def _s5_weights(a_re, a_im, log_dt, b_re, b_im, c_re, c_im, d_skip):
    g, p = a_re.shape
    dt = jnp.exp(log_dt)[:, None]
    decay = jnp.exp(a_re * dt)
    ab_re = decay * jnp.cos(a_im * dt)
    ab_im = decay * jnp.sin(a_im * dt)
    denom = a_re * a_re + a_im * a_im
    num_re = ab_re - 1.0
    num_im = ab_im
    k_re = (num_re * a_re + num_im * a_im) / denom
    k_im = (num_im * a_re - num_re * a_im) / denom
    tab = jnp.concatenate([ab_re, ab_re, -ab_im, ab_im, ab_im, -ab_im, k_re, k_im, k_im, k_re], axis=1)
    r, i = ab_re, ab_im
    for _ in range(S5_CHUNK.bit_length() - 1):
        r, i = r * r - i * i, 2.0 * (r * i)
    gb = GROUPS_PER_BLOCK
    a16 = jnp.stack([r.reshape(g // gb, gb * p), i.reshape(g // gb, gb * p)], axis=1)
    c2 = jnp.concatenate([c_re, -c_im], axis=-1)
    cb = jnp.concatenate([-c_im, -c_re], axis=-1)
    eye = jnp.eye(S5_GROUP_WIDTH, dtype=BF16)
    rep = jnp.repeat(eye, S5_GROUP_WIDTH, axis=1)
    til = jnp.tile(eye, (1, S5_CHUNK))
    return (tab, c2, cb, b_re, b_im, rep, til, a16, d_skip.reshape(1, -1))


def _merge_kernel(x_ref, h_ref, yap_ref, yb_ref, wglu_ref, bglu_ref, wga_ref, wgb_ref,
                  bga_ref, bgb_ref, wa_ref, wb_ref, *rest, n1, n2):
    wo_refs = rest[:n2]
    o_ref, ya_ref, m_ref = rest[n2:]
    j = pl.program_id(1)

    def gated_tile(with_glu):
        if with_glu:
            yp = yap_ref[...]
            z = _dot(yp, wglu_ref[...]) + bglu_ref[...]
            ya_ref[...] = (yp.astype(F32) * jax.nn.sigmoid(z)).astype(BF16)
        h = h_ref[...]
        ga = jax.nn.sigmoid(_dot(h, wga_ref[...].astype(BF16)) + bga_ref[...])
        gb = jax.nn.sigmoid(_dot(h, wgb_ref[...].astype(BF16)) + bgb_ref[...])
        pa = _dot(ya_ref[...], wa_ref[...].astype(BF16))
        pb = _dot(yb_ref[...], wb_ref[...].astype(BF16))
        m_ref[j] = (ga * pa + gb * pb).astype(BF16)

    pl.when(j == 0)(functools.partial(gated_tile, True))
    pl.when(jnp.logical_and(j > 0, j < n1))(functools.partial(gated_tile, False))

    for k in range(n2):
        @pl.when(j == n1 + k)
        def _(k=k):
            m = jnp.concatenate([m_ref[q] for q in range(n1)], axis=1)
            o_ref[...] = x_ref[...] + _dot(m, wo_refs[k][...])


def _merge(x, h, yap, yb, wglu, bglu, wgate, bgate, wa, wb, wo, *, tm=1024, t1=256, t2=512):
    n, d = x.shape
    sw = yap.shape[1]
    n1, n2 = d // t1, d // t2
    nt = n // tm
    second = lambda j: jnp.maximum(j - n1, 0)

    def first(j, back):
        return jnp.where(j < n1, j, jnp.where(j < n1 + back, n1 - 1, 0))

    once = pl.Buffered(1)
    return pl.pallas_call(
        functools.partial(_merge_kernel, n1=n1, n2=n2),
        grid=(nt, n1 + n2),
        in_specs=[
            pl.BlockSpec((tm, t2), lambda i, j: (i, second(j))),
            pl.BlockSpec((tm, d), lambda i, j: (_next_tile_after(i, j, n1, nt), 0)),
            pl.BlockSpec((tm, sw), lambda i, j: (_next_tile_after(i, j, 1, nt), 0)),
            pl.BlockSpec((tm, sw), lambda i, j: (_next_tile_after(i, j, n1, nt), 0)),
            pl.BlockSpec((sw, sw), lambda i, j: (0, 0), pipeline_mode=once),
            pl.BlockSpec((1, sw), lambda i, j: (0, 0)),
            pl.BlockSpec((d, t1), lambda i, j: (0, first(j, 1))),
            pl.BlockSpec((d, t1), lambda i, j: (0, first(j, 2) + n1)),
            pl.BlockSpec((1, t1), lambda i, j: (0, first(j, 1))),
            pl.BlockSpec((1, t1), lambda i, j: (0, first(j, 2) + n1)),
            pl.BlockSpec((sw, t1), lambda i, j: (0, first(j, 3))),
            pl.BlockSpec((sw, t1), lambda i, j: (0, first(j, 3))),
        ] + [pl.BlockSpec((d, t2), lambda i, j, k=k: (0, k), pipeline_mode=once) for k in range(n2)],
        out_specs=pl.BlockSpec((tm, t2), lambda i, j: (i, second(j))),
        out_shape=jax.ShapeDtypeStruct((n, d), F32),
        scratch_shapes=[pltpu.VMEM((tm, sw), BF16), pltpu.VMEM((n1, tm, t1), BF16)],
        compiler_params=pltpu.CompilerParams(
            dimension_semantics=("parallel", "arbitrary"), vmem_limit_bytes=VMEM_LIMIT),
        name="merge",
    )(x, h, yap, yb, wglu, bglu, wgate, wgate, bgate, bgate, wa, wb, *([wo] * n2))


def kernel(x, ffn1_norm, ffn1_w_gate, ffn1_w_up, ffn1_w_down, mix_norm, w_in, s5_a_re, s5_a_im, s5_log_dt, s5_b_re, s5_b_im, s5_c_re, s5_c_im, s5_d, s5_w_glu, s5_b_glu, sgu_ln_g, sgu_ln_b, sgu_w_s, sgu_b_s, w_branch_a, w_branch_b, w_gate, b_gate, w_out, ffn2_norm, ffn2_w_gate, ffn2_w_up, ffn2_w_down, final_norm):
    bsz, seq, d = x.shape
    depth = ffn1_norm.shape[0]
    sw = s5_w_glu.shape[1]
    assert bsz == 1 and sw == w_branch_b.shape[1] and sw % LANE == 0
    assert seq % (S5_CHUNK * LANE) == 0 and sgu_w_s.shape[2] == SGU_CHUNK
    bf = lambda w: w.astype(BF16)
    row = lambda v: v.reshape(1, -1)
    xs = x.reshape(seq, d)
    fin = row(final_norm)
    for i in range(depth):
        xs, h = _ffn(xs, row(ffn1_norm[i]), ffn1_w_gate[i], ffn1_w_up[i], ffn1_w_down[i],
                     row(mix_norm[i]), final=False)
        bsb = jnp.repeat(jnp.transpose(sgu_b_s[i]), sw // SGU_HEADS, axis=1)
        ua, yb = _mix_in(h, bf(w_in[i]), row(sgu_ln_g[i]), row(sgu_ln_b[i]), sgu_w_s[i], bsb, sw=sw)
        yap = _s5_core(ua, *_s5_weights(s5_a_re[i], s5_a_im[i], s5_log_dt[i], s5_b_re[i], s5_b_im[i],
                                        s5_c_re[i], s5_c_im[i], s5_d[i]))
        xs = _merge(xs, h, yap, yb, bf(s5_w_glu[i]), row(s5_b_glu[i]), w_gate[i], row(b_gate[i]),
                    w_branch_a[i], w_branch_b[i], bf(w_out[i]))
        if i == depth - 1:
            xs = _ffn(xs, row(ffn2_norm[i]), ffn2_w_gate[i], ffn2_w_up[i], ffn2_w_down[i], fin, final=True)
        else:
            xs, _ = _ffn(xs, row(ffn2_norm[i]), ffn2_w_gate[i], ffn2_w_up[i], ffn2_w_down[i], fin,
                         final=False)
    return xs.reshape(bsz, seq, d)
```

```python
import functools

import jax
import jax.numpy as jnp
from jax import lax
from jax.experimental import pallas as pl
from jax.experimental.pallas import tpu as pltpu

F32 = jnp.float32
BF16 = jnp.bfloat16

NORM_EPS = 1e-6
S5_GROUP_WIDTH = 16
S5_STATE = 64
SGU_HEADS = 8
SGU_CHUNK = 128
SGU_CAUSAL_BLOCK = 64
S5_CHUNK = 16
LANE = 128
GROUPS_PER_BLOCK = LANE // S5_GROUP_WIDTH
VMEM_LIMIT = 60 * 1024 * 1024
X_HALF_SWITCH_STEPS = (8, 14)


def _rms(x, g):
    ms = jnp.mean(x * x, axis=-1, keepdims=True)
    return (x * lax.rsqrt(ms + NORM_EPS)) * g


def _dot(a, b):
    return jnp.dot(a, b, preferred_element_type=F32)


def _ffn_kernel(xa_ref, xb_ref, g_ref, wg_ref, wu_ref, wd_ref, post_ref, o_ref, *rest, final):
    h_ref = rest[0]
    j = pl.program_id(1)
    last = pl.num_programs(1) - 1
    half = xa_ref.shape[0]

    @pl.when(j == 0)
    def _():
        g = g_ref[...]
        for k, x_ref in enumerate((xa_ref, xb_ref)):
            rs = slice(k * half, (k + 1) * half)
            x = x_ref[...]
            h_ref[rs, :] = _rms(x, g).astype(BF16)
            o_ref[rs, :] = x

    def accumulate():
        h = h_ref[...]
        a = _dot(h, wg_ref[...].astype(BF16))
        b = _dot(h, wu_ref[...].astype(BF16))
        hid = (a * jax.nn.sigmoid(a)) * b
        o_ref[...] += _dot(hid.astype(BF16), (0.5 * wd_ref[...]).astype(BF16))

    pl.when(j < last)(accumulate)

    @pl.when(j == last)
    def _():
        accumulate()
        y = _rms(o_ref[...], post_ref[...])
        if final:
            o_ref[...] = y
        else:
            rest[0][...] = y.astype(BF16)


def _next_tile_after(i, j, first_step, n_tiles):
    return jnp.minimum(i + jnp.where(j >= first_step, 1, 0), n_tiles - 1)


def _ffn(x, g, wg, wu, wd, post, *, final, tm=1024, tf=256):
    n, d = x.shape
    dff = wg.shape[1]
    nt = n // tm
    row_tile = pl.BlockSpec((tm, d), lambda i, j: (i, 0))
    if final:
        out_specs, out_shape = row_tile, jax.ShapeDtypeStruct((n, d), F32)
    else:
        out_specs = [row_tile, row_tile]
        out_shape = [jax.ShapeDtypeStruct((n, d), F32), jax.ShapeDtypeStruct((n, d), BF16)]
    sa, sb = X_HALF_SWITCH_STEPS
    return pl.pallas_call(
        functools.partial(_ffn_kernel, final=final),
        grid=(nt, dff // tf),
        in_specs=[
            pl.BlockSpec((tm // 2, d), lambda i, j: (2 * _next_tile_after(i, j, sa, nt), 0)),
            pl.BlockSpec((tm // 2, d), lambda i, j: (2 * _next_tile_after(i, j, sb, nt) + 1, 0)),
            pl.BlockSpec((1, d), lambda i, j: (0, 0)),
            pl.BlockSpec((d, tf), lambda i, j: (0, j)),
            pl.BlockSpec((d, tf), lambda i, j: (0, j)),
            pl.BlockSpec((tf, d), lambda i, j: (j, 0)),
            pl.BlockSpec((1, d), lambda i, j: (0, 0)),
        ],
        out_specs=out_specs,
        out_shape=out_shape,
        scratch_shapes=[pltpu.VMEM((tm, d), BF16)] if final else [],
        compiler_params=pltpu.CompilerParams(
            dimension_semantics=("parallel", "arbitrary"), vmem_limit_bytes=VMEM_LIMIT),
        name="ffn_final" if final else "ffn",
    )(x, x, g, wg, wu, wd, post)


def _mix_in_kernel(h_ref, w_ref, lng_ref, lnb_ref, ws_ref, bs_ref, ua_ref, yb_ref, vn_ref, *, sw):
    j = pl.program_id(1)

    @pl.when(j == 0)
    def _():
        ua_ref[...] = _dot(h_ref[...], w_ref[...])

    @pl.when(j == 1)
    def _():
        v = jax.nn.gelu(_dot(h_ref[...], w_ref[...]))
        mu = jnp.mean(v, axis=-1, keepdims=True)
        vc = v - mu
        var = jnp.mean(vc * vc, axis=-1, keepdims=True)
        vn_ref[...] = ((vc * lax.rsqrt(var + NORM_EPS)) * lng_ref[...] + lnb_ref[...]).astype(BF16)

    @pl.when(j == 2)
    def _():
        tm = vn_ref.shape[0]
        u = jax.nn.gelu(_dot(h_ref[...], w_ref[...]))
        r = lax.broadcasted_iota(jnp.int32, (SGU_CHUNK, SGU_CHUNK), 0) // SGU_CAUSAL_BLOCK
        c = lax.broadcasted_iota(jnp.int32, (SGU_CHUNK, SGU_CHUNK), 1) // SGU_CAUSAL_BLOCK
        keep = r >= c
        hd = sw // SGU_HEADS
        for h in range(SGU_HEADS):
            wsm = jnp.where(keep, ws_ref[h], 0.0).astype(BF16)
            cols = slice(h * hd, (h + 1) * hd)
            for q in range(tm // SGU_CHUNK):
                rows = slice(q * SGU_CHUNK, (q + 1) * SGU_CHUNK)
                mixed = _dot(wsm, vn_ref[rows, cols]) + bs_ref[:, cols]
                yb_ref[rows, cols] = (u[rows, cols] * mixed).astype(BF16)


def _mix_in(h, w_in, ln_g, ln_b, w_s, bsb, *, sw, tm=1024):
    n, d = h.shape
    assert w_in.shape[1] == 3 * sw
    nt = n // tm
    return pl.pallas_call(
        functools.partial(_mix_in_kernel, sw=sw),
        grid=(nt, 3),
        in_specs=[
            pl.BlockSpec((tm, d), lambda i, j: (i, 0)),
            pl.BlockSpec((d, sw), lambda i, j: (0, jnp.where(j == 0, 0, 3 - j))),
            pl.BlockSpec((1, sw), lambda i, j: (0, 0)),
            pl.BlockSpec((1, sw), lambda i, j: (0, 0)),
            pl.BlockSpec(w_s.shape, lambda i, j: (0, 0, 0)),
            pl.BlockSpec(bsb.shape, lambda i, j: (0, 0)),
        ],
        out_specs=[
            pl.BlockSpec((tm, sw), lambda i, j: (i, 0)),
            pl.BlockSpec((tm, sw), lambda i, j: (i, 0)),
        ],
        out_shape=[
            jax.ShapeDtypeStruct((n, sw), F32),
            jax.ShapeDtypeStruct((n, sw), BF16),
        ],
        scratch_shapes=[pltpu.VMEM((tm, sw), BF16)],
        compiler_params=pltpu.CompilerParams(
            dimension_semantics=("parallel", "arbitrary"), vmem_limit_bytes=VMEM_LIMIT),
        name="mix_in",
    )(h, w_in, ln_g, ln_b, w_s, bsb)


def _dot_split(a, b):
    a_hi = a.astype(BF16)
    b_hi = b.astype(BF16)
    a_lo = (a - a_hi.astype(F32)).astype(BF16)
    b_lo = (b - b_hi.astype(F32)).astype(BF16)
    return _dot(a_hi, b_hi) + (_dot(a_hi, b_lo) + _dot(a_lo, b_hi))


def _expand(x, sel):
    x1 = x.astype(BF16)
    r1 = x - x1.astype(F32)
    x2 = r1.astype(BF16)
    x3 = (r1 - x2.astype(F32)).astype(BF16)
    return _dot(x1, sel) + (_dot(x2, sel) + _dot(x3, sel))


def _expand_t(x, sel):
    tn = (((0,), (0,)), ((), ()))
    x1 = x.astype(BF16)
    r1 = x - x1.astype(F32)
    x2 = r1.astype(BF16)
    x3 = (r1 - x2.astype(F32)).astype(BF16)
    dot_t = lambda v: lax.dot_general(v, sel, tn, preferred_element_type=F32)
    return dot_t(x1) + (dot_t(x2) + dot_t(x3))


def _s5_kernel(u_ref, tab_ref, c2_ref, cb_ref, bre_ref, bim_ref, rep_ref, til_ref, a_ref, d_ref, o_ref,
               dn_ref, upr_ref, upi_ref, wt_ref, wb_ref, wc_ref,
               up_ref, ut_ref, vt_ref, v_ref, xs_ref, xt_ref, yt_ref, yn_ref):
    nc = u_ref.shape[0] // S5_CHUNK
    gw = S5_GROUP_WIDTH
    gb = GROUPS_PER_BLOCK
    tw = S5_CHUNK * gw
    ns = gb * S5_STATE
    sl = 2 * S5_STATE

    tab = tab_ref[...]
    ar2, an, ap = tab[:, 0:sl], tab[:, sl:2 * sl], tab[:, 2 * sl:3 * sl]
    q, qs = tab[:, 3 * sl:4 * sl], tab[:, 4 * sl:5 * sl]
    lo = lax.broadcasted_iota(jnp.int32, (gb, sl), 1) < S5_STATE
    x = jnp.where(lo, 1.0, 0.0)
    xs = jnp.where(lo, 0.0, 1.0)
    dn_ref[pl.ds((S5_CHUNK - 1) * gb, gb), :] = q
    for j in range(1, S5_CHUNK + 1):
        x, xs = x * ar2 + xs * an, xs * ar2 + x * ap
        upr_ref[pl.ds((j - 1) * gb, gb), :] = jnp.where(lo, x, xs)
        upi_ref[pl.ds((j - 1) * gb, gb), :] = jnp.where(lo, xs, x)
        if j < S5_CHUNK:
            q, qs = q * ar2 + qs * an, qs * ar2 + q * ap
            dn_ref[pl.ds((S5_CHUNK - 1 - j) * gb, gb), :] = q

    lane = lax.broadcasted_iota(jnp.int32, (gw, tw), 1)
    for g in range(gb):
        pw = _expand_t(dn_ref[pl.ds(g, S5_CHUNK, stride=gb), :], rep_ref[...])
        br = _expand(bre_ref[g], til_ref[...])
        bi = _expand(bim_ref[g], til_ref[...])
        pr, pi = pw[:S5_STATE], pw[S5_STATE:]
        wbf = jnp.concatenate([pr * br - pi * bi, pr * bi + pi * br], axis=0)
        wb_ref[g] = wbf.astype(BF16)
        hrow = _dot_split(c2_ref[g], wbf)
        ca, cb = c2_ref[g], cb_ref[g]
        for t in range(S5_CHUNK):
            shift = (gw * (t + 1)) % tw
            rolled = pltpu.roll(hrow, shift, axis=1) if shift else hrow
            wt_ref[g, t * gw:(t + 1) * gw, :] = jnp.where(lane < gw * (t + 1), rolled, 0.0).astype(BF16)
            row = t * gb + g
            wc_ref[g, t * gw:(t + 1) * gw, :] = (
                ca * upr_ref[row:row + 1, :] + cb * upi_ref[row:row + 1, :]).astype(BF16)

    for s in range(S5_CHUNK):
        piece = u_ref[pl.ds(s, nc, stride=S5_CHUNK), :]
        up_ref[s] = piece
        ut_ref[s] = piece.T.astype(BF16)

    def chunk_inputs(g):
        return jnp.concatenate(
            [ut_ref[s, g * gw:(g + 1) * gw, :] for s in range(S5_CHUNK)], axis=0)

    for g in range(GROUPS_PER_BLOCK):
        vg = _dot(wb_ref[g], chunk_inputs(g))
        vt_ref[g * S5_STATE:(g + 1) * S5_STATE, :] = vg[:S5_STATE]
        vt_ref[ns + g * S5_STATE:ns + (g + 1) * S5_STATE, :] = vg[S5_STATE:]
    v_ref[...] = vt_ref[...].T

    ar = a_ref[0:1, :]
    ai = a_ref[1:2, :]

    def step(c, carry):
        xr, xi = carry
        xs_ref[pl.ds(c, 1), 0:ns] = xr
        xs_ref[pl.ds(c, 1), ns:2 * ns] = xi
        vr = v_ref[pl.ds(c, 1), 0:ns]
        vi = v_ref[pl.ds(c, 1), ns:2 * ns]
        return ar * xr - ai * xi + vr, ar * xi + ai * xr + vi

    zero = jnp.zeros((1, ns), F32)
    lax.fori_loop(0, nc, step, (zero, zero), unroll=16)
    xt_ref[...] = xs_ref[...].T.astype(BF16)

    for g in range(GROUPS_PER_BLOCK):
        xg = jnp.concatenate(
            [xt_ref[g * S5_STATE:(g + 1) * S5_STATE, :],
             xt_ref[ns + g * S5_STATE:ns + (g + 1) * S5_STATE, :]], axis=0)
        yg = _dot(wt_ref[g], chunk_inputs(g)) + _dot(wc_ref[g], xg)
        for t in range(S5_CHUNK):
            yt_ref[t, g * gw:(g + 1) * gw, :] = yg[t * gw:(t + 1) * gw, :]

    for t in range(S5_CHUNK):
        y = yt_ref[t].T + d_ref[...] * up_ref[t]
        yn_ref[pl.ds(t, nc, stride=S5_CHUNK), :] = jax.nn.gelu(y)
    o_ref[...] = yn_ref[...].astype(BF16)


def _s5_core(u, tab, c2, cb, b_re, b_im, rep, til, a16, dsk):
    n, sw = u.shape
    nb = sw // LANE
    nc = n // S5_CHUNK
    gb = GROUPS_PER_BLOCK
    ns2 = 2 * gb * S5_STATE
    tw = S5_CHUNK * S5_GROUP_WIDTH
    per_group = lambda a: pl.BlockSpec((gb,) + a.shape[1:], lambda j: (j,) + (0,) * (a.ndim - 1))
    whole = lambda a: pl.BlockSpec(a.shape, lambda j: (0, 0))
    table_rows = pltpu.VMEM((S5_CHUNK * gb, 2 * S5_STATE), F32)
    return pl.pallas_call(
        _s5_kernel,
        grid=(nb,),
        in_specs=[
            pl.BlockSpec((n, LANE), lambda j: (0, j)),
            per_group(tab), per_group(c2), per_group(cb), per_group(b_re), per_group(b_im),
            whole(rep), whole(til),
            pl.BlockSpec((None, 2, ns2 // 2), lambda j: (j, 0, 0)),
            pl.BlockSpec((1, LANE), lambda j: (0, j)),
        ],
        out_specs=pl.BlockSpec((n, LANE), lambda j: (0, j)),
        out_shape=jax.ShapeDtypeStruct((n, sw), BF16),
        scratch_shapes=[
            table_rows, table_rows, table_rows,
            pltpu.VMEM((gb, tw, tw), BF16),
            pltpu.VMEM((gb, 2 * S5_STATE, tw), BF16),
            pltpu.VMEM((gb, tw, 2 * S5_STATE), BF16),
            pltpu.VMEM((S5_CHUNK, nc, LANE), F32),
            pltpu.VMEM((S5_CHUNK, LANE, nc), BF16),
            pltpu.VMEM((ns2, nc), F32),
            pltpu.VMEM((nc, ns2), F32),
            pltpu.VMEM((nc, ns2), F32),
            pltpu.VMEM((ns2, nc), BF16),
            pltpu.VMEM((S5_CHUNK, LANE, nc), F32),
            pltpu.VMEM((n, LANE), F32),
        ],
        compiler_params=pltpu.CompilerParams(
            dimension_semantics=("parallel",), vmem_limit_bytes=VMEM_LIMIT),
        name="s5_core",
    )(u, tab, c2, cb, b_re, b_im, rep, til, a16, dsk)


def _s5_weights(a_re, a_im, log_dt, b_re, b_im, c_re, c_im, d_skip):
    g, p = a_re.shape
    dt = jnp.exp(log_dt)[:, None]
    decay = jnp.exp(a_re * dt)
    ab_re = decay * jnp.cos(a_im * dt)
    ab_im = decay * jnp.sin(a_im * dt)
    denom = a_re * a_re + a_im * a_im
    num_re = ab_re - 1.0
    num_im = ab_im
    k_re = (num_re * a_re + num_im * a_im) / denom
    k_im = (num_im * a_re - num_re * a_im) / denom
    tab = jnp.concatenate([ab_re, ab_re, -ab_im, ab_im, ab_im, -ab_im, k_re, k_im, k_im, k_re], axis=1)
    r, i = ab_re, ab_im
    for _ in range(S5_CHUNK.bit_length() - 1):
        r, i = r * r - i * i, 2.0 * (r * i)
    gb = GROUPS_PER_BLOCK
    a16 = jnp.stack([r.reshape(g // gb, gb * p), i.reshape(g // gb, gb * p)], axis=1)
    c2 = jnp.concatenate([c_re, -c_im], axis=-1)
    cb = jnp.concatenate([-c_im, -c_re], axis=-1)
    eye = jnp.eye(S5_GROUP_WIDTH, dtype=BF16)
    rep = jnp.repeat(eye, S5_GROUP_WIDTH, axis=1)
    til = jnp.tile(eye, (1, S5_CHUNK))
    return (tab, c2, cb, b_re, b_im, rep, til, a16, d_skip.reshape(1, -1))


def _merge_kernel(x_ref, h_ref, yap_ref, yb_ref, wglu_ref, bglu_ref, wga_ref, wgb_ref,
                  bga_ref, bgb_ref, wa_ref, wb_ref, *rest, n1, n2):
    wo_refs = rest[:n2]
    o_ref, ya_ref, m_ref = rest[n2:]
    j = pl.program_id(1)

    def gated_tile(with_glu):
        if with_glu:
            yp = yap_ref[...]
            z = _dot(yp, wglu_ref[...]) + bglu_ref[...]
            ya_ref[...] = (yp.astype(F32) * jax.nn.sigmoid(z)).astype(BF16)
        h = h_ref[...]
        ga = jax.nn.sigmoid(_dot(h, wga_ref[...].astype(BF16)) + bga_ref[...])
        gb = jax.nn.sigmoid(_dot(h, wgb_ref[...].astype(BF16)) + bgb_ref[...])
        pa = _dot(ya_ref[...], wa_ref[...].astype(BF16))
        pb = _dot(yb_ref[...], wb_ref[...].astype(BF16))
        m_ref[j] = (ga * pa + gb * pb).astype(BF16)

    pl.when(j == 0)(functools.partial(gated_tile, True))
    pl.when(jnp.logical_and(j > 0, j < n1))(functools.partial(gated_tile, False))

    for k in range(n2):
        @pl.when(j == n1 + k)
        def _(k=k):
            m = jnp.concatenate([m_ref[q] for q in range(n1)], axis=1)
            o_ref[...] = x_ref[...] + _dot(m, wo_refs[k][...])


def _merge(x, h, yap, yb, wglu, bglu, wgate, bgate, wa, wb, wo, *, tm=1024, t1=256, t2=512):
    n, d = x.shape
    sw = yap.shape[1]
    n1, n2 = d // t1, d // t2
    nt = n // tm
    second = lambda j: jnp.maximum(j - n1, 0)

    def first(j, back):
        return jnp.where(j < n1, j, jnp.where(j < n1 + back, n1 - 1, 0))

    once = pl.Buffered(1)
    return pl.pallas_call(
        functools.partial(_merge_kernel, n1=n1, n2=n2),
        grid=(nt, n1 + n2),
        in_specs=[
            pl.BlockSpec((tm, t2), lambda i, j: (i, second(j))),
            pl.BlockSpec((tm, d), lambda i, j: (_next_tile_after(i, j, n1, nt), 0)),
            pl.BlockSpec((tm, sw), lambda i, j: (_next_tile_after(i, j, 1, nt), 0)),
            pl.BlockSpec((tm, sw), lambda i, j: (_next_tile_after(i, j, n1, nt), 0)),
            pl.BlockSpec((sw, sw), lambda i, j: (0, 0), pipeline_mode=once),
            pl.BlockSpec((1, sw), lambda i, j: (0, 0)),
            pl.BlockSpec((d, t1), lambda i, j: (0, first(j, 1))),
            pl.BlockSpec((d, t1), lambda i, j: (0, first(j, 2) + n1)),
            pl.BlockSpec((1, t1), lambda i, j: (0, first(j, 1))),
            pl.BlockSpec((1, t1), lambda i, j: (0, first(j, 2) + n1)),
            pl.BlockSpec((sw, t1), lambda i, j: (0, first(j, 3))),
            pl.BlockSpec((sw, t1), lambda i, j: (0, first(j, 3))),
        ] + [pl.BlockSpec((d, t2), lambda i, j, k=k: (0, k), pipeline_mode=once) for k in range(n2)],
        out_specs=pl.BlockSpec((tm, t2), lambda i, j: (i, second(j))),
        out_shape=jax.ShapeDtypeStruct((n, d), F32),
        scratch_shapes=[pltpu.VMEM((tm, sw), BF16), pltpu.VMEM((n1, tm, t1), BF16)],
        compiler_params=pltpu.CompilerParams(
            dimension_semantics=("parallel", "arbitrary"), vmem_limit_bytes=VMEM_LIMIT),
        name="merge",
    )(x, h, yap, yb, wglu, bglu, wgate, wgate, bgate, bgate, wa, wb, *([wo] * n2))


def kernel(x, ffn1_norm, ffn1_w_gate, ffn1_w_up, ffn1_w_down, mix_norm, w_in, s5_a_re, s5_a_im, s5_log_dt, s5_b_re, s5_b_im, s5_c_re, s5_c_im, s5_d, s5_w_glu, s5_b_glu, sgu_ln_g, sgu_ln_b, sgu_w_s, sgu_b_s, w_branch_a, w_branch_b, w_gate, b_gate, w_out, ffn2_norm, ffn2_w_gate, ffn2_w_up, ffn2_w_down, final_norm):
    bsz, seq, d = x.shape
    depth = ffn1_norm.shape[0]
    sw = s5_w_glu.shape[1]
    assert bsz == 1 and sw == w_branch_b.shape[1] and sw % LANE == 0
    assert seq % (S5_CHUNK * LANE) == 0 and sgu_w_s.shape[2] == SGU_CHUNK
    bf = lambda w: w.astype(BF16)
    row = lambda v: v.reshape(1, -1)
    xs = x.reshape(seq, d)
    fin = row(final_norm)
    for i in range(depth):
        xs, h = _ffn(xs, row(ffn1_norm[i]), ffn1_w_gate[i], ffn1_w_up[i], ffn1_w_down[i],
                     row(mix_norm[i]), final=False)
        bsb = jnp.repeat(jnp.transpose(sgu_b_s[i]), sw // SGU_HEADS, axis=1)
        ua, yb = _mix_in(h, bf(w_in[i]), row(sgu_ln_g[i]), row(sgu_ln_b[i]), sgu_w_s[i], bsb, sw=sw)
        yap = _s5_core(ua, *_s5_weights(s5_a_re[i], s5_a_im[i], s5_log_dt[i], s5_b_re[i], s5_b_im[i],
                                        s5_c_re[i], s5_c_im[i], s5_d[i]))
        xs = _merge(xs, h, yap, yb, bf(s5_w_glu[i]), row(s5_b_glu[i]), w_gate[i], row(b_gate[i]),
                    w_branch_a[i], w_branch_b[i], bf(w_out[i]))
        if i == depth - 1:
            xs = _ffn(xs, row(ffn2_norm[i]), ffn2_w_gate[i], ffn2_w_up[i], ffn2_w_down[i], fin, final=True)
        else:
            xs, _ = _ffn(xs, row(ffn2_norm[i]), ffn2_w_gate[i], ffn2_w_up[i], ffn2_w_down[i], fin,
                         final=False)
    return xs.reshape(bsz, seq, d)
```

```python
import functools

import jax
import jax.numpy as jnp
from jax import lax
from jax.experimental import pallas as pl
from jax.experimental.pallas import tpu as pltpu

F32 = jnp.float32
BF16 = jnp.bfloat16

NORM_EPS = 1e-6
S5_GROUP_WIDTH = 16
S5_STATE = 64
SGU_HEADS = 8
SGU_CHUNK = 128
SGU_CAUSAL_BLOCK = 64
S5_CHUNK = 16
LANE = 128
GROUPS_PER_BLOCK = LANE // S5_GROUP_WIDTH
VMEM_LIMIT = 60 * 1024 * 1024
ROW_CHUNK = 512


def _rms(x, g):
    ms = jnp.mean(x * x, axis=-1, keepdims=True)
    return (x * lax.rsqrt(ms + NORM_EPS)) * g


def _dot(a, b):
    return jnp.dot(a, b, preferred_element_type=F32)


def _ffn_kernel(x_ref, g_ref, wg_ref, wu_ref, wd_ref, post_ref, o_ref, *rest, final, wide_gate, n_steps):
    h_ref = rest[0]
    j = pl.program_id(1)
    last = pl.num_programs(1) - 1
    n_chunks = x_ref.shape[0] // ROW_CHUNK

    @pl.when(j == 0)
    def _():
        g = g_ref[...]

        def rows(r, carry):
            rs = pl.ds(pl.multiple_of(r * ROW_CHUNK, ROW_CHUNK), ROW_CHUNK)
            x = x_ref[rs, :]
            h_ref[rs, :] = _rms(x, g).astype(BF16)
            o_ref[rs, :] = x
            return carry

        lax.fori_loop(0, n_chunks, rows, 0, unroll=2)

    tf = wd_ref.shape[0]

    def accumulate(half=0):
        cols = slice(half * tf, (half + 1) * tf)
        h = h_ref[...]
        a = _dot(h, wg_ref[:, cols].astype(BF16))
        b = _dot(h, wu_ref[:, cols].astype(BF16))
        hid = (a * jax.nn.sigmoid(a)) * b
        o_ref[...] += _dot(hid.astype(BF16), (0.5 * wd_ref[...]).astype(BF16))

    if wide_gate:
        odd = lax.rem(j, 2) == 1
        pl.when(jnp.logical_and(j < last, jnp.logical_not(odd)))(functools.partial(accumulate, 0))
        pl.when(jnp.logical_and(j < last, odd))(functools.partial(accumulate, 1))
    else:
        pl.when(j < last)(accumulate)

    @pl.when(j == last)
    def _():
        accumulate((n_steps - 1) % 2 if wide_gate else 0)
        y = _rms(o_ref[...], post_ref[...])
        if final:
            o_ref[...] = y
        else:
            rest[0][...] = y.astype(BF16)


def _next_tile_after(i, j, first_step, n_tiles):
    return jnp.minimum(i + jnp.where(j >= first_step, 1, 0), n_tiles - 1)


def _ffn(x, g, wg, wu, wd, post, *, final, tm=1024, tf=256):
    n, d = x.shape
    dff = wg.shape[1]
    nt = n // tm
    row_tile = pl.BlockSpec((tm, d), lambda i, j: (i, 0))
    if final:
        out_specs, out_shape = row_tile, jax.ShapeDtypeStruct((n, d), F32)
    else:
        out_specs = [row_tile, row_tile]
        out_shape = [jax.ShapeDtypeStruct((n, d), F32), jax.ShapeDtypeStruct((n, d), BF16)]
    wide_gate = final
    n_steps = dff // tf
    gate_up = (pl.BlockSpec((d, 2 * tf), lambda i, j: (0, j // 2)) if wide_gate
               else pl.BlockSpec((d, tf), lambda i, j: (0, j)))
    return pl.pallas_call(
        functools.partial(_ffn_kernel, final=final, wide_gate=wide_gate, n_steps=n_steps),
        grid=(nt, n_steps),
        in_specs=[
            row_tile,
            pl.BlockSpec((1, d), lambda i, j: (0, 0)),
            gate_up,
            gate_up,
            pl.BlockSpec((tf, d), lambda i, j: (j, 0)),
            pl.BlockSpec((1, d), lambda i, j: (0, 0)),
        ],
        out_specs=out_specs,
        out_shape=out_shape,
        scratch_shapes=[pltpu.VMEM((tm, d), BF16)] if final else [],
        compiler_params=pltpu.CompilerParams(
            dimension_semantics=("parallel", "arbitrary"), vmem_limit_bytes=VMEM_LIMIT),
        name="ffn_final" if final else "ffn",
    )(x, g, wg, wu, wd, post)


def _mix_in_kernel(h_ref, w_ref, lng_ref, lnb_ref, ws_ref, bs_ref, ua_ref, yb_ref, vn_ref, *, sw):
    j = pl.program_id(1)

    @pl.when(j == 0)
    def _():
        ua_ref[...] = _dot(h_ref[...], w_ref[...])

    @pl.when(j == 1)
    def _():
        v = jax.nn.gelu(_dot(h_ref[...], w_ref[...]))
        mu = jnp.mean(v, axis=-1, keepdims=True)
        vc = v - mu
        var = jnp.mean(vc * vc, axis=-1, keepdims=True)
        vn_ref[...] = ((vc * lax.rsqrt(var + NORM_EPS)) * lng_ref[...] + lnb_ref[...]).astype(BF16)

    @pl.when(j == 2)
    def _():
        tm = vn_ref.shape[0]
        u = jax.nn.gelu(_dot(h_ref[...], w_ref[...]))
        r = lax.broadcasted_iota(jnp.int32, (SGU_CHUNK, SGU_CHUNK), 0) // SGU_CAUSAL_BLOCK
        c = lax.broadcasted_iota(jnp.int32, (SGU_CHUNK, SGU_CHUNK), 1) // SGU_CAUSAL_BLOCK
        keep = r >= c
        hd = sw // SGU_HEADS
        for h in range(SGU_HEADS):
            wsm = jnp.where(keep, ws_ref[h], 0.0).astype(BF16)
            cols = slice(h * hd, (h + 1) * hd)
            for q in range(tm // SGU_CHUNK):
                rows = slice(q * SGU_CHUNK, (q + 1) * SGU_CHUNK)
                mixed = _dot(wsm, vn_ref[rows, cols]) + bs_ref[:, cols]
                yb_ref[rows, cols] = (u[rows, cols] * mixed).astype(BF16)


def _mix_in(h, w_in, ln_g, ln_b, w_s, bsb, *, sw, tm=1024):
    n, d = h.shape
    assert w_in.shape[1] == 3 * sw
    nt = n // tm
    return pl.pallas_call(
        functools.partial(_mix_in_kernel, sw=sw),
        grid=(nt, 3),
        in_specs=[
            pl.BlockSpec((tm, d), lambda i, j: (i, 0)),
            pl.BlockSpec((d, sw), lambda i, j: (0, jnp.where(j == 0, 0, 3 - j))),
            pl.BlockSpec((1, sw), lambda i, j: (0, 0)),
            pl.BlockSpec((1, sw), lambda i, j: (0, 0)),
            pl.BlockSpec(w_s.shape, lambda i, j: (0, 0, 0)),
            pl.BlockSpec(bsb.shape, lambda i, j: (0, 0)),
        ],
        out_specs=[
            pl.BlockSpec((tm, sw), lambda i, j: (i, 0)),
            pl.BlockSpec((tm, sw), lambda i, j: (i, 0)),
        ],
        out_shape=[
            jax.ShapeDtypeStruct((n, sw), F32),
            jax.ShapeDtypeStruct((n, sw), BF16),
        ],
        scratch_shapes=[pltpu.VMEM((tm, sw), BF16)],
        compiler_params=pltpu.CompilerParams(
            dimension_semantics=("parallel", "arbitrary"), vmem_limit_bytes=VMEM_LIMIT),
        name="mix_in",
    )(h, w_in, ln_g, ln_b, w_s, bsb)


def _dot_split(a, b):
    a_hi = a.astype(BF16)
    b_hi = b.astype(BF16)
    a_lo = (a - a_hi.astype(F32)).astype(BF16)
    b_lo = (b - b_hi.astype(F32)).astype(BF16)
    return _dot(a_hi, b_hi) + (_dot(a_hi, b_lo) + _dot(a_lo, b_hi))


def _expand(x, sel):
    x1 = x.astype(BF16)
    r1 = x - x1.astype(F32)
    x2 = r1.astype(BF16)
    x3 = (r1 - x2.astype(F32)).astype(BF16)
    return _dot(x1, sel) + (_dot(x2, sel) + _dot(x3, sel))


def _expand_t(x, sel):
    tn = (((0,), (0,)), ((), ()))
    x1 = x.astype(BF16)
    r1 = x - x1.astype(F32)
    x2 = r1.astype(BF16)
    x3 = (r1 - x2.astype(F32)).astype(BF16)
    dot_t = lambda v: lax.dot_general(v, sel, tn, preferred_element_type=F32)
    return dot_t(x1) + (dot_t(x2) + dot_t(x3))


def _s5_kernel(u_ref, tab_ref, c2_ref, cb_ref, bre_ref, bim_ref, rep_ref, til_ref, a_ref, d_ref, o_ref,
               dn_ref, upr_ref, upi_ref, wt_ref, wb_ref, wc_ref,
               up_ref, ut_ref, vt_ref, v_ref, xs_ref, xt_ref, yt_ref, yn_ref):
    nc = u_ref.shape[0] // S5_CHUNK
    gw = S5_GROUP_WIDTH
    gb = GROUPS_PER_BLOCK
    tw = S5_CHUNK * gw
    ns = gb * S5_STATE
    sl = 2 * S5_STATE

    tab = tab_ref[...]
    ar2, an, ap = tab[:, 0:sl], tab[:, sl:2 * sl], tab[:, 2 * sl:3 * sl]
    q, qs = tab[:, 3 * sl:4 * sl], tab[:, 4 * sl:5 * sl]
    lo = lax.broadcasted_iota(jnp.int32, (gb, sl), 1) < S5_STATE
    x = jnp.where(lo, 1.0, 0.0)
    xs = jnp.where(lo, 0.0, 1.0)
    dn_ref[pl.ds((S5_CHUNK - 1) * gb, gb), :] = q
    for j in range(1, S5_CHUNK + 1):
        x, xs = x * ar2 + xs * an, xs * ar2 + x * ap
        upr_ref[pl.ds((j - 1) * gb, gb), :] = jnp.where(lo, x, xs)
        upi_ref[pl.ds((j - 1) * gb, gb), :] = jnp.where(lo, xs, x)
        if j < S5_CHUNK:
            q, qs = q * ar2 + qs * an, qs * ar2 + q * ap
            dn_ref[pl.ds((S5_CHUNK - 1 - j) * gb, gb), :] = q

    lane = lax.broadcasted_iota(jnp.int32, (gw, tw), 1)
    for g in range(gb):
        pw = _expand_t(dn_ref[pl.ds(g, S5_CHUNK, stride=gb), :], rep_ref[...])
        br = _expand(bre_ref[g], til_ref[...])
        bi = _expand(bim_ref[g], til_ref[...])
        pr, pi = pw[:S5_STATE], pw[S5_STATE:]
        wbf = jnp.concatenate([pr * br - pi * bi, pr * bi + pi * br], axis=0)
        wb_ref[g] = wbf.astype(BF16)
        hrow = _dot_split(c2_ref[g], wbf)
        ca, cb = c2_ref[g], cb_ref[g]
        for t in range(S5_CHUNK):
            shift = (gw * (t + 1)) % tw
            rolled = pltpu.roll(hrow, shift, axis=1) if shift else hrow
            wt_ref[g, t * gw:(t + 1) * gw, :] = jnp.where(lane < gw * (t + 1), rolled, 0.0).astype(BF16)
            row = t * gb + g
            wc_ref[g, t * gw:(t + 1) * gw, :] = (
                ca * upr_ref[row:row + 1, :] + cb * upi_ref[row:row + 1, :]).astype(BF16)

    for s in range(S5_CHUNK):
        piece = u_ref[pl.ds(s, nc, stride=S5_CHUNK), :]
        up_ref[s] = piece
        ut_ref[s] = piece.T.astype(BF16)

    def chunk_inputs(g):
        return jnp.concatenate(
            [ut_ref[s, g * gw:(g + 1) * gw, :] for s in range(S5_CHUNK)], axis=0)

    for g in range(GROUPS_PER_BLOCK):
        vg = _dot(wb_ref[g], chunk_inputs(g))
        vt_ref[g * S5_STATE:(g + 1) * S5_STATE, :] = vg[:S5_STATE]
        vt_ref[ns + g * S5_STATE:ns + (g + 1) * S5_STATE, :] = vg[S5_STATE:]
    v_ref[...] = vt_ref[...].T

    ar = a_ref[0:1, :]
    ai = a_ref[1:2, :]

    def step(c, carry):
        xr, xi = carry
        xs_ref[pl.ds(c, 1), 0:ns] = xr
        xs_ref[pl.ds(c, 1), ns:2 * ns] = xi
        vr = v_ref[pl.ds(c, 1), 0:ns]
        vi = v_ref[pl.ds(c, 1), ns:2 * ns]
        return ar * xr - ai * xi + vr, ar * xi + ai * xr + vi

    zero = jnp.zeros((1, ns), F32)
    lax.fori_loop(0, nc, step, (zero, zero), unroll=16)
    xt_ref[...] = xs_ref[...].T.astype(BF16)

    for g in range(GROUPS_PER_BLOCK):
        xg = jnp.concatenate(
            [xt_ref[g * S5_STATE:(g + 1) * S5_STATE, :],
             xt_ref[ns + g * S5_STATE:ns + (g + 1) * S5_STATE, :]], axis=0)
        yg = _dot(wt_ref[g], chunk_inputs(g)) + _dot(wc_ref[g], xg)
        for t in range(S5_CHUNK):
            yt_ref[t, g * gw:(g + 1) * gw, :] = yg[t * gw:(t + 1) * gw, :]

    for t in range(S5_CHUNK):
        y = yt_ref[t].T + d_ref[...] * up_ref[t]
        yn_ref[pl.ds(t, nc, stride=S5_CHUNK), :] = jax.nn.gelu(y)
    o_ref[...] = yn_ref[...].astype(BF16)


def _s5_core(u, tab, c2, cb, b_re, b_im, rep, til, a16, dsk):
    n, sw = u.shape
    nb = sw // LANE
    nc = n // S5_CHUNK
    gb = GROUPS_PER_BLOCK
    ns2 = 2 * gb * S5_STATE
    tw = S5_CHUNK * S5_GROUP_WIDTH
    per_group = lambda a: pl.BlockSpec((gb,) + a.shape[1:], lambda j: (j,) + (0,) * (a.ndim - 1))
    whole = lambda a: pl.BlockSpec(a.shape, lambda j: (0, 0))
    table_rows = pltpu.VMEM((S5_CHUNK * gb, 2 * S5_STATE), F32)
    return pl.pallas_call(
        _s5_kernel,
        grid=(nb,),
        in_specs=[
            pl.BlockSpec((n, LANE), lambda j: (0, j)),
            per_group(tab), per_group(c2), per_group(cb), per_group(b_re), per_group(b_im),
            whole(rep), whole(til),
            pl.BlockSpec((None, 2, ns2 // 2), lambda j: (j, 0, 0)),
            pl.BlockSpec((1, LANE), lambda j: (0, j)),
        ],
        out_specs=pl.BlockSpec((n, LANE), lambda j: (0, j)),
        out_shape=jax.ShapeDtypeStruct((n, sw), BF16),
        scratch_shapes=[
            table_rows, table_rows, table_rows,
            pltpu.VMEM((gb, tw, tw), BF16),
            pltpu.VMEM((gb, 2 * S5_STATE, tw), BF16),
            pltpu.VMEM((gb, tw, 2 * S5_STATE), BF16),
            pltpu.VMEM((S5_CHUNK, nc, LANE), F32),
            pltpu.VMEM((S5_CHUNK, LANE, nc), BF16),
            pltpu.VMEM((ns2, nc), F32),
            pltpu.VMEM((nc, ns2), F32),
            pltpu.VMEM((nc, ns2), F32),
            pltpu.VMEM((ns2, nc), BF16),
            pltpu.VMEM((S5_CHUNK, LANE, nc), F32),
            pltpu.VMEM((n, LANE), F32),
        ],
        compiler_params=pltpu.CompilerParams(
            dimension_semantics=("parallel",), vmem_limit_bytes=VMEM_LIMIT),
        name="s5_core",
    )(u, tab, c2, cb, b_re, b_im, rep, til, a16, dsk)


def _s5_weights(a_re, a_im, log_dt, b_re, b_im, c_re, c_im, d_skip):
    g, p = a_re.shape
    dt = jnp.exp(log_dt)[:, None]
    decay = jnp.exp(a_re * dt)
    ab_re = decay * jnp.cos(a_im * dt)
    ab_im = decay * jnp.sin(a_im * dt)
    denom = a_re * a_re + a_im * a_im
    num_re = ab_re - 1.0
    num_im = ab_im
    k_re = (num_re * a_re + num_im * a_im) / denom
    k_im = (num_im * a_re - num_re * a_im) / denom
    tab = jnp.concatenate([ab_re, ab_re, -ab_im, ab_im, ab_im, -ab_im, k_re, k_im, k_im, k_re], axis=1)
    r, i = ab_re, ab_im
    for _ in range(S5_CHUNK.bit_length() - 1):
        r, i = r * r - i * i, 2.0 * (r * i)
    gb = GROUPS_PER_BLOCK
    a16 = jnp.stack([r.reshape(g // gb, gb * p), i.reshape(g // gb, gb * p)], axis=1)
    c2 = jnp.concatenate([c_re, -c_im], axis=-1)
    cb = jnp.concatenate([-c_im, -c_re], axis=-1)
    eye = jnp.eye(S5_GROUP_WIDTH, dtype=BF16)
    rep = jnp.repeat(eye, S5_GROUP_WIDTH, axis=1)
    til = jnp.tile(eye, (1, S5_CHUNK))
    return (tab, c2, cb, b_re, b_im, rep, til, a16, d_skip.reshape(1, -1))


def _merge_kernel(x_ref, h_ref, yap_ref, yb_ref, wglu_ref, bglu_ref, wga_ref, wgb_ref,
                  bga_ref, bgb_ref, wa_ref, wb_ref, *rest, n1, n2):
    wo_refs = rest[:n2]
    o_ref, ya_ref, m_ref = rest[n2:]
    j = pl.program_id(1)

    def gated_tile(with_glu):
        if with_glu:
            yp = yap_ref[...]
            z = _dot(yp, wglu_ref[...]) + bglu_ref[...]
            ya_ref[...] = (yp.astype(F32) * jax.nn.sigmoid(z)).astype(BF16)
        h = h_ref[...]
        ga = jax.nn.sigmoid(_dot(h, wga_ref[...].astype(BF16)) + bga_ref[...])
        gb = jax.nn.sigmoid(_dot(h, wgb_ref[...].astype(BF16)) + bgb_ref[...])
        pa = _dot(ya_ref[...], wa_ref[...].astype(BF16))
        pb = _dot(yb_ref[...], wb_ref[...].astype(BF16))
        m_ref[j] = (ga * pa + gb * pb).astype(BF16)

    pl.when(j == 0)(functools.partial(gated_tile, True))
    pl.when(jnp.logical_and(j > 0, j < n1))(functools.partial(gated_tile, False))

    for k in range(n2):
        @pl.when(j == n1 + k)
        def _(k=k):
            m = jnp.concatenate([m_ref[q] for q in range(n1)], axis=1)
            o_ref[...] = x_ref[...] + _dot(m, wo_refs[k][...])


def _merge(x, h, yap, yb, wglu, bglu, wgate, bgate, wa, wb, wo, *, tm=1024, t1=256, t2=512):
    n, d = x.shape
    sw = yap.shape[1]
    n1, n2 = d // t1, d // t2
    nt = n // tm
    second = lambda j: jnp.maximum(j - n1, 0)

    def first(j, back):
        return jnp.where(j < n1, j, jnp.where(j < n1 + back, n1 - 1, 0))

    once = pl.Buffered(1)
    return pl.pallas_call(
        functools.partial(_merge_kernel, n1=n1, n2=n2),
        grid=(nt, n1 + n2),
        in_specs=[
            pl.BlockSpec((tm, t2), lambda i, j: (i, second(j))),
            pl.BlockSpec((tm, d), lambda i, j: (_next_tile_after(i, j, n1, nt), 0)),
            pl.BlockSpec((tm, sw), lambda i, j: (_next_tile_after(i, j, 1, nt), 0)),
            pl.BlockSpec((tm, sw), lambda i, j: (_next_tile_after(i, j, n1, nt), 0)),
            pl.BlockSpec((sw, sw), lambda i, j: (0, 0), pipeline_mode=once),
            pl.BlockSpec((1, sw), lambda i, j: (0, 0)),
            pl.BlockSpec((d, t1), lambda i, j: (0, first(j, 1))),
            pl.BlockSpec((d, t1), lambda i, j: (0, first(j, 2) + n1)),
            pl.BlockSpec((1, t1), lambda i, j: (0, first(j, 1))),
            pl.BlockSpec((1, t1), lambda i, j: (0, first(j, 2) + n1)),
            pl.BlockSpec((sw, t1), lambda i, j: (0, first(j, 3))),
            pl.BlockSpec((sw, t1), lambda i, j: (0, first(j, 3))),
        ] + [pl.BlockSpec((d, t2), lambda i, j, k=k: (0, k), pipeline_mode=once) for k in range(n2)],
        out_specs=pl.BlockSpec((tm, t2), lambda i, j: (i, second(j))),
        out_shape=jax.ShapeDtypeStruct((n, d), F32),
        scratch_shapes=[pltpu.VMEM((tm, sw), BF16), pltpu.VMEM((n1, tm, t1), BF16)],
        compiler_params=pltpu.CompilerParams(
            dimension_semantics=("parallel", "arbitrary"), vmem_limit_bytes=VMEM_LIMIT),
        name="merge",
    )(x, h, yap, yb, wglu, bglu, wgate, wgate, bgate, bgate, wa, wb, *([wo] * n2))


def kernel(x, ffn1_norm, ffn1_w_gate, ffn1_w_up, ffn1_w_down, mix_norm, w_in, s5_a_re, s5_a_im, s5_log_dt, s5_b_re, s5_b_im, s5_c_re, s5_c_im, s5_d, s5_w_glu, s5_b_glu, sgu_ln_g, sgu_ln_b, sgu_w_s, sgu_b_s, w_branch_a, w_branch_b, w_gate, b_gate, w_out, ffn2_norm, ffn2_w_gate, ffn2_w_up, ffn2_w_down, final_norm):
    bsz, seq, d = x.shape
    depth = ffn1_norm.shape[0]
    sw = s5_w_glu.shape[1]
    assert bsz == 1 and sw == w_branch_b.shape[1] and sw % LANE == 0
    assert seq % (S5_CHUNK * LANE) == 0 and sgu_w_s.shape[2] == SGU_CHUNK
    bf = lambda w: w.astype(BF16)
    row = lambda v: v.reshape(1, -1)
    xs = x.reshape(seq, d)
    fin = row(final_norm)
    for i in range(depth):
        xs, h = _ffn(xs, row(ffn1_norm[i]), ffn1_w_gate[i], ffn1_w_up[i], ffn1_w_down[i],
                     row(mix_norm[i]), final=False)
        bsb = jnp.repeat(jnp.transpose(sgu_b_s[i]), sw // SGU_HEADS, axis=1)
        ua, yb = _mix_in(h, bf(w_in[i]), row(sgu_ln_g[i]), row(sgu_ln_b[i]), sgu_w_s[i], bsb, sw=sw)
        yap = _s5_core(ua, *_s5_weights(s5_a_re[i], s5_a_im[i], s5_log_dt[i], s5_b_re[i], s5_b_im[i],
                                        s5_c_re[i], s5_c_im[i], s5_d[i]))
        xs = _merge(xs, h, yap, yb, bf(s5_w_glu[i]), row(s5_b_glu[i]), w_gate[i], row(b_gate[i]),
                    w_branch_a[i], w_branch_b[i], bf(w_out[i]))
        if i == depth - 1:
            xs = _ffn(xs, row(ffn2_norm[i]), ffn2_w_gate[i], ffn2_w_up[i], ffn2_w_down[i], fin, final=True)
        else:
            xs, _ = _ffn(xs, row(ffn2_norm[i]), ffn2_w_gate[i], ffn2_w_up[i], ffn2_w_down[i], fin,
                         final=False)
    return xs.reshape(bsz, seq, d)
```

```python
import functools

import jax
import jax.numpy as jnp
from jax import lax
from jax.experimental import pallas as pl
from jax.experimental.pallas import tpu as pltpu

F32 = jnp.float32
BF16 = jnp.bfloat16

NORM_EPS = 1e-6
S5_GROUP_WIDTH = 16
S5_STATE = 64
SGU_HEADS = 8
SGU_CHUNK = 128
SGU_CAUSAL_BLOCK = 64
S5_CHUNK = 16
LANE = 128
GROUPS_PER_BLOCK = LANE // S5_GROUP_WIDTH
VMEM_LIMIT = 60 * 1024 * 1024
ROW_CHUNK = 512


def _rms(x, g):
    ms = jnp.mean(x * x, axis=-1, keepdims=True)
    return (x * lax.rsqrt(ms + NORM_EPS)) * g


def _dot(a, b):
    return jnp.dot(a, b, preferred_element_type=F32)


def _ffn_kernel(x_ref, g_ref, wg_ref, wu_ref, wd_ref, post_ref, o_ref, *rest, final):
    h_ref = rest[0]
    j = pl.program_id(1)
    last = pl.num_programs(1) - 1
    n_chunks = x_ref.shape[0] // ROW_CHUNK

    @pl.when(j == 0)
    def _():
        g = g_ref[...]

        def rows(r, carry):
            rs = pl.ds(pl.multiple_of(r * ROW_CHUNK, ROW_CHUNK), ROW_CHUNK)
            x = x_ref[rs, :]
            h_ref[rs, :] = _rms(x, g).astype(BF16)
            o_ref[rs, :] = x
            return carry

        lax.fori_loop(0, n_chunks, rows, 0, unroll=2)

    def accumulate():
        h = h_ref[...]
        a = _dot(h, wg_ref[...].astype(BF16))
        b = _dot(h, wu_ref[...].astype(BF16))
        hid = (a * jax.nn.sigmoid(a)) * b
        o_ref[...] += _dot(hid.astype(BF16), (0.5 * wd_ref[...]).astype(BF16))

    pl.when(j < last)(accumulate)

    @pl.when(j == last)
    def _():
        accumulate()
        y = _rms(o_ref[...], post_ref[...])
        if final:
            o_ref[...] = y
        else:
            rest[0][...] = y.astype(BF16)


def _next_tile_after(i, j, first_step, n_tiles):
    return jnp.minimum(i + jnp.where(j >= first_step, 1, 0), n_tiles - 1)


def _ffn(x, g, wg, wu, wd, post, *, final, tm=1024, tf=256):
    n, d = x.shape
    dff = wg.shape[1]
    nt = n // tm
    row_tile = pl.BlockSpec((tm, d), lambda i, j: (i, 0))
    if final:
        out_specs, out_shape = row_tile, jax.ShapeDtypeStruct((n, d), F32)
    else:
        out_specs = [row_tile, row_tile]
        out_shape = [jax.ShapeDtypeStruct((n, d), F32), jax.ShapeDtypeStruct((n, d), BF16)]
    return pl.pallas_call(
        functools.partial(_ffn_kernel, final=final),
        grid=(nt, dff // tf),
        in_specs=[
            row_tile,
            pl.BlockSpec((1, d), lambda i, j: (0, 0)),
            pl.BlockSpec((d, tf), lambda i, j: (0, j)),
            pl.BlockSpec((d, tf), lambda i, j: (0, j)),
            pl.BlockSpec((tf, d), lambda i, j: (j, 0)),
            pl.BlockSpec((1, d), lambda i, j: (0, 0)),
        ],
        out_specs=out_specs,
        out_shape=out_shape,
        scratch_shapes=[pltpu.VMEM((tm, d), BF16)] if final else [],
        compiler_params=pltpu.CompilerParams(
            dimension_semantics=("parallel", "arbitrary"), vmem_limit_bytes=VMEM_LIMIT),
        name="ffn_final" if final else "ffn",
    )(x, g, wg, wu, wd, post)


def _mix_in_kernel(h_ref, w_ref, lng_ref, lnb_ref, ws_ref, bs_ref, ua_ref, yb_ref, vn_ref, *, sw):
    j = pl.program_id(1)

    @pl.when(j == 0)
    def _():
        ua_ref[...] = _dot(h_ref[...], w_ref[...])

    @pl.when(j == 1)
    def _():
        v = jax.nn.gelu(_dot(h_ref[...], w_ref[...]))
        mu = jnp.mean(v, axis=-1, keepdims=True)
        vc = v - mu
        var = jnp.mean(vc * vc, axis=-1, keepdims=True)
        vn_ref[...] = ((vc * lax.rsqrt(var + NORM_EPS)) * lng_ref[...] + lnb_ref[...]).astype(BF16)

    @pl.when(j == 2)
    def _():
        tm = vn_ref.shape[0]
        u = jax.nn.gelu(_dot(h_ref[...], w_ref[...]))
        r = lax.broadcasted_iota(jnp.int32, (SGU_CHUNK, SGU_CHUNK), 0) // SGU_CAUSAL_BLOCK
        c = lax.broadcasted_iota(jnp.int32, (SGU_CHUNK, SGU_CHUNK), 1) // SGU_CAUSAL_BLOCK
        keep = r >= c
        hd = sw // SGU_HEADS
        for h in range(SGU_HEADS):
            wsm = jnp.where(keep, ws_ref[h], 0.0).astype(BF16)
            cols = slice(h * hd, (h + 1) * hd)
            for q in range(tm // SGU_CHUNK):
                rows = slice(q * SGU_CHUNK, (q + 1) * SGU_CHUNK)
                mixed = _dot(wsm, vn_ref[rows, cols]) + bs_ref[:, cols]
                yb_ref[rows, cols] = (u[rows, cols] * mixed).astype(BF16)


def _mix_in(h, w_in, ln_g, ln_b, w_s, bsb, *, sw, tm=1024):
    n, d = h.shape
    assert w_in.shape[1] == 3 * sw
    nt = n // tm
    return pl.pallas_call(
        functools.partial(_mix_in_kernel, sw=sw),
        grid=(nt, 3),
        in_specs=[
            pl.BlockSpec((tm, d), lambda i, j: (i, 0)),
            pl.BlockSpec((d, sw), lambda i, j: (0, jnp.where(j == 0, 0, 3 - j))),
            pl.BlockSpec((1, sw), lambda i, j: (0, 0)),
            pl.BlockSpec((1, sw), lambda i, j: (0, 0)),
            pl.BlockSpec(w_s.shape, lambda i, j: (0, 0, 0)),
            pl.BlockSpec(bsb.shape, lambda i, j: (0, 0)),
        ],
        out_specs=[
            pl.BlockSpec((tm, sw), lambda i, j: (i, 0)),
            pl.BlockSpec((tm, sw), lambda i, j: (i, 0)),
        ],
        out_shape=[
            jax.ShapeDtypeStruct((n, sw), F32),
            jax.ShapeDtypeStruct((n, sw), BF16),
        ],
        scratch_shapes=[pltpu.VMEM((tm, sw), BF16)],
        compiler_params=pltpu.CompilerParams(
            dimension_semantics=("parallel", "arbitrary"), vmem_limit_bytes=VMEM_LIMIT),
        name="mix_in",
    )(h, w_in, ln_g, ln_b, w_s, bsb)


def _dot_split(a, b):
    a_hi = a.astype(BF16)
    b_hi = b.astype(BF16)
    a_lo = (a - a_hi.astype(F32)).astype(BF16)
    b_lo = (b - b_hi.astype(F32)).astype(BF16)
    return _dot(a_hi, b_hi) + (_dot(a_hi, b_lo) + _dot(a_lo, b_hi))


def _expand(x, sel):
    x1 = x.astype(BF16)
    r1 = x - x1.astype(F32)
    x2 = r1.astype(BF16)
    x3 = (r1 - x2.astype(F32)).astype(BF16)
    return _dot(x1, sel) + (_dot(x2, sel) + _dot(x3, sel))


def _expand_t(x, sel):
    tn = (((0,), (0,)), ((), ()))
    x1 = x.astype(BF16)
    r1 = x - x1.astype(F32)
    x2 = r1.astype(BF16)
    x3 = (r1 - x2.astype(F32)).astype(BF16)
    dot_t = lambda v: lax.dot_general(v, sel, tn, preferred_element_type=F32)
    return dot_t(x1) + (dot_t(x2) + dot_t(x3))


def _s5_kernel(u_ref, tab_ref, c2_ref, cb_ref, bre_ref, bim_ref, rep_ref, til_ref, a_ref, d_ref, o_ref,
               dn_ref, upr_ref, upi_ref, wt_ref, wb_ref, wc_ref,
               up_ref, ut_ref, vt_ref, v_ref, xs_ref, xt_ref, yt_ref, yn_ref):
    nc = u_ref.shape[0] // S5_CHUNK
    gw = S5_GROUP_WIDTH
    gb = GROUPS_PER_BLOCK
    tw = S5_CHUNK * gw
    ns = gb * S5_STATE
    sl = 2 * S5_STATE

    tab = tab_ref[...]
    ar2, an, ap = tab[:, 0:sl], tab[:, sl:2 * sl], tab[:, 2 * sl:3 * sl]
    q, qs = tab[:, 3 * sl:4 * sl], tab[:, 4 * sl:5 * sl]
    lo = lax.broadcasted_iota(jnp.int32, (gb, sl), 1) < S5_STATE
    x = jnp.where(lo, 1.0, 0.0)
    xs = jnp.where(lo, 0.0, 1.0)
    dn_ref[pl.ds((S5_CHUNK - 1) * gb, gb), :] = q
    for j in range(1, S5_CHUNK + 1):
        x, xs = x * ar2 + xs * an, xs * ar2 + x * ap
        upr_ref[pl.ds((j - 1) * gb, gb), :] = jnp.where(lo, x, xs)
        upi_ref[pl.ds((j - 1) * gb, gb), :] = jnp.where(lo, xs, x)
        if j < S5_CHUNK:
            q, qs = q * ar2 + qs * an, qs * ar2 + q * ap
            dn_ref[pl.ds((S5_CHUNK - 1 - j) * gb, gb), :] = q

    lane = lax.broadcasted_iota(jnp.int32, (gw, tw), 1)
    for g in range(gb):
        pw = _expand_t(dn_ref[pl.ds(g, S5_CHUNK, stride=gb), :], rep_ref[...])
        br = _expand(bre_ref[g], til_ref[...])
        bi = _expand(bim_ref[g], til_ref[...])
        pr, pi = pw[:S5_STATE], pw[S5_STATE:]
        wbf = jnp.concatenate([pr * br - pi * bi, pr * bi + pi * br], axis=0)
        wb_ref[g] = wbf.astype(BF16)
        hrow = _dot_split(c2_ref[g], wbf)
        ca, cb = c2_ref[g], cb_ref[g]
        for t in range(S5_CHUNK):
            shift = (gw * (t + 1)) % tw
            rolled = pltpu.roll(hrow, shift, axis=1) if shift else hrow
            wt_ref[g, t * gw:(t + 1) * gw, :] = jnp.where(lane < gw * (t + 1), rolled, 0.0).astype(BF16)
            row = t * gb + g
            wc_ref[g, t * gw:(t + 1) * gw, :] = (
                ca * upr_ref[row:row + 1, :] + cb * upi_ref[row:row + 1, :]).astype(BF16)

    for s in range(S5_CHUNK):
        piece = u_ref[pl.ds(s, nc, stride=S5_CHUNK), :]
        up_ref[s] = piece
        ut_ref[s] = piece.T.astype(BF16)

    def chunk_inputs(g):
        return jnp.concatenate(
            [ut_ref[s, g * gw:(g + 1) * gw, :] for s in range(S5_CHUNK)], axis=0)

    for g in range(GROUPS_PER_BLOCK):
        vg = _dot(wb_ref[g], chunk_inputs(g))
        vt_ref[g * S5_STATE:(g + 1) * S5_STATE, :] = vg[:S5_STATE]
        vt_ref[ns + g * S5_STATE:ns + (g + 1) * S5_STATE, :] = vg[S5_STATE:]
    v_ref[...] = vt_ref[...].T

    ar = a_ref[0:1, :]
    ai = a_ref[1:2, :]

    def step(c, carry):
        xr, xi = carry
        xs_ref[pl.ds(c, 1), 0:ns] = xr
        xs_ref[pl.ds(c, 1), ns:2 * ns] = xi
        vr = v_ref[pl.ds(c, 1), 0:ns]
        vi = v_ref[pl.ds(c, 1), ns:2 * ns]
        return ar * xr - ai * xi + vr, ar * xi + ai * xr + vi

    zero = jnp.zeros((1, ns), F32)
    lax.fori_loop(0, nc, step, (zero, zero), unroll=16)
    xt_ref[...] = xs_ref[...].T.astype(BF16)

    for g in range(GROUPS_PER_BLOCK):
        xg = jnp.concatenate(
            [xt_ref[g * S5_STATE:(g + 1) * S5_STATE, :],
             xt_ref[ns + g * S5_STATE:ns + (g + 1) * S5_STATE, :]], axis=0)
        yg = _dot(wt_ref[g], chunk_inputs(g)) + _dot(wc_ref[g], xg)
        for t in range(S5_CHUNK):
            yt_ref[t, g * gw:(g + 1) * gw, :] = yg[t * gw:(t + 1) * gw, :]

    for t in range(S5_CHUNK):
        y = yt_ref[t].T + d_ref[...] * up_ref[t]
        yn_ref[pl.ds(t, nc, stride=S5_CHUNK), :] = jax.nn.gelu(y)
    o_ref[...] = yn_ref[...].astype(BF16)


def _s5_core(u, tab, c2, cb, b_re, b_im, rep, til, a16, dsk):
    n, sw = u.shape
    nb = sw // LANE
    nc = n // S5_CHUNK
    gb = GROUPS_PER_BLOCK
    ns2 = 2 * gb * S5_STATE
    tw = S5_CHUNK * S5_GROUP_WIDTH
    per_group = lambda a: pl.BlockSpec((gb,) + a.shape[1:], lambda j: (j,) + (0,) * (a.ndim - 1))
    whole = lambda a: pl.BlockSpec(a.shape, lambda j: (0, 0))
    table_rows = pltpu.VMEM((S5_CHUNK * gb, 2 * S5_STATE), F32)
    return pl.pallas_call(
        _s5_kernel,
        grid=(nb,),
        in_specs=[
            pl.BlockSpec((n, LANE), lambda j: (0, j)),
            per_group(tab), per_group(c2), per_group(cb), per_group(b_re), per_group(b_im),
            whole(rep), whole(til),
            pl.BlockSpec((None, 2, ns2 // 2), lambda j: (j, 0, 0)),
            pl.BlockSpec((1, LANE), lambda j: (0, j)),
        ],
        out_specs=pl.BlockSpec((n, LANE), lambda j: (0, j)),
        out_shape=jax.ShapeDtypeStruct((n, sw), BF16),
        scratch_shapes=[
            table_rows, table_rows, table_rows,
            pltpu.VMEM((gb, tw, tw), BF16),
            pltpu.VMEM((gb, 2 * S5_STATE, tw), BF16),
            pltpu.VMEM((gb, tw, 2 * S5_STATE), BF16),
            pltpu.VMEM((S5_CHUNK, nc, LANE), F32),
            pltpu.VMEM((S5_CHUNK, LANE, nc), BF16),
            pltpu.VMEM((ns2, nc), F32),
            pltpu.VMEM((nc, ns2), F32),
            pltpu.VMEM((nc, ns2), F32),
            pltpu.VMEM((ns2, nc), BF16),
            pltpu.VMEM((S5_CHUNK, LANE, nc), F32),
            pltpu.VMEM((n, LANE), F32),
        ],
        compiler_params=pltpu.CompilerParams(
            dimension_semantics=("parallel",), vmem_limit_bytes=VMEM_LIMIT),
        name="s5_core",
    )(u, tab, c2, cb, b_re, b_im, rep, til, a16, dsk)


def _s5_weights(a_re, a_im, log_dt, b_re, b_im, c_re, c_im, d_skip):
    g, p = a_re.shape
    dt = jnp.exp(log_dt)[:, None]
    decay = jnp.exp(a_re * dt)
    ab_re = decay * jnp.cos(a_im * dt)
    ab_im = decay * jnp.sin(a_im * dt)
    denom = a_re * a_re + a_im * a_im
    num_re = ab_re - 1.0
    num_im = ab_im
    k_re = (num_re * a_re + num_im * a_im) / denom
    k_im = (num_im * a_re - num_re * a_im) / denom
    tab = jnp.concatenate([ab_re, ab_re, -ab_im, ab_im, ab_im, -ab_im, k_re, k_im, k_im, k_re], axis=1)
    r, i = ab_re, ab_im
    for _ in range(S5_CHUNK.bit_length() - 1):
        r, i = r * r - i * i, 2.0 * (r * i)
    gb = GROUPS_PER_BLOCK
    a16 = jnp.stack([r.reshape(g // gb, gb * p), i.reshape(g // gb, gb * p)], axis=1)
    c2 = jnp.concatenate([c_re, -c_im], axis=-1)
    cb = jnp.concatenate([-c_im, -c_re], axis=-1)
    eye = jnp.eye(S5_GROUP_WIDTH, dtype=BF16)
    rep = jnp.repeat(eye, S5_GROUP_WIDTH, axis=1)
    til = jnp.tile(eye, (1, S5_CHUNK))
    return (tab, c2, cb, b_re, b_im, rep, til, a16, d_skip.reshape(1, -1))


def _merge_kernel(x_ref, h_ref, yap_ref, yb_ref, wglu_ref, bglu_ref, wga_ref, wgb_ref,
                  bga_ref, bgb_ref, wa_ref, wb_ref, *rest, n1, n2):
    wo_refs = rest[:n2]
    o_ref, ya_ref, m_ref = rest[n2:]
    j = pl.program_id(1)

    def gated_tile(with_glu):
        if with_glu:
            yp = yap_ref[...]
            z = _dot(yp, wglu_ref[...].astype(BF16)) + bglu_ref[...]
            ya_ref[...] = (yp.astype(F32) * jax.nn.sigmoid(z)).astype(BF16)
        h = h_ref[...]
        ga = jax.nn.sigmoid(_dot(h, wga_ref[...].astype(BF16)) + bga_ref[...])
        gb = jax.nn.sigmoid(_dot(h, wgb_ref[...].astype(BF16)) + bgb_ref[...])
        pa = _dot(ya_ref[...], wa_ref[...].astype(BF16))
        pb = _dot(yb_ref[...], wb_ref[...].astype(BF16))
        m_ref[j] = (ga * pa + gb * pb).astype(BF16)

    pl.when(j == 0)(functools.partial(gated_tile, True))
    pl.when(jnp.logical_and(j > 0, j < n1))(functools.partial(gated_tile, False))

    for k in range(n2):
        @pl.when(j == n1 + k)
        def _(k=k):
            m = jnp.concatenate([m_ref[q] for q in range(n1)], axis=1)
            o_ref[...] = x_ref[...] + _dot(m, wo_refs[k][...])


def _merge(x, h, yap, yb, wglu, bglu, wgate, bgate, wa, wb, wo, *, tm=1024, t1=256, t2=512):
    n, d = x.shape
    sw = yap.shape[1]
    n1, n2 = d // t1, d // t2
    nt = n // tm
    second = lambda j: jnp.maximum(j - n1, 0)

    def first(j, back):
        return jnp.where(j < n1, j, jnp.where(j < n1 + back, n1 - 1, 0))

    once = pl.Buffered(1)
    return pl.pallas_call(
        functools.partial(_merge_kernel, n1=n1, n2=n2),
        grid=(nt, n1 + n2),
        in_specs=[
            pl.BlockSpec((tm, t2), lambda i, j: (i, second(j))),
            pl.BlockSpec((tm, d), lambda i, j: (_next_tile_after(i, j, n1, nt), 0)),
            pl.BlockSpec((tm, sw), lambda i, j: (_next_tile_after(i, j, 1, nt), 0)),
            pl.BlockSpec((tm, sw), lambda i, j: (_next_tile_after(i, j, n1, nt), 0)),
            pl.BlockSpec((sw, sw), lambda i, j: (0, 0), pipeline_mode=once),
            pl.BlockSpec((1, sw), lambda i, j: (0, 0)),
            pl.BlockSpec((d, t1), lambda i, j: (0, first(j, 1))),
            pl.BlockSpec((d, t1), lambda i, j: (0, first(j, 2) + n1)),
            pl.BlockSpec((1, t1), lambda i, j: (0, first(j, 1))),
            pl.BlockSpec((1, t1), lambda i, j: (0, first(j, 2) + n1)),
            pl.BlockSpec((sw, t1), lambda i, j: (0, first(j, 3))),
            pl.BlockSpec((sw, t1), lambda i, j: (0, first(j, 3))),
        ] + [pl.BlockSpec((d, t2), lambda i, j, k=k: (0, k), pipeline_mode=once) for k in range(n2)],
        out_specs=pl.BlockSpec((tm, t2), lambda i, j: (i, second(j))),
        out_shape=jax.ShapeDtypeStruct((n, d), F32),
        scratch_shapes=[pltpu.VMEM((tm, sw), BF16), pltpu.VMEM((n1, tm, t1), BF16)],
        compiler_params=pltpu.CompilerParams(
            dimension_semantics=("parallel", "arbitrary"), vmem_limit_bytes=VMEM_LIMIT),
        name="merge",
    )(x, h, yap, yb, wglu, bglu, wgate, wgate, bgate, bgate, wa, wb, *([wo] * n2))


def kernel(x, ffn1_norm, ffn1_w_gate, ffn1_w_up, ffn1_w_down, mix_norm, w_in, s5_a_re, s5_a_im, s5_log_dt, s5_b_re, s5_b_im, s5_c_re, s5_c_im, s5_d, s5_w_glu, s5_b_glu, sgu_ln_g, sgu_ln_b, sgu_w_s, sgu_b_s, w_branch_a, w_branch_b, w_gate, b_gate, w_out, ffn2_norm, ffn2_w_gate, ffn2_w_up, ffn2_w_down, final_norm):
    bsz, seq, d = x.shape
    depth = ffn1_norm.shape[0]
    sw = s5_w_glu.shape[1]
    assert bsz == 1 and sw == w_branch_b.shape[1] and sw % LANE == 0
    assert seq % (S5_CHUNK * LANE) == 0 and sgu_w_s.shape[2] == SGU_CHUNK
    bf = lambda w: w.astype(BF16)
    row = lambda v: v.reshape(1, -1)
    xs = x.reshape(seq, d)
    fin = row(final_norm)
    for i in range(depth):
        xs, h = _ffn(xs, row(ffn1_norm[i]), ffn1_w_gate[i], ffn1_w_up[i], ffn1_w_down[i],
                     row(mix_norm[i]), final=False)
        bsb = jnp.repeat(jnp.transpose(sgu_b_s[i]), sw // SGU_HEADS, axis=1)
        ua, yb = _mix_in(h, bf(w_in[i]), row(sgu_ln_g[i]), row(sgu_ln_b[i]), sgu_w_s[i], bsb, sw=sw)
        yap = _s5_core(ua, *_s5_weights(s5_a_re[i], s5_a_im[i], s5_log_dt[i], s5_b_re[i], s5_b_im[i],
                                        s5_c_re[i], s5_c_im[i], s5_d[i]))
        xs = _merge(xs, h, yap, yb, s5_w_glu[i], row(s5_b_glu[i]), w_gate[i], row(b_gate[i]),
                    w_branch_a[i], w_branch_b[i], bf(w_out[i]))
        if i == depth - 1:
            xs = _ffn(xs, row(ffn2_norm[i]), ffn2_w_gate[i], ffn2_w_up[i], ffn2_w_down[i], fin, final=True)
        else:
            xs, _ = _ffn(xs, row(ffn2_norm[i]), ffn2_w_gate[i], ffn2_w_up[i], ffn2_w_down[i], fin,
                         final=False)
    return xs.reshape(bsz, seq, d)
```

```python
import functools

import jax
import jax.numpy as jnp
from jax import lax
from jax.experimental import pallas as pl
from jax.experimental.pallas import tpu as pltpu

F32 = jnp.float32
BF16 = jnp.bfloat16

NORM_EPS = 1e-6
S5_GROUP_WIDTH = 16
S5_STATE = 64
SGU_HEADS = 8
SGU_CHUNK = 128
SGU_CAUSAL_BLOCK = 64
S5_CHUNK = 16
LANE = 128
GROUPS_PER_BLOCK = LANE // S5_GROUP_WIDTH
VMEM_LIMIT = 60 * 1024 * 1024
ROW_CHUNK = 512
WEIGHT_RING = 3


def _rms(x, g):
    ms = jnp.mean(x * x, axis=-1, keepdims=True)
    return (x * lax.rsqrt(ms + NORM_EPS)) * g


def _dot(a, b):
    return jnp.dot(a, b, preferred_element_type=F32)


def _ffn_kernel(x_ref, g_ref, wg_hbm, wu_hbm, wd_hbm, post_ref, o_ref, *rest, final, tf, n_steps):
    h_ref, wg_buf, wu_buf, wd_buf, sem = rest
    i = pl.program_id(0)
    j = pl.program_id(1)
    last = n_steps - 1
    t = i * n_steps + j
    total = pl.num_programs(0) * n_steps
    n_chunks = x_ref.shape[0] // ROW_CHUNK

    def tile_copies(step, slot):
        cols = pl.ds(pl.multiple_of(lax.rem(step, n_steps) * tf, tf), tf)
        return (pltpu.make_async_copy(wg_hbm.at[:, cols], wg_buf.at[slot], sem.at[0, slot]),
                pltpu.make_async_copy(wu_hbm.at[:, cols], wu_buf.at[slot], sem.at[1, slot]),
                pltpu.make_async_copy(wd_hbm.at[cols, :], wd_buf.at[slot], sem.at[2, slot]))

    @pl.when(t == 0)
    def _():
        for step in range(WEIGHT_RING - 1):
            for cp in tile_copies(step, step):
                cp.start()

    @pl.when(j == 0)
    def _():
        g = g_ref[...]

        def rows(r, carry):
            rs = pl.ds(pl.multiple_of(r * ROW_CHUNK, ROW_CHUNK), ROW_CHUNK)
            x = x_ref[rs, :]
            h_ref[rs, :] = _rms(x, g).astype(BF16)
            o_ref[rs, :] = x
            return carry

        lax.fori_loop(0, n_chunks, rows, 0, unroll=2)

    def step_body(slot, is_last):
        for cp in tile_copies(t, slot):
            cp.wait()
        ahead = t + (WEIGHT_RING - 1)

        @pl.when(ahead < total)
        def _():
            for cp in tile_copies(ahead, (slot + WEIGHT_RING - 1) % WEIGHT_RING):
                cp.start()

        h = h_ref[...]
        a = _dot(h, wg_buf[slot].astype(BF16))
        b = _dot(h, wu_buf[slot].astype(BF16))
        hid = (a * jax.nn.sigmoid(a)) * b
        o_ref[...] += _dot(hid.astype(BF16), (0.5 * wd_buf[slot]).astype(BF16))
        if is_last:
            y = _rms(o_ref[...], post_ref[...])
            if final:
                o_ref[...] = y
            else:
                h_ref[...] = y.astype(BF16)

    phase = lax.rem(t, WEIGHT_RING)
    for slot in range(WEIGHT_RING):
        here = phase == slot
        pl.when(jnp.logical_and(here, j < last))(functools.partial(step_body, slot, False))
        pl.when(jnp.logical_and(here, j == last))(functools.partial(step_body, slot, True))


def _next_tile_after(i, j, first_step, n_tiles):
    return jnp.minimum(i + jnp.where(j >= first_step, 1, 0), n_tiles - 1)


def _ffn(x, g, wg, wu, wd, post, *, final, tm=1024, tf=256):
    n, d = x.shape
    dff = wg.shape[1]
    nt = n // tm
    row_tile = pl.BlockSpec((tm, d), lambda i, j: (i, 0))
    in_hbm = pl.BlockSpec(memory_space=pl.ANY)
    if final:
        out_specs, out_shape = row_tile, jax.ShapeDtypeStruct((n, d), F32)
    else:
        out_specs = [row_tile, row_tile]
        out_shape = [jax.ShapeDtypeStruct((n, d), F32), jax.ShapeDtypeStruct((n, d), BF16)]
    return pl.pallas_call(
        functools.partial(_ffn_kernel, final=final, tf=tf, n_steps=dff // tf),
        grid=(nt, dff // tf),
        in_specs=[
            row_tile,
            pl.BlockSpec((1, d), lambda i, j: (0, 0)),
            in_hbm, in_hbm, in_hbm,
            pl.BlockSpec((1, d), lambda i, j: (0, 0)),
        ],
        out_specs=out_specs,
        out_shape=out_shape,
        scratch_shapes=([pltpu.VMEM((tm, d), BF16)] if final else []) + [
            pltpu.VMEM((WEIGHT_RING, d, tf), F32),
            pltpu.VMEM((WEIGHT_RING, d, tf), F32),
            pltpu.VMEM((WEIGHT_RING, tf, d), F32),
            pltpu.SemaphoreType.DMA((3, WEIGHT_RING)),
        ],
        compiler_params=pltpu.CompilerParams(
            dimension_semantics=("arbitrary", "arbitrary"), vmem_limit_bytes=VMEM_LIMIT),
        name="ffn_final" if final else "ffn",
    )(x, g, wg, wu, wd, post)


def _mix_in_kernel(h_ref, w_ref, lng_ref, lnb_ref, ws_ref, bs_ref, ua_ref, yb_ref, vn_ref, *, sw):
    j = pl.program_id(1)

    @pl.when(j == 0)
    def _():
        ua_ref[...] = _dot(h_ref[...], w_ref[...])

    @pl.when(j == 1)
    def _():
        v = jax.nn.gelu(_dot(h_ref[...], w_ref[...]))
        mu = jnp.mean(v, axis=-1, keepdims=True)
        vc = v - mu
        var = jnp.mean(vc * vc, axis=-1, keepdims=True)
        vn_ref[...] = ((vc * lax.rsqrt(var + NORM_EPS)) * lng_ref[...] + lnb_ref[...]).astype(BF16)

    @pl.when(j == 2)
    def _():
        tm = vn_ref.shape[0]
        u = jax.nn.gelu(_dot(h_ref[...], w_ref[...]))
        r = lax.broadcasted_iota(jnp.int32, (SGU_CHUNK, SGU_CHUNK), 0) // SGU_CAUSAL_BLOCK
        c = lax.broadcasted_iota(jnp.int32, (SGU_CHUNK, SGU_CHUNK), 1) // SGU_CAUSAL_BLOCK
        keep = r >= c
        hd = sw // SGU_HEADS
        for h in range(SGU_HEADS):
            wsm = jnp.where(keep, ws_ref[h], 0.0).astype(BF16)
            cols = slice(h * hd, (h + 1) * hd)
            for q in range(tm // SGU_CHUNK):
                rows = slice(q * SGU_CHUNK, (q + 1) * SGU_CHUNK)
                mixed = _dot(wsm, vn_ref[rows, cols]) + bs_ref[:, cols]
                yb_ref[rows, cols] = (u[rows, cols] * mixed).astype(BF16)


def _mix_in(h, w_in, ln_g, ln_b, w_s, bsb, *, sw, tm=1024):
    n, d = h.shape
    assert w_in.shape[1] == 3 * sw
    nt = n // tm
    return pl.pallas_call(
        functools.partial(_mix_in_kernel, sw=sw),
        grid=(nt, 3),
        in_specs=[
            pl.BlockSpec((tm, d), lambda i, j: (i, 0)),
            pl.BlockSpec((d, sw), lambda i, j: (0, jnp.where(j == 0, 0, 3 - j))),
            pl.BlockSpec((1, sw), lambda i, j: (0, 0)),
            pl.BlockSpec((1, sw), lambda i, j: (0, 0)),
            pl.BlockSpec(w_s.shape, lambda i, j: (0, 0, 0)),
            pl.BlockSpec(bsb.shape, lambda i, j: (0, 0)),
        ],
        out_specs=[
            pl.BlockSpec((tm, sw), lambda i, j: (i, 0)),
            pl.BlockSpec((tm, sw), lambda i, j: (i, 0)),
        ],
        out_shape=[
            jax.ShapeDtypeStruct((n, sw), F32),
            jax.ShapeDtypeStruct((n, sw), BF16),
        ],
        scratch_shapes=[pltpu.VMEM((tm, sw), BF16)],
        compiler_params=pltpu.CompilerParams(
            dimension_semantics=("parallel", "arbitrary"), vmem_limit_bytes=VMEM_LIMIT),
        name="mix_in",
    )(h, w_in, ln_g, ln_b, w_s, bsb)


def _dot_split(a, b):
    a_hi = a.astype(BF16)
    b_hi = b.astype(BF16)
    a_lo = (a - a_hi.astype(F32)).astype(BF16)
    b_lo = (b - b_hi.astype(F32)).astype(BF16)
    return _dot(a_hi, b_hi) + (_dot(a_hi, b_lo) + _dot(a_lo, b_hi))


def _expand(x, sel):
    x1 = x.astype(BF16)
    r1 = x - x1.astype(F32)
    x2 = r1.astype(BF16)
    x3 = (r1 - x2.astype(F32)).astype(BF16)
    return _dot(x1, sel) + (_dot(x2, sel) + _dot(x3, sel))


def _expand_t(x, sel):
    tn = (((0,), (0,)), ((), ()))
    x1 = x.astype(BF16)
    r1 = x - x1.astype(F32)
    x2 = r1.astype(BF16)
    x3 = (r1 - x2.astype(F32)).astype(BF16)
    dot_t = lambda v: lax.dot_general(v, sel, tn, preferred_element_type=F32)
    return dot_t(x1) + (dot_t(x2) + dot_t(x3))


def _s5_kernel(u_ref, tab_ref, c2_ref, cb_ref, bre_ref, bim_ref, rep_ref, til_ref, a_ref, d_ref, o_ref,
               dn_ref, upr_ref, upi_ref, wt_ref, wb_ref, wc_ref,
               up_ref, ut_ref, vt_ref, v_ref, xs_ref, xt_ref, yt_ref, yn_ref):
    nc = u_ref.shape[0] // S5_CHUNK
    gw = S5_GROUP_WIDTH
    gb = GROUPS_PER_BLOCK
    tw = S5_CHUNK * gw
    ns = gb * S5_STATE
    sl = 2 * S5_STATE

    tab = tab_ref[...]
    ar2, an, ap = tab[:, 0:sl], tab[:, sl:2 * sl], tab[:, 2 * sl:3 * sl]
    q, qs = tab[:, 3 * sl:4 * sl], tab[:, 4 * sl:5 * sl]
    lo = lax.broadcasted_iota(jnp.int32, (gb, sl), 1) < S5_STATE
    x = jnp.where(lo, 1.0, 0.0)
    xs = jnp.where(lo, 0.0, 1.0)
    dn_ref[pl.ds((S5_CHUNK - 1) * gb, gb), :] = q
    for j in range(1, S5_CHUNK + 1):
        x, xs = x * ar2 + xs * an, xs * ar2 + x * ap
        upr_ref[pl.ds((j - 1) * gb, gb), :] = jnp.where(lo, x, xs)
        upi_ref[pl.ds((j - 1) * gb, gb), :] = jnp.where(lo, xs, x)
        if j < S5_CHUNK:
            q, qs = q * ar2 + qs * an, qs * ar2 + q * ap
            dn_ref[pl.ds((S5_CHUNK - 1 - j) * gb, gb), :] = q

    lane = lax.broadcasted_iota(jnp.int32, (gw, tw), 1)
    for g in range(gb):
        pw = _expand_t(dn_ref[pl.ds(g, S5_CHUNK, stride=gb), :], rep_ref[...])
        br = _expand(bre_ref[g], til_ref[...])
        bi = _expand(bim_ref[g], til_ref[...])
        pr, pi = pw[:S5_STATE], pw[S5_STATE:]
        wbf = jnp.concatenate([pr * br - pi * bi, pr * bi + pi * br], axis=0)
        wb_ref[g] = wbf.astype(BF16)
        hrow = _dot_split(c2_ref[g], wbf)
        ca, cb = c2_ref[g], cb_ref[g]
        for t in range(S5_CHUNK):
            shift = (gw * (t + 1)) % tw
            rolled = pltpu.roll(hrow, shift, axis=1) if shift else hrow
            wt_ref[g, t * gw:(t + 1) * gw, :] = jnp.where(lane < gw * (t + 1), rolled, 0.0).astype(BF16)
            row = t * gb + g
            wc_ref[g, t * gw:(t + 1) * gw, :] = (
                ca * upr_ref[row:row + 1, :] + cb * upi_ref[row:row + 1, :]).astype(BF16)

    for s in range(S5_CHUNK):
        piece = u_ref[pl.ds(s, nc, stride=S5_CHUNK), :]
        up_ref[s] = piece
        ut_ref[s] = piece.T.astype(BF16)

    def chunk_inputs(g):
        return jnp.concatenate(
            [ut_ref[s, g * gw:(g + 1) * gw, :] for s in range(S5_CHUNK)], axis=0)

    for g in range(GROUPS_PER_BLOCK):
        vg = _dot(wb_ref[g], chunk_inputs(g))
        vt_ref[g * S5_STATE:(g + 1) * S5_STATE, :] = vg[:S5_STATE]
        vt_ref[ns + g * S5_STATE:ns + (g + 1) * S5_STATE, :] = vg[S5_STATE:]
    v_ref[...] = vt_ref[...].T

    ar = a_ref[0:1, :]
    ai = a_ref[1:2, :]

    def step(c, carry):
        xr, xi = carry
        xs_ref[pl.ds(c, 1), 0:ns] = xr
        xs_ref[pl.ds(c, 1), ns:2 * ns] = xi
        vr = v_ref[pl.ds(c, 1), 0:ns]
        vi = v_ref[pl.ds(c, 1), ns:2 * ns]
        return ar * xr - ai * xi + vr, ar * xi + ai * xr + vi

    zero = jnp.zeros((1, ns), F32)
    lax.fori_loop(0, nc, step, (zero, zero), unroll=16)
    xt_ref[...] = xs_ref[...].T.astype(BF16)

    for g in range(GROUPS_PER_BLOCK):
        xg = jnp.concatenate(
            [xt_ref[g * S5_STATE:(g + 1) * S5_STATE, :],
             xt_ref[ns + g * S5_STATE:ns + (g + 1) * S5_STATE, :]], axis=0)
        yg = _dot(wt_ref[g], chunk_inputs(g)) + _dot(wc_ref[g], xg)
        for t in range(S5_CHUNK):
            yt_ref[t, g * gw:(g + 1) * gw, :] = yg[t * gw:(t + 1) * gw, :]

    for t in range(S5_CHUNK):
        y = yt_ref[t].T + d_ref[...] * up_ref[t]
        yn_ref[pl.ds(t, nc, stride=S5_CHUNK), :] = jax.nn.gelu(y)
    o_ref[...] = yn_ref[...].astype(BF16)


def _s5_core(u, tab, c2, cb, b_re, b_im, rep, til, a16, dsk):
    n, sw = u.shape
    nb = sw // LANE
    nc = n // S5_CHUNK
    gb = GROUPS_PER_BLOCK
    ns2 = 2 * gb * S5_STATE
    tw = S5_CHUNK * S5_GROUP_WIDTH
    per_group = lambda a: pl.BlockSpec((gb,) + a.shape[1:], lambda j: (j,) + (0,) * (a.ndim - 1))
    whole = lambda a: pl.BlockSpec(a.shape, lambda j: (0, 0))
    table_rows = pltpu.VMEM((S5_CHUNK * gb, 2 * S5_STATE), F32)
    return pl.pallas_call(
        _s5_kernel,
        grid=(nb,),
        in_specs=[
            pl.BlockSpec((n, LANE), lambda j: (0, j)),
            per_group(tab), per_group(c2), per_group(cb), per_group(b_re), per_group(b_im),
            whole(rep), whole(til),
            pl.BlockSpec((None, 2, ns2 // 2), lambda j: (j, 0, 0)),
            pl.BlockSpec((1, LANE), lambda j: (0, j)),
        ],
        out_specs=pl.BlockSpec((n, LANE), lambda j: (0, j)),
        out_shape=jax.ShapeDtypeStruct((n, sw), BF16),
        scratch_shapes=[
            table_rows, table_rows, table_rows,
            pltpu.VMEM((gb, tw, tw), BF16),
            pltpu.VMEM((gb, 2 * S5_STATE, tw), BF16),
            pltpu.VMEM((gb, tw, 2 * S5_STATE), BF16),
            pltpu.VMEM((S5_CHUNK, nc, LANE), F32),
            pltpu.VMEM((S5_CHUNK, LANE, nc), BF16),
            pltpu.VMEM((ns2, nc), F32),
            pltpu.VMEM((nc, ns2), F32),
            pltpu.VMEM((nc, ns2), F32),
            pltpu.VMEM((ns2, nc), BF16),
            pltpu.VMEM((S5_CHUNK, LANE, nc), F32),
            pltpu.VMEM((n, LANE), F32),
        ],
        compiler_params=pltpu.CompilerParams(
            dimension_semantics=("parallel",), vmem_limit_bytes=VMEM_LIMIT),
        name="s5_core",
    )(u, tab, c2, cb, b_re, b_im, rep, til, a16, dsk)


def _s5_weights(a_re, a_im, log_dt, b_re, b_im, c_re, c_im, d_skip):
    g, p = a_re.shape
    dt = jnp.exp(log_dt)[:, None]
    decay = jnp.exp(a_re * dt)
    ab_re = decay * jnp.cos(a_im * dt)
    ab_im = decay * jnp.sin(a_im * dt)
    denom = a_re * a_re + a_im * a_im
    num_re = ab_re - 1.0
    num_im = ab_im
    k_re = (num_re * a_re + num_im * a_im) / denom
    k_im = (num_im * a_re - num_re * a_im) / denom
    tab = jnp.concatenate([ab_re, ab_re, -ab_im, ab_im, ab_im, -ab_im, k_re, k_im, k_im, k_re], axis=1)
    r, i = ab_re, ab_im
    for _ in range(S5_CHUNK.bit_length() - 1):
        r, i = r * r - i * i, 2.0 * (r * i)
    gb = GROUPS_PER_BLOCK
    a16 = jnp.stack([r.reshape(g // gb, gb * p), i.reshape(g // gb, gb * p)], axis=1)
    c2 = jnp.concatenate([c_re, -c_im], axis=-1)
    cb = jnp.concatenate([-c_im, -c_re], axis=-1)
    eye = jnp.eye(S5_GROUP_WIDTH, dtype=BF16)
    rep = jnp.repeat(eye, S5_GROUP_WIDTH, axis=1)
    til = jnp.tile(eye, (1, S5_CHUNK))
    return (tab, c2, cb, b_re, b_im, rep, til, a16, d_skip.reshape(1, -1))


def _merge_kernel(x_ref, h_ref, yap_ref, yb_ref, wglu_ref, bglu_ref, wga_ref, wgb_ref,
                  bga_ref, bgb_ref, wa_ref, wb_ref, *rest, n1, n2):
    wo_refs = rest[:n2]
    o_ref, ya_ref, m_ref = rest[n2:]
    j = pl.program_id(1)

    def gated_tile(with_glu):
        if with_glu:
            yp = yap_ref[...]
            z = _dot(yp, wglu_ref[...]) + bglu_ref[...]
            ya_ref[...] = (yp.astype(F32) * jax.nn.sigmoid(z)).astype(BF16)
        h = h_ref[...]
        ga = jax.nn.sigmoid(_dot(h, wga_ref[...].astype(BF16)) + bga_ref[...])
        gb = jax.nn.sigmoid(_dot(h, wgb_ref[...].astype(BF16)) + bgb_ref[...])
        pa = _dot(ya_ref[...], wa_ref[...].astype(BF16))
        pb = _dot(yb_ref[...], wb_ref[...].astype(BF16))
        m_ref[j] = (ga * pa + gb * pb).astype(BF16)

    pl.when(j == 0)(functools.partial(gated_tile, True))
    pl.when(jnp.logical_and(j > 0, j < n1))(functools.partial(gated_tile, False))

    for k in range(n2):
        @pl.when(j == n1 + k)
        def _(k=k):
            m = jnp.concatenate([m_ref[q] for q in range(n1)], axis=1)
            o_ref[...] = x_ref[...] + _dot(m, wo_refs[k][...])


def _merge(x, h, yap, yb, wglu, bglu, wgate, bgate, wa, wb, wo, *, tm=1024, t1=256, t2=512):
    n, d = x.shape
    sw = yap.shape[1]
    n1, n2 = d // t1, d // t2
    nt = n // tm
    second = lambda j: jnp.maximum(j - n1, 0)

    def first(j, back):
        return jnp.where(j < n1, j, jnp.where(j < n1 + back, n1 - 1, 0))

    once = pl.Buffered(1)
    return pl.pallas_call(
        functools.partial(_merge_kernel, n1=n1, n2=n2),
        grid=(nt, n1 + n2),
        in_specs=[
            pl.BlockSpec((tm, t2), lambda i, j: (i, second(j))),
            pl.BlockSpec((tm, d), lambda i, j: (_next_tile_after(i, j, n1, nt), 0)),
            pl.BlockSpec((tm, sw), lambda i, j: (_next_tile_after(i, j, 1, nt), 0)),
            pl.BlockSpec((tm, sw), lambda i, j: (_next_tile_after(i, j, n1, nt), 0)),
            pl.BlockSpec((sw, sw), lambda i, j: (0, 0), pipeline_mode=once),
            pl.BlockSpec((1, sw), lambda i, j: (0, 0)),
            pl.BlockSpec((d, t1), lambda i, j: (0, first(j, 1))),
            pl.BlockSpec((d, t1), lambda i, j: (0, first(j, 2) + n1)),
            pl.BlockSpec((1, t1), lambda i, j: (0, first(j, 1))),
            pl.BlockSpec((1, t1), lambda i, j: (0, first(j, 2) + n1)),
            pl.BlockSpec((sw, t1), lambda i, j: (0, first(j, 3))),
            pl.BlockSpec((sw, t1), lambda i, j: (0, first(j, 3))),
        ] + [pl.BlockSpec((d, t2), lambda i, j, k=k: (0, k), pipeline_mode=once) for k in range(n2)],
        out_specs=pl.BlockSpec((tm, t2), lambda i, j: (i, second(j))),
        out_shape=jax.ShapeDtypeStruct((n, d), F32),
        scratch_shapes=[pltpu.VMEM((tm, sw), BF16), pltpu.VMEM((n1, tm, t1), BF16)],
        compiler_params=pltpu.CompilerParams(
            dimension_semantics=("parallel", "arbitrary"), vmem_limit_bytes=VMEM_LIMIT),
        name="merge",
    )(x, h, yap, yb, wglu, bglu, wgate, wgate, bgate, bgate, wa, wb, *([wo] * n2))


def kernel(x, ffn1_norm, ffn1_w_gate, ffn1_w_up, ffn1_w_down, mix_norm, w_in, s5_a_re, s5_a_im, s5_log_dt, s5_b_re, s5_b_im, s5_c_re, s5_c_im, s5_d, s5_w_glu, s5_b_glu, sgu_ln_g, sgu_ln_b, sgu_w_s, sgu_b_s, w_branch_a, w_branch_b, w_gate, b_gate, w_out, ffn2_norm, ffn2_w_gate, ffn2_w_up, ffn2_w_down, final_norm):
    bsz, seq, d = x.shape
    depth = ffn1_norm.shape[0]
    sw = s5_w_glu.shape[1]
    assert bsz == 1 and sw == w_branch_b.shape[1] and sw % LANE == 0
    assert seq % (S5_CHUNK * LANE) == 0 and sgu_w_s.shape[2] == SGU_CHUNK
    bf = lambda w: w.astype(BF16)
    row = lambda v: v.reshape(1, -1)
    xs = x.reshape(seq, d)
    fin = row(final_norm)
    for i in range(depth):
        xs, h = _ffn(xs, row(ffn1_norm[i]), ffn1_w_gate[i], ffn1_w_up[i], ffn1_w_down[i],
                     row(mix_norm[i]), final=False)
        bsb = jnp.repeat(jnp.transpose(sgu_b_s[i]), sw // SGU_HEADS, axis=1)
        ua, yb = _mix_in(h, bf(w_in[i]), row(sgu_ln_g[i]), row(sgu_ln_b[i]), sgu_w_s[i], bsb, sw=sw)
        yap = _s5_core(ua, *_s5_weights(s5_a_re[i], s5_a_im[i], s5_log_dt[i], s5_b_re[i], s5_b_im[i],
                                        s5_c_re[i], s5_c_im[i], s5_d[i]))
        xs = _merge(xs, h, yap, yb, bf(s5_w_glu[i]), row(s5_b_glu[i]), w_gate[i], row(b_gate[i]),
                    w_branch_a[i], w_branch_b[i], bf(w_out[i]))
        if i == depth - 1:
            xs = _ffn(xs, row(ffn2_norm[i]), ffn2_w_gate[i], ffn2_w_up[i], ffn2_w_down[i], fin, final=True)
        else:
            xs, _ = _ffn(xs, row(ffn2_norm[i]), ffn2_w_gate[i], ffn2_w_up[i], ffn2_w_down[i], fin,
                         final=False)
    return xs.reshape(bsz, seq, d)
```

```python
import functools

import jax
import jax.numpy as jnp
from jax import lax
from jax.experimental import pallas as pl
from jax.experimental.pallas import tpu as pltpu

F32 = jnp.float32
BF16 = jnp.bfloat16

NORM_EPS = 1e-6
S5_GROUP_WIDTH = 16
S5_STATE = 64
SGU_HEADS = 8
SGU_CHUNK = 128
SGU_CAUSAL_BLOCK = 64
S5_CHUNK = 16
LANE = 128
GROUPS_PER_BLOCK = LANE // S5_GROUP_WIDTH
VMEM_LIMIT = 60 * 1024 * 1024
ROW_CHUNK = 512
LN_ROW_BLOCKS = 4


def _rms(x, g):
    ms = jnp.mean(x * x, axis=-1, keepdims=True)
    return (x * lax.rsqrt(ms + NORM_EPS)) * g


def _dot(a, b):
    return jnp.dot(a, b, preferred_element_type=F32)


def _ffn_kernel(x_ref, g_ref, wg_ref, wu_ref, wd_ref, post_ref, o_ref, *rest, final):
    h_ref = rest[0]
    j = pl.program_id(1)
    last = pl.num_programs(1) - 1
    n_chunks = x_ref.shape[0] // ROW_CHUNK

    @pl.when(j == 0)
    def _():
        g = g_ref[...]

        def rows(r, carry):
            rs = pl.ds(pl.multiple_of(r * ROW_CHUNK, ROW_CHUNK), ROW_CHUNK)
            x = x_ref[rs, :]
            h_ref[rs, :] = _rms(x, g).astype(BF16)
            o_ref[rs, :] = x
            return carry

        lax.fori_loop(0, n_chunks, rows, 0, unroll=2)

    def accumulate():
        h = h_ref[...]
        a = _dot(h, wg_ref[...].astype(BF16))
        b = _dot(h, wu_ref[...].astype(BF16))
        hid = (a * jax.nn.sigmoid(a)) * b
        o_ref[...] += _dot(hid.astype(BF16), (0.5 * wd_ref[...]).astype(BF16))

    pl.when(j < last)(accumulate)

    @pl.when(j == last)
    def _():
        accumulate()
        y = _rms(o_ref[...], post_ref[...])
        if final:
            o_ref[...] = y
        else:
            rest[0][...] = y.astype(BF16)


def _next_tile_after(i, j, first_step, n_tiles):
    return jnp.minimum(i + jnp.where(j >= first_step, 1, 0), n_tiles - 1)


def _ffn(x, g, wg, wu, wd, post, *, final, tm=1024, tf=256):
    n, d = x.shape
    dff = wg.shape[1]
    nt = n // tm
    row_tile = pl.BlockSpec((tm, d), lambda i, j: (i, 0))
    if final:
        out_specs, out_shape = row_tile, jax.ShapeDtypeStruct((n, d), F32)
    else:
        out_specs = [row_tile, row_tile]
        out_shape = [jax.ShapeDtypeStruct((n, d), F32), jax.ShapeDtypeStruct((n, d), BF16)]
    return pl.pallas_call(
        functools.partial(_ffn_kernel, final=final),
        grid=(nt, dff // tf),
        in_specs=[
            row_tile,
            pl.BlockSpec((1, d), lambda i, j: (0, 0)),
            pl.BlockSpec((d, tf), lambda i, j: (0, j)),
            pl.BlockSpec((d, tf), lambda i, j: (0, j)),
            pl.BlockSpec((tf, d), lambda i, j: (j, 0)),
            pl.BlockSpec((1, d), lambda i, j: (0, 0)),
        ],
        out_specs=out_specs,
        out_shape=out_shape,
        scratch_shapes=[pltpu.VMEM((tm, d), BF16)] if final else [],
        compiler_params=pltpu.CompilerParams(
            dimension_semantics=("parallel", "arbitrary"), vmem_limit_bytes=VMEM_LIMIT),
        name="ffn_final" if final else "ffn",
    )(x, g, wg, wu, wd, post)


def _mix_in_kernel(h_ref, w_ref, lng_ref, lnb_ref, ws_ref, bs_ref, ua_ref, yb_ref, vn_ref, *, sw):
    j = pl.program_id(1)

    @pl.when(j == 0)
    def _():
        ua_ref[...] = _dot(h_ref[...], w_ref[...])

    @pl.when(j == 1)
    def _():
        rb = vn_ref.shape[0] // LN_ROW_BLOCKS
        for k in range(LN_ROW_BLOCKS):
            rows = slice(k * rb, (k + 1) * rb)
            v = jax.nn.gelu(_dot(h_ref[rows, :], w_ref[...]))
            mu = jnp.mean(v, axis=-1, keepdims=True)
            vc = v - mu
            var = jnp.mean(vc * vc, axis=-1, keepdims=True)
            vn_ref[rows, :] = ((vc * lax.rsqrt(var + NORM_EPS)) * lng_ref[...] + lnb_ref[...]).astype(BF16)

    @pl.when(j == 2)
    def _():
        tm = vn_ref.shape[0]
        u = jax.nn.gelu(_dot(h_ref[...], w_ref[...]))
        r = lax.broadcasted_iota(jnp.int32, (SGU_CHUNK, SGU_CHUNK), 0) // SGU_CAUSAL_BLOCK
        c = lax.broadcasted_iota(jnp.int32, (SGU_CHUNK, SGU_CHUNK), 1) // SGU_CAUSAL_BLOCK
        keep = r >= c
        hd = sw // SGU_HEADS
        for h in range(SGU_HEADS):
            wsm = jnp.where(keep, ws_ref[h], 0.0).astype(BF16)
            cols = slice(h * hd, (h + 1) * hd)
            for q in range(0, tm // SGU_CHUNK, 2):
                ra = slice(q * SGU_CHUNK, (q + 1) * SGU_CHUNK)
                rb = slice((q + 1) * SGU_CHUNK, (q + 2) * SGU_CHUNK)
                both = _dot(wsm, jnp.concatenate([vn_ref[ra, cols], vn_ref[rb, cols]], axis=1))
                yb_ref[ra, cols] = (u[ra, cols] * (both[:, :hd] + bs_ref[:, cols])).astype(BF16)
                yb_ref[rb, cols] = (u[rb, cols] * (both[:, hd:] + bs_ref[:, cols])).astype(BF16)


def _mix_in(h, w_in, ln_g, ln_b, w_s, bsb, *, sw, tm=1024):
    n, d = h.shape
    assert w_in.shape[1] == 3 * sw
    nt = n // tm
    return pl.pallas_call(
        functools.partial(_mix_in_kernel, sw=sw),
        grid=(nt, 3),
        in_specs=[
            pl.BlockSpec((tm, d), lambda i, j: (i, 0)),
            pl.BlockSpec((d, sw), lambda i, j: (0, jnp.where(j == 0, 0, 3 - j))),
            pl.BlockSpec((1, sw), lambda i, j: (0, 0)),
            pl.BlockSpec((1, sw), lambda i, j: (0, 0)),
            pl.BlockSpec(w_s.shape, lambda i, j: (0, 0, 0)),
            pl.BlockSpec(bsb.shape, lambda i, j: (0, 0)),
        ],
        out_specs=[
            pl.BlockSpec((tm, sw), lambda i, j: (i, 0)),
            pl.BlockSpec((tm, sw), lambda i, j: (i, 0)),
        ],
        out_shape=[
            jax.ShapeDtypeStruct((n, sw), F32),
            jax.ShapeDtypeStruct((n, sw), BF16),
        ],
        scratch_shapes=[pltpu.VMEM((tm, sw), BF16)],
        compiler_params=pltpu.CompilerParams(
            dimension_semantics=("parallel", "arbitrary"), vmem_limit_bytes=VMEM_LIMIT),
        name="mix_in",
    )(h, w_in, ln_g, ln_b, w_s, bsb)


def _dot_split(a, b):
    a_hi = a.astype(BF16)
    b_hi = b.astype(BF16)
    a_lo = (a - a_hi.astype(F32)).astype(BF16)
    b_lo = (b - b_hi.astype(F32)).astype(BF16)
    return _dot(a_hi, b_hi) + (_dot(a_hi, b_lo) + _dot(a_lo, b_hi))


def _expand(x, sel):
    x1 = x.astype(BF16)
    r1 = x - x1.astype(F32)
    x2 = r1.astype(BF16)
    x3 = (r1 - x2.astype(F32)).astype(BF16)
    return _dot(x1, sel) + (_dot(x2, sel) + _dot(x3, sel))


def _expand_t(x, sel):
    tn = (((0,), (0,)), ((), ()))
    x1 = x.astype(BF16)
    r1 = x - x1.astype(F32)
    x2 = r1.astype(BF16)
    x3 = (r1 - x2.astype(F32)).astype(BF16)
    dot_t = lambda v: lax.dot_general(v, sel, tn, preferred_element_type=F32)
    return dot_t(x1) + (dot_t(x2) + dot_t(x3))


def _s5_kernel(u_ref, tab_ref, c2_ref, cb_ref, bre_ref, bim_ref, rep_ref, til_ref, a_ref, d_ref, o_ref,
               dn_ref, upr_ref, upi_ref, wt_ref, wb_ref, wc_ref,
               up_ref, ut_ref, vt_ref, v_ref, xs_ref, xt_ref, yt_ref, yn_ref):
    nc = u_ref.shape[0] // S5_CHUNK
    gw = S5_GROUP_WIDTH
    gb = GROUPS_PER_BLOCK
    tw = S5_CHUNK * gw
    ns = gb * S5_STATE
    sl = 2 * S5_STATE

    tab = tab_ref[...]
    ar2, an, ap = tab[:, 0:sl], tab[:, sl:2 * sl], tab[:, 2 * sl:3 * sl]
    q, qs = tab[:, 3 * sl:4 * sl], tab[:, 4 * sl:5 * sl]
    lo = lax.broadcasted_iota(jnp.int32, (gb, sl), 1) < S5_STATE
    x = jnp.where(lo, 1.0, 0.0)
    xs = jnp.where(lo, 0.0, 1.0)
    dn_ref[pl.ds((S5_CHUNK - 1) * gb, gb), :] = q
    for j in range(1, S5_CHUNK + 1):
        x, xs = x * ar2 + xs * an, xs * ar2 + x * ap
        upr_ref[pl.ds((j - 1) * gb, gb), :] = jnp.where(lo, x, xs)
        upi_ref[pl.ds((j - 1) * gb, gb), :] = jnp.where(lo, xs, x)
        if j < S5_CHUNK:
            q, qs = q * ar2 + qs * an, qs * ar2 + q * ap
            dn_ref[pl.ds((S5_CHUNK - 1 - j) * gb, gb), :] = q

    lane = lax.broadcasted_iota(jnp.int32, (gw, tw), 1)
    for g in range(gb):
        pw = _expand_t(dn_ref[pl.ds(g, S5_CHUNK, stride=gb), :], rep_ref[...])
        br = _expand(bre_ref[g], til_ref[...])
        bi = _expand(bim_ref[g], til_ref[...])
        pr, pi = pw[:S5_STATE], pw[S5_STATE:]
        wbf = jnp.concatenate([pr * br - pi * bi, pr * bi + pi * br], axis=0)
        wb_ref[g] = wbf.astype(BF16)
        hrow = _dot_split(c2_ref[g], wbf)
        ca, cb = c2_ref[g], cb_ref[g]
        for t in range(S5_CHUNK):
            shift = (gw * (t + 1)) % tw
            rolled = pltpu.roll(hrow, shift, axis=1) if shift else hrow
            wt_ref[g, t * gw:(t + 1) * gw, :] = jnp.where(lane < gw * (t + 1), rolled, 0.0).astype(BF16)
            row = t * gb + g
            wc_ref[g, t * gw:(t + 1) * gw, :] = (
                ca * upr_ref[row:row + 1, :] + cb * upi_ref[row:row + 1, :]).astype(BF16)

    for s in range(S5_CHUNK):
        piece = u_ref[pl.ds(s, nc, stride=S5_CHUNK), :]
        up_ref[s] = piece
        ut_ref[s] = piece.T.astype(BF16)

    def chunk_inputs(g):
        return jnp.concatenate(
            [ut_ref[s, g * gw:(g + 1) * gw, :] for s in range(S5_CHUNK)], axis=0)

    for g in range(GROUPS_PER_BLOCK):
        vg = _dot(wb_ref[g], chunk_inputs(g))
        vt_ref[g * S5_STATE:(g + 1) * S5_STATE, :] = vg[:S5_STATE]
        vt_ref[ns + g * S5_STATE:ns + (g + 1) * S5_STATE, :] = vg[S5_STATE:]
    v_ref[...] = vt_ref[...].T

    ar = a_ref[0:1, :]
    ai = a_ref[1:2, :]

    def step(c, carry):
        xr, xi = carry
        xs_ref[pl.ds(c, 1), 0:ns] = xr
        xs_ref[pl.ds(c, 1), ns:2 * ns] = xi
        vr = v_ref[pl.ds(c, 1), 0:ns]
        vi = v_ref[pl.ds(c, 1), ns:2 * ns]
        return ar * xr - ai * xi + vr, ar * xi + ai * xr + vi

    zero = jnp.zeros((1, ns), F32)
    lax.fori_loop(0, nc, step, (zero, zero), unroll=True)
    xt_ref[...] = xs_ref[...].T.astype(BF16)

    for g in range(GROUPS_PER_BLOCK):
        xg = jnp.concatenate(
            [xt_ref[g * S5_STATE:(g + 1) * S5_STATE, :],
             xt_ref[ns + g * S5_STATE:ns + (g + 1) * S5_STATE, :]], axis=0)
        yg = _dot(wt_ref[g], chunk_inputs(g)) + _dot(wc_ref[g], xg)
        for t in range(S5_CHUNK):
            yt_ref[t, g * gw:(g + 1) * gw, :] = yg[t * gw:(t + 1) * gw, :]

    for t in range(S5_CHUNK):
        y = yt_ref[t].T + d_ref[...] * up_ref[t]
        yn_ref[pl.ds(t, nc, stride=S5_CHUNK), :] = jax.nn.gelu(y)
    o_ref[...] = yn_ref[...].astype(BF16)


def _s5_core(u, tab, c2, cb, b_re, b_im, rep, til, a16, dsk):
    n, sw = u.shape
    nb = sw // LANE
    nc = n // S5_CHUNK
    gb = GROUPS_PER_BLOCK
    ns2 = 2 * gb * S5_STATE
    tw = S5_CHUNK * S5_GROUP_WIDTH
    per_group = lambda a: pl.BlockSpec((gb,) + a.shape[1:], lambda j: (j,) + (0,) * (a.ndim - 1))
    whole = lambda a: pl.BlockSpec(a.shape, lambda j: (0, 0))
    table_rows = pltpu.VMEM((S5_CHUNK * gb, 2 * S5_STATE), F32)
    return pl.pallas_call(
        _s5_kernel,
        grid=(nb,),
        in_specs=[
            pl.BlockSpec((n, LANE), lambda j: (0, j)),
            per_group(tab), per_group(c2), per_group(cb), per_group(b_re), per_group(b_im),
            whole(rep), whole(til),
            pl.BlockSpec((None, 2, ns2 // 2), lambda j: (j, 0, 0)),
            pl.BlockSpec((1, LANE), lambda j: (0, j)),
        ],
        out_specs=pl.BlockSpec((n, LANE), lambda j: (0, j)),
        out_shape=jax.ShapeDtypeStruct((n, sw), BF16),
        scratch_shapes=[
            table_rows, table_rows, table_rows,
            pltpu.VMEM((gb, tw, tw), BF16),
            pltpu.VMEM((gb, 2 * S5_STATE, tw), BF16),
            pltpu.VMEM((gb, tw, 2 * S5_STATE), BF16),
            pltpu.VMEM((S5_CHUNK, nc, LANE), F32),
            pltpu.VMEM((S5_CHUNK, LANE, nc), BF16),
            pltpu.VMEM((ns2, nc), F32),
            pltpu.VMEM((nc, ns2), F32),
            pltpu.VMEM((nc, ns2), F32),
            pltpu.VMEM((ns2, nc), BF16),
            pltpu.VMEM((S5_CHUNK, LANE, nc), F32),
            pltpu.VMEM((n, LANE), F32),
        ],
        compiler_params=pltpu.CompilerParams(
            dimension_semantics=("parallel",), vmem_limit_bytes=VMEM_LIMIT),
        name="s5_core",
    )(u, tab, c2, cb, b_re, b_im, rep, til, a16, dsk)


def _s5_weights(a_re, a_im, log_dt, b_re, b_im, c_re, c_im, d_skip):
    g, p = a_re.shape
    dt = jnp.exp(log_dt)[:, None]
    decay = jnp.exp(a_re * dt)
    ab_re = decay * jnp.cos(a_im * dt)
    ab_im = decay * jnp.sin(a_im * dt)
    denom = a_re * a_re + a_im * a_im
    num_re = ab_re - 1.0
    num_im = ab_im
    k_re = (num_re * a_re + num_im * a_im) / denom
    k_im = (num_im * a_re - num_re * a_im) / denom
    tab = jnp.concatenate([ab_re, ab_re, -ab_im, ab_im, ab_im, -ab_im, k_re, k_im, k_im, k_re], axis=1)
    r, i = ab_re, ab_im
    for _ in range(S5_CHUNK.bit_length() - 1):
        r, i = r * r - i * i, 2.0 * (r * i)
    gb = GROUPS_PER_BLOCK
    a16 = jnp.stack([r.reshape(g // gb, gb * p), i.reshape(g // gb, gb * p)], axis=1)
    c2 = jnp.concatenate([c_re, -c_im], axis=-1)
    cb = jnp.concatenate([-c_im, -c_re], axis=-1)
    eye = jnp.eye(S5_GROUP_WIDTH, dtype=BF16)
    rep = jnp.repeat(eye, S5_GROUP_WIDTH, axis=1)
    til = jnp.tile(eye, (1, S5_CHUNK))
    return (tab, c2, cb, b_re, b_im, rep, til, a16, d_skip.reshape(1, -1))


def _merge_kernel(x_ref, h_ref, yap_ref, yb_ref, wglu_ref, bglu_ref, wga_ref, wgb_ref,
                  bga_ref, bgb_ref, wa_ref, wb_ref, *rest, n1, n2):
    wo_refs = rest[:n2]
    o_ref, ya_ref, m_ref = rest[n2:]
    j = pl.program_id(1)

    def gated_tile(with_glu):
        if with_glu:
            yp = yap_ref[...]
            z = _dot(yp, wglu_ref[...]) + bglu_ref[...]
            ya_ref[...] = (yp.astype(F32) * jax.nn.sigmoid(z)).astype(BF16)
        h = h_ref[...]
        ga = jax.nn.sigmoid(_dot(h, wga_ref[...].astype(BF16)) + bga_ref[...])
        gb = jax.nn.sigmoid(_dot(h, wgb_ref[...].astype(BF16)) + bgb_ref[...])
        pa = _dot(ya_ref[...], wa_ref[...].astype(BF16))
        pb = _dot(yb_ref[...], wb_ref[...].astype(BF16))
        m_ref[j] = (ga * pa + gb * pb).astype(BF16)

    pl.when(j == 0)(functools.partial(gated_tile, True))
    pl.when(jnp.logical_and(j > 0, j < n1))(functools.partial(gated_tile, False))

    for k in range(n2):
        @pl.when(j == n1 + k)
        def _(k=k):
            m = jnp.concatenate([m_ref[q] for q in range(n1)], axis=1)
            o_ref[...] = x_ref[...] + _dot(m, wo_refs[k][...])


def _merge(x, h, yap, yb, wglu, bglu, wgate, bgate, wa, wb, wo, *, tm=1024, t1=256, t2=512):
    n, d = x.shape
    sw = yap.shape[1]
    n1, n2 = d // t1, d // t2
    nt = n // tm
    second = lambda j: jnp.maximum(j - n1, 0)

    def first(j, back):
        return jnp.where(j < n1, j, jnp.where(j < n1 + back, n1 - 1, 0))

    once = pl.Buffered(1)
    return pl.pallas_call(
        functools.partial(_merge_kernel, n1=n1, n2=n2),
        grid=(nt, n1 + n2),
        in_specs=[
            pl.BlockSpec((tm, t2), lambda i, j: (i, second(j))),
            pl.BlockSpec((tm, d), lambda i, j: (_next_tile_after(i, j, n1, nt), 0)),
            pl.BlockSpec((tm, sw), lambda i, j: (_next_tile_after(i, j, 1, nt), 0)),
            pl.BlockSpec((tm, sw), lambda i, j: (_next_tile_after(i, j, n1, nt), 0)),
            pl.BlockSpec((sw, sw), lambda i, j: (0, 0), pipeline_mode=once),
            pl.BlockSpec((1, sw), lambda i, j: (0, 0)),
            pl.BlockSpec((d, t1), lambda i, j: (0, first(j, 1))),
            pl.BlockSpec((d, t1), lambda i, j: (0, first(j, 2) + n1)),
            pl.BlockSpec((1, t1), lambda i, j: (0, first(j, 1))),
            pl.BlockSpec((1, t1), lambda i, j: (0, first(j, 2) + n1)),
            pl.BlockSpec((sw, t1), lambda i, j: (0, first(j, 3))),
            pl.BlockSpec((sw, t1), lambda i, j: (0, first(j, 3))),
        ] + [pl.BlockSpec((d, t2), lambda i, j, k=k: (0, k), pipeline_mode=once) for k in range(n2)],
        out_specs=pl.BlockSpec((tm, t2), lambda i, j: (i, second(j))),
        out_shape=jax.ShapeDtypeStruct((n, d), F32),
        scratch_shapes=[pltpu.VMEM((tm, sw), BF16), pltpu.VMEM((n1, tm, t1), BF16)],
        compiler_params=pltpu.CompilerParams(
            dimension_semantics=("parallel", "arbitrary"), vmem_limit_bytes=VMEM_LIMIT),
        name="merge",
    )(x, h, yap, yb, wglu, bglu, wgate, wgate, bgate, bgate, wa, wb, *([wo] * n2))


def kernel(x, ffn1_norm, ffn1_w_gate, ffn1_w_up, ffn1_w_down, mix_norm, w_in, s5_a_re, s5_a_im, s5_log_dt, s5_b_re, s5_b_im, s5_c_re, s5_c_im, s5_d, s5_w_glu, s5_b_glu, sgu_ln_g, sgu_ln_b, sgu_w_s, sgu_b_s, w_branch_a, w_branch_b, w_gate, b_gate, w_out, ffn2_norm, ffn2_w_gate, ffn2_w_up, ffn2_w_down, final_norm):
    bsz, seq, d = x.shape
    depth = ffn1_norm.shape[0]
    sw = s5_w_glu.shape[1]
    assert bsz == 1 and sw == w_branch_b.shape[1] and sw % LANE == 0
    assert seq % (S5_CHUNK * LANE) == 0 and sgu_w_s.shape[2] == SGU_CHUNK
    bf = lambda w: w.astype(BF16)
    row = lambda v: v.reshape(1, -1)
    xs = x.reshape(seq, d)
    fin = row(final_norm)
    for i in range(depth):
        xs, h = _ffn(xs, row(ffn1_norm[i]), ffn1_w_gate[i], ffn1_w_up[i], ffn1_w_down[i],
                     row(mix_norm[i]), final=False)
        bsb = jnp.repeat(jnp.transpose(sgu_b_s[i]), sw // SGU_HEADS, axis=1)
        ua, yb = _mix_in(h, bf(w_in[i]), row(sgu_ln_g[i]), row(sgu_ln_b[i]), sgu_w_s[i], bsb, sw=sw)
        yap = _s5_core(ua, *_s5_weights(s5_a_re[i], s5_a_im[i], s5_log_dt[i], s5_b_re[i], s5_b_im[i],
                                        s5_c_re[i], s5_c_im[i], s5_d[i]))
        xs = _merge(xs, h, yap, yb, bf(s5_w_glu[i]), row(s5_b_glu[i]), w_gate[i], row(b_gate[i]),
                    w_branch_a[i], w_branch_b[i], bf(w_out[i]))
        if i == depth - 1:
            xs = _ffn(xs, row(ffn2_norm[i]), ffn2_w_gate[i], ffn2_w_up[i], ffn2_w_down[i], fin, final=True)
        else:
            xs, _ = _ffn(xs, row(ffn2_norm[i]), ffn2_w_gate[i], ffn2_w_up[i], ffn2_w_down[i], fin,
                         final=False)
    return xs.reshape(bsz, seq, d)
```

```python
import functools

import jax
import jax.numpy as jnp
from jax import lax
from jax.experimental import pallas as pl
from jax.experimental.pallas import tpu as pltpu

F32 = jnp.float32
BF16 = jnp.bfloat16

NORM_EPS = 1e-6
S5_GROUP_WIDTH = 16
S5_STATE = 64
SGU_HEADS = 8
SGU_CHUNK = 128
SGU_CAUSAL_BLOCK = 64
S5_CHUNK = 16
LANE = 128
GROUPS_PER_BLOCK = LANE // S5_GROUP_WIDTH
VMEM_LIMIT = 60 * 1024 * 1024
ROW_CHUNK = 512
LN_ROW_BLOCKS = 4


def _rms(x, g):
    ms = jnp.mean(x * x, axis=-1, keepdims=True)
    return (x * lax.rsqrt(ms + NORM_EPS)) * g


def _dot(a, b):
    return jnp.dot(a, b, preferred_element_type=F32)


def _ffn_kernel(x_ref, g_ref, wg_ref, wu_ref, wd_ref, post_ref, o_ref, *rest, final):
    h_ref = rest[0]
    j = pl.program_id(1)
    last = pl.num_programs(1) - 1
    n_chunks = x_ref.shape[0] // ROW_CHUNK

    @pl.when(j == 0)
    def _():
        g = g_ref[...]

        def rows(r, carry):
            rs = pl.ds(pl.multiple_of(r * ROW_CHUNK, ROW_CHUNK), ROW_CHUNK)
            x = x_ref[rs, :]
            h_ref[rs, :] = _rms(x, g).astype(BF16)
            o_ref[rs, :] = x
            return carry

        lax.fori_loop(0, n_chunks, rows, 0, unroll=2)

    def accumulate():
        h = h_ref[...]
        a = _dot(h, wg_ref[...].astype(BF16))
        b = _dot(h, wu_ref[...].astype(BF16))
        hid = (a * jax.nn.sigmoid(a)) * b
        o_ref[...] += _dot(hid.astype(BF16), (0.5 * wd_ref[...]).astype(BF16))

    pl.when(j < last)(accumulate)

    @pl.when(j == last)
    def _():
        accumulate()
        y = _rms(o_ref[...], post_ref[...])
        if final:
            o_ref[...] = y
        else:
            rest[0][...] = y.astype(BF16)


def _next_tile_after(i, j, first_step, n_tiles):
    return jnp.minimum(i + jnp.where(j >= first_step, 1, 0), n_tiles - 1)


def _ffn(x, g, wg, wu, wd, post, *, final, tm=1024, tf=256):
    n, d = x.shape
    dff = wg.shape[1]
    nt = n // tm
    row_tile = pl.BlockSpec((tm, d), lambda i, j: (i, 0))
    if final:
        out_specs, out_shape = row_tile, jax.ShapeDtypeStruct((n, d), F32)
    else:
        out_specs = [row_tile, row_tile]
        out_shape = [jax.ShapeDtypeStruct((n, d), F32), jax.ShapeDtypeStruct((n, d), BF16)]
    return pl.pallas_call(
        functools.partial(_ffn_kernel, final=final),
        grid=(nt, dff // tf),
        in_specs=[
            row_tile,
            pl.BlockSpec((1, d), lambda i, j: (0, 0)),
            pl.BlockSpec((d, tf), lambda i, j: (0, j)),
            pl.BlockSpec((d, tf), lambda i, j: (0, j)),
            pl.BlockSpec((tf, d), lambda i, j: (j, 0)),
            pl.BlockSpec((1, d), lambda i, j: (0, 0)),
        ],
        out_specs=out_specs,
        out_shape=out_shape,
        scratch_shapes=[pltpu.VMEM((tm, d), BF16)] if final else [],
        compiler_params=pltpu.CompilerParams(
            dimension_semantics=("parallel", "arbitrary"), vmem_limit_bytes=VMEM_LIMIT),
        name="ffn_final" if final else "ffn",
    )(x, g, wg, wu, wd, post)


def _mix_in_kernel(h_ref, w_ref, lng_ref, lnb_ref, ws_ref, bs_ref, ua_ref, yb_ref, vn_ref, *, sw):
    j = pl.program_id(1)

    @pl.when(j == 0)
    def _():
        ua_ref[...] = _dot(h_ref[...], w_ref[...])

    @pl.when(j == 1)
    def _():
        rb = vn_ref.shape[0] // LN_ROW_BLOCKS
        for k in range(LN_ROW_BLOCKS):
            rows = slice(k * rb, (k + 1) * rb)
            v = jax.nn.gelu(_dot(h_ref[rows, :], w_ref[...]))
            mu = jnp.mean(v, axis=-1, keepdims=True)
            vc = v - mu
            var = jnp.mean(vc * vc, axis=-1, keepdims=True)
            vn_ref[rows, :] = ((vc * lax.rsqrt(var + NORM_EPS)) * lng_ref[...] + lnb_ref[...]).astype(BF16)

    @pl.when(j == 2)
    def _():
        tm = vn_ref.shape[0]
        u = jax.nn.gelu(_dot(h_ref[...], w_ref[...]))
        r = lax.broadcasted_iota(jnp.int32, (SGU_CHUNK, SGU_CHUNK), 0) // SGU_CAUSAL_BLOCK
        c = lax.broadcasted_iota(jnp.int32, (SGU_CHUNK, SGU_CHUNK), 1) // SGU_CAUSAL_BLOCK
        keep = r >= c
        hd = sw // SGU_HEADS
        for h in range(SGU_HEADS):
            wsm = jnp.where(keep, ws_ref[h], 0.0).astype(BF16)
            cols = slice(h * hd, (h + 1) * hd)
            for q in range(0, tm // SGU_CHUNK, 2):
                ra = slice(q * SGU_CHUNK, (q + 1) * SGU_CHUNK)
                rb = slice((q + 1) * SGU_CHUNK, (q + 2) * SGU_CHUNK)
                both = _dot(wsm, jnp.concatenate([vn_ref[ra, cols], vn_ref[rb, cols]], axis=1))
                yb_ref[ra, cols] = (u[ra, cols] * (both[:, :hd] + bs_ref[:, cols])).astype(BF16)
                yb_ref[rb, cols] = (u[rb, cols] * (both[:, hd:] + bs_ref[:, cols])).astype(BF16)


def _mix_in(h, w_in, ln_g, ln_b, w_s, bsb, *, sw, tm=1024):
    n, d = h.shape
    assert w_in.shape[1] == 3 * sw
    nt = n // tm
    return pl.pallas_call(
        functools.partial(_mix_in_kernel, sw=sw),
        grid=(nt, 3),
        in_specs=[
            pl.BlockSpec((tm, d), lambda i, j: (i, 0)),
            pl.BlockSpec((d, sw), lambda i, j: (0, jnp.where(j == 0, 0, 3 - j))),
            pl.BlockSpec((1, sw), lambda i, j: (0, 0)),
            pl.BlockSpec((1, sw), lambda i, j: (0, 0)),
            pl.BlockSpec(w_s.shape, lambda i, j: (0, 0, 0)),
            pl.BlockSpec(bsb.shape, lambda i, j: (0, 0)),
        ],
        out_specs=[
            pl.BlockSpec((tm, sw), lambda i, j: (i, 0)),
            pl.BlockSpec((tm, sw), lambda i, j: (i, 0)),
        ],
        out_shape=[
            jax.ShapeDtypeStruct((n, sw), F32),
            jax.ShapeDtypeStruct((n, sw), BF16),
        ],
        scratch_shapes=[pltpu.VMEM((tm, sw), BF16)],
        compiler_params=pltpu.CompilerParams(
            dimension_semantics=("parallel", "arbitrary"), vmem_limit_bytes=VMEM_LIMIT),
        name="mix_in",
    )(h, w_in, ln_g, ln_b, w_s, bsb)


def _dot_split(a, b):
    a_hi = a.astype(BF16)
    b_hi = b.astype(BF16)
    a_lo = (a - a_hi.astype(F32)).astype(BF16)
    b_lo = (b - b_hi.astype(F32)).astype(BF16)
    return _dot(a_hi, b_hi) + (_dot(a_hi, b_lo) + _dot(a_lo, b_hi))


def _expand(x, sel):
    x1 = x.astype(BF16)
    r1 = x - x1.astype(F32)
    x2 = r1.astype(BF16)
    x3 = (r1 - x2.astype(F32)).astype(BF16)
    return _dot(x1, sel) + (_dot(x2, sel) + _dot(x3, sel))


def _expand_t(x, sel):
    tn = (((0,), (0,)), ((), ()))
    x1 = x.astype(BF16)
    r1 = x - x1.astype(F32)
    x2 = r1.astype(BF16)
    x3 = (r1 - x2.astype(F32)).astype(BF16)
    dot_t = lambda v: lax.dot_general(v, sel, tn, preferred_element_type=F32)
    return dot_t(x1) + (dot_t(x2) + dot_t(x3))


def _s5_kernel(u_ref, tab_ref, c2_ref, cb_ref, bre_ref, bim_ref, rep_ref, til_ref, a_ref, d_ref, o_ref,
               dn_ref, upr_ref, upi_ref, wt_ref, wb_ref, wc_ref,
               up_ref, ut_ref, vt_ref, v_ref, xs_ref, xt_ref, yt_ref, yn_ref):
    nc = u_ref.shape[0] // S5_CHUNK
    gw = S5_GROUP_WIDTH
    gb = GROUPS_PER_BLOCK
    tw = S5_CHUNK * gw
    ns = gb * S5_STATE
    sl = 2 * S5_STATE

    tab = tab_ref[...]
    ar2, an, ap = tab[:, 0:sl], tab[:, sl:2 * sl], tab[:, 2 * sl:3 * sl]
    q, qs = tab[:, 3 * sl:4 * sl], tab[:, 4 * sl:5 * sl]
    lo = lax.broadcasted_iota(jnp.int32, (gb, sl), 1) < S5_STATE
    x = jnp.where(lo, 1.0, 0.0)
    xs = jnp.where(lo, 0.0, 1.0)
    dn_ref[pl.ds((S5_CHUNK - 1) * gb, gb), :] = q
    for j in range(1, S5_CHUNK + 1):
        x, xs = x * ar2 + xs * an, xs * ar2 + x * ap
        upr_ref[pl.ds((j - 1) * gb, gb), :] = jnp.where(lo, x, xs)
        upi_ref[pl.ds((j - 1) * gb, gb), :] = jnp.where(lo, xs, x)
        if j < S5_CHUNK:
            q, qs = q * ar2 + qs * an, qs * ar2 + q * ap
            dn_ref[pl.ds((S5_CHUNK - 1 - j) * gb, gb), :] = q

    lane = lax.broadcasted_iota(jnp.int32, (gw, tw), 1)
    for g in range(gb):
        pw = _expand_t(dn_ref[pl.ds(g, S5_CHUNK, stride=gb), :], rep_ref[...])
        br = _expand(bre_ref[g], til_ref[...])
        bi = _expand(bim_ref[g], til_ref[...])
        pr, pi = pw[:S5_STATE], pw[S5_STATE:]
        wbf = jnp.concatenate([pr * br - pi * bi, pr * bi + pi * br], axis=0)
        wb_ref[g] = wbf.astype(BF16)
        hrow = _dot_split(c2_ref[g], wbf)
        ca, cb = c2_ref[g], cb_ref[g]
        for t in range(S5_CHUNK):
            shift = (gw * (t + 1)) % tw
            rolled = pltpu.roll(hrow, shift, axis=1) if shift else hrow
            wt_ref[g, t * gw:(t + 1) * gw, :] = jnp.where(lane < gw * (t + 1), rolled, 0.0).astype(BF16)
            row = t * gb + g
            wc_ref[g, t * gw:(t + 1) * gw, :] = (
                ca * upr_ref[row:row + 1, :] + cb * upi_ref[row:row + 1, :]).astype(BF16)

    for s in range(S5_CHUNK):
        piece = u_ref[pl.ds(s, nc, stride=S5_CHUNK), :]
        up_ref[s] = piece
        ut_ref[s] = piece.T.astype(BF16)

    def chunk_inputs(g):
        return jnp.concatenate(
            [ut_ref[s, g * gw:(g + 1) * gw, :] for s in range(S5_CHUNK)], axis=0)

    for g in range(GROUPS_PER_BLOCK):
        ug = chunk_inputs(g)
        vg = _dot(wb_ref[g], ug)
        vt_ref[g * S5_STATE:(g + 1) * S5_STATE, :] = vg[:S5_STATE]
        vt_ref[ns + g * S5_STATE:ns + (g + 1) * S5_STATE, :] = vg[S5_STATE:]
        yg = _dot(wt_ref[g], ug)
        for t in range(S5_CHUNK):
            yt_ref[t, g * gw:(g + 1) * gw, :] = yg[t * gw:(t + 1) * gw, :]
    v_ref[...] = vt_ref[...].T

    ar = a_ref[0:1, :]
    ai = a_ref[1:2, :]

    def step(c, carry):
        xr, xi = carry
        xs_ref[pl.ds(c, 1), 0:ns] = xr
        xs_ref[pl.ds(c, 1), ns:2 * ns] = xi
        vr = v_ref[pl.ds(c, 1), 0:ns]
        vi = v_ref[pl.ds(c, 1), ns:2 * ns]
        return ar * xr - ai * xi + vr, ar * xi + ai * xr + vi

    zero = jnp.zeros((1, ns), F32)
    lax.fori_loop(0, nc, step, (zero, zero), unroll=True)
    xt_ref[...] = xs_ref[...].T.astype(BF16)

    for g in range(GROUPS_PER_BLOCK):
        xg = jnp.concatenate(
            [xt_ref[g * S5_STATE:(g + 1) * S5_STATE, :],
             xt_ref[ns + g * S5_STATE:ns + (g + 1) * S5_STATE, :]], axis=0)
        yg = _dot(wc_ref[g], xg)
        for t in range(S5_CHUNK):
            yt_ref[t, g * gw:(g + 1) * gw, :] += yg[t * gw:(t + 1) * gw, :]

    for t in range(S5_CHUNK):
        y = yt_ref[t].T + d_ref[...] * up_ref[t]
        yn_ref[pl.ds(t, nc, stride=S5_CHUNK), :] = jax.nn.gelu(y)
    o_ref[...] = yn_ref[...].astype(BF16)


def _s5_core(u, tab, c2, cb, b_re, b_im, rep, til, a16, dsk):
    n, sw = u.shape
    nb = sw // LANE
    nc = n // S5_CHUNK
    gb = GROUPS_PER_BLOCK
    ns2 = 2 * gb * S5_STATE
    tw = S5_CHUNK * S5_GROUP_WIDTH
    per_group = lambda a: pl.BlockSpec((gb,) + a.shape[1:], lambda j: (j,) + (0,) * (a.ndim - 1))
    whole = lambda a: pl.BlockSpec(a.shape, lambda j: (0, 0))
    table_rows = pltpu.VMEM((S5_CHUNK * gb, 2 * S5_STATE), F32)
    return pl.pallas_call(
        _s5_kernel,
        grid=(nb,),
        in_specs=[
            pl.BlockSpec((n, LANE), lambda j: (0, j)),
            per_group(tab), per_group(c2), per_group(cb), per_group(b_re), per_group(b_im),
            whole(rep), whole(til),
            pl.BlockSpec((None, 2, ns2 // 2), lambda j: (j, 0, 0)),
            pl.BlockSpec((1, LANE), lambda j: (0, j)),
        ],
        out_specs=pl.BlockSpec((n, LANE), lambda j: (0, j)),
        out_shape=jax.ShapeDtypeStruct((n, sw), BF16),
        scratch_shapes=[
            table_rows, table_rows, table_rows,
            pltpu.VMEM((gb, tw, tw), BF16),
            pltpu.VMEM((gb, 2 * S5_STATE, tw), BF16),
            pltpu.VMEM((gb, tw, 2 * S5_STATE), BF16),
            pltpu.VMEM((S5_CHUNK, nc, LANE), F32),
            pltpu.VMEM((S5_CHUNK, LANE, nc), BF16),
            pltpu.VMEM((ns2, nc), F32),
            pltpu.VMEM((nc, ns2), F32),
            pltpu.VMEM((nc, ns2), F32),
            pltpu.VMEM((ns2, nc), BF16),
            pltpu.VMEM((S5_CHUNK, LANE, nc), F32),
            pltpu.VMEM((n, LANE), F32),
        ],
        compiler_params=pltpu.CompilerParams(
            dimension_semantics=("parallel",), vmem_limit_bytes=VMEM_LIMIT),
        name="s5_core",
    )(u, tab, c2, cb, b_re, b_im, rep, til, a16, dsk)


def _s5_weights(a_re, a_im, log_dt, b_re, b_im, c_re, c_im, d_skip):
    g, p = a_re.shape
    dt = jnp.exp(log_dt)[:, None]
    decay = jnp.exp(a_re * dt)
    ab_re = decay * jnp.cos(a_im * dt)
    ab_im = decay * jnp.sin(a_im * dt)
    denom = a_re * a_re + a_im * a_im
    num_re = ab_re - 1.0
    num_im = ab_im
    k_re = (num_re * a_re + num_im * a_im) / denom
    k_im = (num_im * a_re - num_re * a_im) / denom
    tab = jnp.concatenate([ab_re, ab_re, -ab_im, ab_im, ab_im, -ab_im, k_re, k_im, k_im, k_re], axis=1)
    r, i = ab_re, ab_im
    for _ in range(S5_CHUNK.bit_length() - 1):
        r, i = r * r - i * i, 2.0 * (r * i)
    gb = GROUPS_PER_BLOCK
    a16 = jnp.stack([r.reshape(g // gb, gb * p), i.reshape(g // gb, gb * p)], axis=1)
    c2 = jnp.concatenate([c_re, -c_im], axis=-1)
    cb = jnp.concatenate([-c_im, -c_re], axis=-1)
    eye = jnp.eye(S5_GROUP_WIDTH, dtype=BF16)
    rep = jnp.repeat(eye, S5_GROUP_WIDTH, axis=1)
    til = jnp.tile(eye, (1, S5_CHUNK))
    return (tab, c2, cb, b_re, b_im, rep, til, a16, d_skip.reshape(1, -1))


def _merge_kernel(x_ref, h_ref, yap_ref, yb_ref, wglu_ref, bglu_ref, wga_ref, wgb_ref,
                  bga_ref, bgb_ref, wa_ref, wb_ref, *rest, n1, n2):
    wo_refs = rest[:n2]
    o_ref, ya_ref, m_ref = rest[n2:]
    j = pl.program_id(1)

    def gated_tile(with_glu):
        if with_glu:
            yp = yap_ref[...]
            z = _dot(yp, wglu_ref[...]) + bglu_ref[...]
            ya_ref[...] = (yp.astype(F32) * jax.nn.sigmoid(z)).astype(BF16)
        h = h_ref[...]
        ga = jax.nn.sigmoid(_dot(h, wga_ref[...].astype(BF16)) + bga_ref[...])
        gb = jax.nn.sigmoid(_dot(h, wgb_ref[...].astype(BF16)) + bgb_ref[...])
        pa = _dot(ya_ref[...], wa_ref[...].astype(BF16))
        pb = _dot(yb_ref[...], wb_ref[...].astype(BF16))
        m_ref[j] = (ga * pa + gb * pb).astype(BF16)

    pl.when(j == 0)(functools.partial(gated_tile, True))
    pl.when(jnp.logical_and(j > 0, j < n1))(functools.partial(gated_tile, False))

    for k in range(n2):
        @pl.when(j == n1 + k)
        def _(k=k):
            m = jnp.concatenate([m_ref[q] for q in range(n1)], axis=1)
            o_ref[...] = x_ref[...] + _dot(m, wo_refs[k][...])


def _merge(x, h, yap, yb, wglu, bglu, wgate, bgate, wa, wb, wo, *, tm=1024, t1=256, t2=512):
    n, d = x.shape
    sw = yap.shape[1]
    n1, n2 = d // t1, d // t2
    nt = n // tm
    second = lambda j: jnp.maximum(j - n1, 0)

    def first(j, back):
        return jnp.where(j < n1, j, jnp.where(j < n1 + back, n1 - 1, 0))

    once = pl.Buffered(1)
    return pl.pallas_call(
        functools.partial(_merge_kernel, n1=n1, n2=n2),
        grid=(nt, n1 + n2),
        in_specs=[
            pl.BlockSpec((tm, t2), lambda i, j: (i, second(j))),
            pl.BlockSpec((tm, d), lambda i, j: (_next_tile_after(i, j, n1, nt), 0)),
            pl.BlockSpec((tm, sw), lambda i, j: (_next_tile_after(i, j, 1, nt), 0)),
            pl.BlockSpec((tm, sw), lambda i, j: (_next_tile_after(i, j, n1, nt), 0)),
            pl.BlockSpec((sw, sw), lambda i, j: (0, 0), pipeline_mode=once),
            pl.BlockSpec((1, sw), lambda i, j: (0, 0)),
            pl.BlockSpec((d, t1), lambda i, j: (0, first(j, 1))),
            pl.BlockSpec((d, t1), lambda i, j: (0, first(j, 2) + n1)),
            pl.BlockSpec((1, t1), lambda i, j: (0, first(j, 1))),
            pl.BlockSpec((1, t1), lambda i, j: (0, first(j, 2) + n1)),
            pl.BlockSpec((sw, t1), lambda i, j: (0, first(j, 3))),
            pl.BlockSpec((sw, t1), lambda i, j: (0, first(j, 3))),
        ] + [pl.BlockSpec((d, t2), lambda i, j, k=k: (0, k), pipeline_mode=once) for k in range(n2)],
        out_specs=pl.BlockSpec((tm, t2), lambda i, j: (i, second(j))),
        out_shape=jax.ShapeDtypeStruct((n, d), F32),
        scratch_shapes=[pltpu.VMEM((tm, sw), BF16), pltpu.VMEM((n1, tm, t1), BF16)],
        compiler_params=pltpu.CompilerParams(
            dimension_semantics=("parallel", "arbitrary"), vmem_limit_bytes=VMEM_LIMIT),
        name="merge",
    )(x, h, yap, yb, wglu, bglu, wgate, wgate, bgate, bgate, wa, wb, *([wo] * n2))


def kernel(x, ffn1_norm, ffn1_w_gate, ffn1_w_up, ffn1_w_down, mix_norm, w_in, s5_a_re, s5_a_im, s5_log_dt, s5_b_re, s5_b_im, s5_c_re, s5_c_im, s5_d, s5_w_glu, s5_b_glu, sgu_ln_g, sgu_ln_b, sgu_w_s, sgu_b_s, w_branch_a, w_branch_b, w_gate, b_gate, w_out, ffn2_norm, ffn2_w_gate, ffn2_w_up, ffn2_w_down, final_norm):
    bsz, seq, d = x.shape
    depth = ffn1_norm.shape[0]
    sw = s5_w_glu.shape[1]
    assert bsz == 1 and sw == w_branch_b.shape[1] and sw % LANE == 0
    assert seq % (S5_CHUNK * LANE) == 0 and sgu_w_s.shape[2] == SGU_CHUNK
    bf = lambda w: w.astype(BF16)
    row = lambda v: v.reshape(1, -1)
    xs = x.reshape(seq, d)
    fin = row(final_norm)
    for i in range(depth):
        xs, h = _ffn(xs, row(ffn1_norm[i]), ffn1_w_gate[i], ffn1_w_up[i], ffn1_w_down[i],
                     row(mix_norm[i]), final=False)
        bsb = jnp.repeat(jnp.transpose(sgu_b_s[i]), sw // SGU_HEADS, axis=1)
        ua, yb = _mix_in(h, bf(w_in[i]), row(sgu_ln_g[i]), row(sgu_ln_b[i]), sgu_w_s[i], bsb, sw=sw)
        yap = _s5_core(ua, *_s5_weights(s5_a_re[i], s5_a_im[i], s5_log_dt[i], s5_b_re[i], s5_b_im[i],
                                        s5_c_re[i], s5_c_im[i], s5_d[i]))
        xs = _merge(xs, h, yap, yb, bf(s5_w_glu[i]), row(s5_b_glu[i]), w_gate[i], row(b_gate[i]),
                    w_branch_a[i], w_branch_b[i], bf(w_out[i]))
        if i == depth - 1:
            xs = _ffn(xs, row(ffn2_norm[i]), ffn2_w_gate[i], ffn2_w_up[i], ffn2_w_down[i], fin, final=True)
        else:
            xs, _ = _ffn(xs, row(ffn2_norm[i]), ffn2_w_gate[i], ffn2_w_up[i], ffn2_w_down[i], fin,
                         final=False)
    return xs.reshape(bsz, seq, d)
```

```python
import functools

import jax
import jax.numpy as jnp
from jax import lax
from jax.experimental import pallas as pl
from jax.experimental.pallas import tpu as pltpu

F32 = jnp.float32
BF16 = jnp.bfloat16

NORM_EPS = 1e-6
S5_GROUP_WIDTH = 16
S5_STATE = 64
SGU_HEADS = 8
SGU_CHUNK = 128
SGU_CAUSAL_BLOCK = 64
S5_CHUNK = 16
LANE = 128
GROUPS_PER_BLOCK = LANE // S5_GROUP_WIDTH
VMEM_LIMIT = 60 * 1024 * 1024
ROW_CHUNK = 512
LN_ROW_BLOCKS = 4


def _rms(x, g):
    ms = jnp.mean(x * x, axis=-1, keepdims=True)
    return (x * lax.rsqrt(ms + NORM_EPS)) * g


def _dot(a, b):
    return jnp.dot(a, b, preferred_element_type=F32)


def _ffn_kernel(x_ref, g_ref, wg_ref, wu_ref, wd_ref, post_ref, o_ref, *rest, final):
    h_ref = rest[0]
    j = pl.program_id(1)
    last = pl.num_programs(1) - 1
    n_chunks = x_ref.shape[0] // ROW_CHUNK

    @pl.when(j == 0)
    def _():
        g = g_ref[...]
        wg = wg_ref[...].astype(BF16)
        wu = wu_ref[...].astype(BF16)
        wd = (0.5 * wd_ref[...]).astype(BF16)
        for r in range(n_chunks):
            rs = slice(r * ROW_CHUNK, (r + 1) * ROW_CHUNK)
            x = x_ref[rs, :]
            h = _rms(x, g).astype(BF16)
            h_ref[rs, :] = h
            a = _dot(h, wg)
            b = _dot(h, wu)
            hid = (a * jax.nn.sigmoid(a)) * b
            o_ref[rs, :] = x + _dot(hid.astype(BF16), wd)

    def accumulate():
        h = h_ref[...]
        a = _dot(h, wg_ref[...].astype(BF16))
        b = _dot(h, wu_ref[...].astype(BF16))
        hid = (a * jax.nn.sigmoid(a)) * b
        o_ref[...] += _dot(hid.astype(BF16), (0.5 * wd_ref[...]).astype(BF16))

    pl.when(jnp.logical_and(j > 0, j < last))(accumulate)

    @pl.when(j == last)
    def _():
        accumulate()
        y = _rms(o_ref[...], post_ref[...])
        if final:
            o_ref[...] = y
        else:
            rest[0][...] = y.astype(BF16)


def _next_tile_after(i, j, first_step, n_tiles):
    return jnp.minimum(i + jnp.where(j >= first_step, 1, 0), n_tiles - 1)


def _ffn(x, g, wg, wu, wd, post, *, final, tm=1024, tf=256):
    n, d = x.shape
    dff = wg.shape[1]
    nt = n // tm
    row_tile = pl.BlockSpec((tm, d), lambda i, j: (i, 0))
    if final:
        out_specs, out_shape = row_tile, jax.ShapeDtypeStruct((n, d), F32)
    else:
        out_specs = [row_tile, row_tile]
        out_shape = [jax.ShapeDtypeStruct((n, d), F32), jax.ShapeDtypeStruct((n, d), BF16)]
    return pl.pallas_call(
        functools.partial(_ffn_kernel, final=final),
        grid=(nt, dff // tf),
        in_specs=[
            row_tile,
            pl.BlockSpec((1, d), lambda i, j: (0, 0)),
            pl.BlockSpec((d, tf), lambda i, j: (0, j)),
            pl.BlockSpec((d, tf), lambda i, j: (0, j)),
            pl.BlockSpec((tf, d), lambda i, j: (j, 0)),
            pl.BlockSpec((1, d), lambda i, j: (0, 0)),
        ],
        out_specs=out_specs,
        out_shape=out_shape,
        scratch_shapes=[pltpu.VMEM((tm, d), BF16)] if final else [],
        compiler_params=pltpu.CompilerParams(
            dimension_semantics=("parallel", "arbitrary"), vmem_limit_bytes=VMEM_LIMIT),
        name="ffn_final" if final else "ffn",
    )(x, g, wg, wu, wd, post)


def _mix_in_kernel(h_ref, w_ref, lng_ref, lnb_ref, ws_ref, bs_ref, ua_ref, yb_ref, vn_ref, *, sw):
    j = pl.program_id(1)

    @pl.when(j == 0)
    def _():
        ua_ref[...] = _dot(h_ref[...], w_ref[...])

    @pl.when(j == 1)
    def _():
        rb = vn_ref.shape[0] // LN_ROW_BLOCKS
        for k in range(LN_ROW_BLOCKS):
            rows = slice(k * rb, (k + 1) * rb)
            v = jax.nn.gelu(_dot(h_ref[rows, :], w_ref[...]))
            mu = jnp.mean(v, axis=-1, keepdims=True)
            vc = v - mu
            var = jnp.mean(vc * vc, axis=-1, keepdims=True)
            vn_ref[rows, :] = ((vc * lax.rsqrt(var + NORM_EPS)) * lng_ref[...] + lnb_ref[...]).astype(BF16)

    @pl.when(j == 2)
    def _():
        tm = vn_ref.shape[0]
        u = jax.nn.gelu(_dot(h_ref[...], w_ref[...]))
        r = lax.broadcasted_iota(jnp.int32, (SGU_CHUNK, SGU_CHUNK), 0) // SGU_CAUSAL_BLOCK
        c = lax.broadcasted_iota(jnp.int32, (SGU_CHUNK, SGU_CHUNK), 1) // SGU_CAUSAL_BLOCK
        keep = r >= c
        hd = sw // SGU_HEADS
        for h in range(SGU_HEADS):
            wsm = jnp.where(keep, ws_ref[h], 0.0).astype(BF16)
            cols = slice(h * hd, (h + 1) * hd)
            for q in range(0, tm // SGU_CHUNK, 2):
                ra = slice(q * SGU_CHUNK, (q + 1) * SGU_CHUNK)
                rb = slice((q + 1) * SGU_CHUNK, (q + 2) * SGU_CHUNK)
                both = _dot(wsm, jnp.concatenate([vn_ref[ra, cols], vn_ref[rb, cols]], axis=1))
                yb_ref[ra, cols] = (u[ra, cols] * (both[:, :hd] + bs_ref[:, cols])).astype(BF16)
                yb_ref[rb, cols] = (u[rb, cols] * (both[:, hd:] + bs_ref[:, cols])).astype(BF16)


def _mix_in(h, w_in, ln_g, ln_b, w_s, bsb, *, sw, tm=1024):
    n, d = h.shape
    assert w_in.shape[1] == 3 * sw
    nt = n // tm
    return pl.pallas_call(
        functools.partial(_mix_in_kernel, sw=sw),
        grid=(nt, 3),
        in_specs=[
            pl.BlockSpec((tm, d), lambda i, j: (i, 0)),
            pl.BlockSpec((d, sw), lambda i, j: (0, jnp.where(j == 0, 0, 3 - j))),
            pl.BlockSpec((1, sw), lambda i, j: (0, 0)),
            pl.BlockSpec((1, sw), lambda i, j: (0, 0)),
            pl.BlockSpec(w_s.shape, lambda i, j: (0, 0, 0)),
            pl.BlockSpec(bsb.shape, lambda i, j: (0, 0)),
        ],
        out_specs=[
            pl.BlockSpec((tm, sw), lambda i, j: (i, 0)),
            pl.BlockSpec((tm, sw), lambda i, j: (i, 0)),
        ],
        out_shape=[
            jax.ShapeDtypeStruct((n, sw), F32),
            jax.ShapeDtypeStruct((n, sw), BF16),
        ],
        scratch_shapes=[pltpu.VMEM((tm, sw), BF16)],
        compiler_params=pltpu.CompilerParams(
            dimension_semantics=("parallel", "arbitrary"), vmem_limit_bytes=VMEM_LIMIT),
        name="mix_in",
    )(h, w_in, ln_g, ln_b, w_s, bsb)


def _dot_split(a, b):
    a_hi = a.astype(BF16)
    b_hi = b.astype(BF16)
    a_lo = (a - a_hi.astype(F32)).astype(BF16)
    b_lo = (b - b_hi.astype(F32)).astype(BF16)
    return _dot(a_hi, b_hi) + (_dot(a_hi, b_lo) + _dot(a_lo, b_hi))


def _expand(x, sel):
    x1 = x.astype(BF16)
    r1 = x - x1.astype(F32)
    x2 = r1.astype(BF16)
    x3 = (r1 - x2.astype(F32)).astype(BF16)
    return _dot(x1, sel) + (_dot(x2, sel) + _dot(x3, sel))


def _expand_t(x, sel):
    tn = (((0,), (0,)), ((), ()))
    x1 = x.astype(BF16)
    r1 = x - x1.astype(F32)
    x2 = r1.astype(BF16)
    x3 = (r1 - x2.astype(F32)).astype(BF16)
    dot_t = lambda v: lax.dot_general(v, sel, tn, preferred_element_type=F32)
    return dot_t(x1) + (dot_t(x2) + dot_t(x3))


def _s5_kernel(u_ref, tab_ref, c2_ref, cb_ref, bre_ref, bim_ref, rep_ref, til_ref, a_ref, d_ref, o_ref,
               dn_ref, upr_ref, upi_ref, wt_ref, wb_ref, wc_ref,
               up_ref, ut_ref, vt_ref, v_ref, xs_ref, xt_ref, yt_ref, yn_ref):
    nc = u_ref.shape[0] // S5_CHUNK
    gw = S5_GROUP_WIDTH
    gb = GROUPS_PER_BLOCK
    tw = S5_CHUNK * gw
    ns = gb * S5_STATE
    sl = 2 * S5_STATE

    tab = tab_ref[...]
    ar2, an, ap = tab[:, 0:sl], tab[:, sl:2 * sl], tab[:, 2 * sl:3 * sl]
    q, qs = tab[:, 3 * sl:4 * sl], tab[:, 4 * sl:5 * sl]
    lo = lax.broadcasted_iota(jnp.int32, (gb, sl), 1) < S5_STATE
    x = jnp.where(lo, 1.0, 0.0)
    xs = jnp.where(lo, 0.0, 1.0)
    dn_ref[pl.ds((S5_CHUNK - 1) * gb, gb), :] = q
    for j in range(1, S5_CHUNK + 1):
        x, xs = x * ar2 + xs * an, xs * ar2 + x * ap
        upr_ref[pl.ds((j - 1) * gb, gb), :] = jnp.where(lo, x, xs)
        upi_ref[pl.ds((j - 1) * gb, gb), :] = jnp.where(lo, xs, x)
        if j < S5_CHUNK:
            q, qs = q * ar2 + qs * an, qs * ar2 + q * ap
            dn_ref[pl.ds((S5_CHUNK - 1 - j) * gb, gb), :] = q

    lane = lax.broadcasted_iota(jnp.int32, (gw, tw), 1)
    for g in range(gb):
        pw = _expand_t(dn_ref[pl.ds(g, S5_CHUNK, stride=gb), :], rep_ref[...])
        br = _expand(bre_ref[g], til_ref[...])
        bi = _expand(bim_ref[g], til_ref[...])
        pr, pi = pw[:S5_STATE], pw[S5_STATE:]
        wbf = jnp.concatenate([pr * br - pi * bi, pr * bi + pi * br], axis=0)
        wb_ref[g] = wbf.astype(BF16)
        hrow = _dot_split(c2_ref[g], wbf)
        ca, cb = c2_ref[g], cb_ref[g]
        for t in range(S5_CHUNK):
            shift = (gw * (t + 1)) % tw
            rolled = pltpu.roll(hrow, shift, axis=1) if shift else hrow
            wt_ref[g, t * gw:(t + 1) * gw, :] = jnp.where(lane < gw * (t + 1), rolled, 0.0).astype(BF16)
            row = t * gb + g
            wc_ref[g, t * gw:(t + 1) * gw, :] = (
                ca * upr_ref[row:row + 1, :] + cb * upi_ref[row:row + 1, :]).astype(BF16)

    for s in range(S5_CHUNK):
        piece = u_ref[pl.ds(s, nc, stride=S5_CHUNK), :]
        up_ref[s] = piece
        ut_ref[s] = piece.T.astype(BF16)

    def chunk_inputs(g):
        return jnp.concatenate(
            [ut_ref[s, g * gw:(g + 1) * gw, :] for s in range(S5_CHUNK)], axis=0)

    for g in range(GROUPS_PER_BLOCK):
        ug = chunk_inputs(g)
        vg = _dot(wb_ref[g], ug)
        vt_ref[g * S5_STATE:(g + 1) * S5_STATE, :] = vg[:S5_STATE]
        vt_ref[ns + g * S5_STATE:ns + (g + 1) * S5_STATE, :] = vg[S5_STATE:]
        yg = _dot(wt_ref[g], ug)
        for t in range(S5_CHUNK):
            yt_ref[t, g * gw:(g + 1) * gw, :] = yg[t * gw:(t + 1) * gw, :]
    v_ref[...] = vt_ref[...].T

    ar = a_ref[0:1, :]
    ai = a_ref[1:2, :]

    def step(c, carry):
        xr, xi = carry
        xs_ref[pl.ds(c, 1), 0:ns] = xr
        xs_ref[pl.ds(c, 1), ns:2 * ns] = xi
        vr = v_ref[pl.ds(c, 1), 0:ns]
        vi = v_ref[pl.ds(c, 1), ns:2 * ns]
        return ar * xr - ai * xi + vr, ar * xi + ai * xr + vi

    zero = jnp.zeros((1, ns), F32)
    lax.fori_loop(0, nc, step, (zero, zero), unroll=True)
    xt_ref[...] = xs_ref[...].T.astype(BF16)

    for g in range(GROUPS_PER_BLOCK):
        xg = jnp.concatenate(
            [xt_ref[g * S5_STATE:(g + 1) * S5_STATE, :],
             xt_ref[ns + g * S5_STATE:ns + (g + 1) * S5_STATE, :]], axis=0)
        yg = _dot(wc_ref[g], xg)
        for t in range(S5_CHUNK):
            yt_ref[t, g * gw:(g + 1) * gw, :] += yg[t * gw:(t + 1) * gw, :]

    for t in range(S5_CHUNK):
        y = yt_ref[t].T + d_ref[...] * up_ref[t]
        yn_ref[pl.ds(t, nc, stride=S5_CHUNK), :] = jax.nn.gelu(y)
    o_ref[...] = yn_ref[...].astype(BF16)


def _s5_core(u, tab, c2, cb, b_re, b_im, rep, til, a16, dsk):
    n, sw = u.shape
    nb = sw // LANE
    nc = n // S5_CHUNK
    gb = GROUPS_PER_BLOCK
    ns2 = 2 * gb * S5_STATE
    tw = S5_CHUNK * S5_GROUP_WIDTH
    per_group = lambda a: pl.BlockSpec((gb,) + a.shape[1:], lambda j: (j,) + (0,) * (a.ndim - 1))
    whole = lambda a: pl.BlockSpec(a.shape, lambda j: (0, 0))
    table_rows = pltpu.VMEM((S5_CHUNK * gb, 2 * S5_STATE), F32)
    return pl.pallas_call(
        _s5_kernel,
        grid=(nb,),
        in_specs=[
            pl.BlockSpec((n, LANE), lambda j: (0, j)),
            per_group(tab), per_group(c2), per_group(cb), per_group(b_re), per_group(b_im),
            whole(rep), whole(til),
            pl.BlockSpec((None, 2, ns2 // 2), lambda j: (j, 0, 0)),
            pl.BlockSpec((1, LANE), lambda j: (0, j)),
        ],
        out_specs=pl.BlockSpec((n, LANE), lambda j: (0, j)),
        out_shape=jax.ShapeDtypeStruct((n, sw), BF16),
        scratch_shapes=[
            table_rows, table_rows, table_rows,
            pltpu.VMEM((gb, tw, tw), BF16),
            pltpu.VMEM((gb, 2 * S5_STATE, tw), BF16),
            pltpu.VMEM((gb, tw, 2 * S5_STATE), BF16),
            pltpu.VMEM((S5_CHUNK, nc, LANE), F32),
            pltpu.VMEM((S5_CHUNK, LANE, nc), BF16),
            pltpu.VMEM((ns2, nc), F32),
            pltpu.VMEM((nc, ns2), F32),
            pltpu.VMEM((nc, ns2), F32),
            pltpu.VMEM((ns2, nc), BF16),
            pltpu.VMEM((S5_CHUNK, LANE, nc), F32),
            pltpu.VMEM((n, LANE), F32),
        ],
        compiler_params=pltpu.CompilerParams(
            dimension_semantics=("parallel",), vmem_limit_bytes=VMEM_LIMIT),
        name="s5_core",
    )(u, tab, c2, cb, b_re, b_im, rep, til, a16, dsk)


def _s5_weights(a_re, a_im, log_dt, b_re, b_im, c_re, c_im, d_skip):
    g, p = a_re.shape
    dt = jnp.exp(log_dt)[:, None]
    decay = jnp.exp(a_re * dt)
    ab_re = decay * jnp.cos(a_im * dt)
    ab_im = decay * jnp.sin(a_im * dt)
    denom = a_re * a_re + a_im * a_im
    num_re = ab_re - 1.0
    num_im = ab_im
    k_re = (num_re * a_re + num_im * a_im) / denom
    k_im = (num_im * a_re - num_re * a_im) / denom
    tab = jnp.concatenate([ab_re, ab_re, -ab_im, ab_im, ab_im, -ab_im, k_re, k_im, k_im, k_re], axis=1)
    r, i = ab_re, ab_im
    for _ in range(S5_CHUNK.bit_length() - 1):
        r, i = r * r - i * i, 2.0 * (r * i)
    gb = GROUPS_PER_BLOCK
    a16 = jnp.stack([r.reshape(g // gb, gb * p), i.reshape(g // gb, gb * p)], axis=1)
    c2 = jnp.concatenate([c_re, -c_im], axis=-1)
    cb = jnp.concatenate([-c_im, -c_re], axis=-1)
    eye = jnp.eye(S5_GROUP_WIDTH, dtype=BF16)
    rep = jnp.repeat(eye, S5_GROUP_WIDTH, axis=1)
    til = jnp.tile(eye, (1, S5_CHUNK))
    return (tab, c2, cb, b_re, b_im, rep, til, a16, d_skip.reshape(1, -1))


def _merge_kernel(x_ref, h_ref, yap_ref, yb_ref, wglu_ref, bglu_ref, wga_ref, wgb_ref,
                  bga_ref, bgb_ref, wa_ref, wb_ref, *rest, n1, n2):
    wo_refs = rest[:n2]
    o_ref, ya_ref, m_ref = rest[n2:]
    j = pl.program_id(1)

    def gated_tile(with_glu):
        if with_glu:
            yp = yap_ref[...]
            z = _dot(yp, wglu_ref[...]) + bglu_ref[...]
            ya_ref[...] = (yp.astype(F32) * jax.nn.sigmoid(z)).astype(BF16)
        h = h_ref[...]
        ga = jax.nn.sigmoid(_dot(h, wga_ref[...].astype(BF16)) + bga_ref[...])
        gb = jax.nn.sigmoid(_dot(h, wgb_ref[...].astype(BF16)) + bgb_ref[...])
        pa = _dot(ya_ref[...], wa_ref[...].astype(BF16))
        pb = _dot(yb_ref[...], wb_ref[...].astype(BF16))
        m_ref[j] = (ga * pa + gb * pb).astype(BF16)

    pl.when(j == 0)(functools.partial(gated_tile, True))
    pl.when(jnp.logical_and(j > 0, j < n1))(functools.partial(gated_tile, False))

    for k in range(n2):
        @pl.when(j == n1 + k)
        def _(k=k):
            m = jnp.concatenate([m_ref[q] for q in range(n1)], axis=1)
            o_ref[...] = x_ref[...] + _dot(m, wo_refs[k][...])


def _merge(x, h, yap, yb, wglu, bglu, wgate, bgate, wa, wb, wo, *, tm=1024, t1=256, t2=512):
    n, d = x.shape
    sw = yap.shape[1]
    n1, n2 = d // t1, d // t2
    nt = n // tm
    second = lambda j: jnp.maximum(j - n1, 0)

    def first(j, back):
        return jnp.where(j < n1, j, jnp.where(j < n1 + back, n1 - 1, 0))

    once = pl.Buffered(1)
    return pl.pallas_call(
        functools.partial(_merge_kernel, n1=n1, n2=n2),
        grid=(nt, n1 + n2),
        in_specs=[
            pl.BlockSpec((tm, t2), lambda i, j: (i, second(j))),
            pl.BlockSpec((tm, d), lambda i, j: (_next_tile_after(i, j, n1, nt), 0)),
            pl.BlockSpec((tm, sw), lambda i, j: (_next_tile_after(i, j, 1, nt), 0)),
            pl.BlockSpec((tm, sw), lambda i, j: (_next_tile_after(i, j, n1, nt), 0)),
            pl.BlockSpec((sw, sw), lambda i, j: (0, 0), pipeline_mode=once),
            pl.BlockSpec((1, sw), lambda i, j: (0, 0)),
            pl.BlockSpec((d, t1), lambda i, j: (0, first(j, 1))),
            pl.BlockSpec((d, t1), lambda i, j: (0, first(j, 2) + n1)),
            pl.BlockSpec((1, t1), lambda i, j: (0, first(j, 1))),
            pl.BlockSpec((1, t1), lambda i, j: (0, first(j, 2) + n1)),
            pl.BlockSpec((sw, t1), lambda i, j: (0, first(j, 3))),
            pl.BlockSpec((sw, t1), lambda i, j: (0, first(j, 3))),
        ] + [pl.BlockSpec((d, t2), lambda i, j, k=k: (0, k), pipeline_mode=once) for k in range(n2)],
        out_specs=pl.BlockSpec((tm, t2), lambda i, j: (i, second(j))),
        out_shape=jax.ShapeDtypeStruct((n, d), F32),
        scratch_shapes=[pltpu.VMEM((tm, sw), BF16), pltpu.VMEM((n1, tm, t1), BF16)],
        compiler_params=pltpu.CompilerParams(
            dimension_semantics=("parallel", "arbitrary"), vmem_limit_bytes=VMEM_LIMIT),
        name="merge",
    )(x, h, yap, yb, wglu, bglu, wgate, wgate, bgate, bgate, wa, wb, *([wo] * n2))


def kernel(x, ffn1_norm, ffn1_w_gate, ffn1_w_up, ffn1_w_down, mix_norm, w_in, s5_a_re, s5_a_im, s5_log_dt, s5_b_re, s5_b_im, s5_c_re, s5_c_im, s5_d, s5_w_glu, s5_b_glu, sgu_ln_g, sgu_ln_b, sgu_w_s, sgu_b_s, w_branch_a, w_branch_b, w_gate, b_gate, w_out, ffn2_norm, ffn2_w_gate, ffn2_w_up, ffn2_w_down, final_norm):
    bsz, seq, d = x.shape
    depth = ffn1_norm.shape[0]
    sw = s5_w_glu.shape[1]
    assert bsz == 1 and sw == w_branch_b.shape[1] and sw % LANE == 0
    assert seq % (S5_CHUNK * LANE) == 0 and sgu_w_s.shape[2] == SGU_CHUNK
    bf = lambda w: w.astype(BF16)
    row = lambda v: v.reshape(1, -1)
    xs = x.reshape(seq, d)
    fin = row(final_norm)
    for i in range(depth):
        xs, h = _ffn(xs, row(ffn1_norm[i]), ffn1_w_gate[i], ffn1_w_up[i], ffn1_w_down[i],
                     row(mix_norm[i]), final=False)
        bsb = jnp.repeat(jnp.transpose(sgu_b_s[i]), sw // SGU_HEADS, axis=1)
        ua, yb = _mix_in(h, bf(w_in[i]), row(sgu_ln_g[i]), row(sgu_ln_b[i]), sgu_w_s[i], bsb, sw=sw)
        yap = _s5_core(ua, *_s5_weights(s5_a_re[i], s5_a_im[i], s5_log_dt[i], s5_b_re[i], s5_b_im[i],
                                        s5_c_re[i], s5_c_im[i], s5_d[i]))
        xs = _merge(xs, h, yap, yb, bf(s5_w_glu[i]), row(s5_b_glu[i]), w_gate[i], row(b_gate[i]),
                    w_branch_a[i], w_branch_b[i], bf(w_out[i]))
        if i == depth - 1:
            xs = _ffn(xs, row(ffn2_norm[i]), ffn2_w_gate[i], ffn2_w_up[i], ffn2_w_down[i], fin, final=True)
        else:
            xs, _ = _ffn(xs, row(ffn2_norm[i]), ffn2_w_gate[i], ffn2_w_up[i], ffn2_w_down[i], fin,
                         final=False)
    return xs.reshape(bsz, seq, d)
```

```python
import functools

import jax
import jax.numpy as jnp
from jax import lax
from jax.experimental import pallas as pl
from jax.experimental.pallas import tpu as pltpu

F32 = jnp.float32
BF16 = jnp.bfloat16

NORM_EPS = 1e-6
S5_GROUP_WIDTH = 16
S5_STATE = 64
SGU_HEADS = 8
SGU_CHUNK = 128
SGU_CAUSAL_BLOCK = 64
S5_CHUNK = 16
LANE = 128
GROUPS_PER_BLOCK = LANE // S5_GROUP_WIDTH
VMEM_LIMIT = 60 * 1024 * 1024
ROW_CHUNK = 512
LN_ROW_BLOCKS = 4


def _rms(x, g):
    ms = jnp.mean(x * x, axis=-1, keepdims=True)
    return (x * lax.rsqrt(ms + NORM_EPS)) * g


def _dot(a, b):
    return jnp.dot(a, b, preferred_element_type=F32)


def _ffn_kernel(x_ref, g_ref, wg_ref, wu_ref, wd_ref, post_ref, o_ref, *rest, final):
    h_ref = rest[0]
    j = pl.program_id(1)
    last = pl.num_programs(1) - 1
    n_chunks = x_ref.shape[0] // ROW_CHUNK

    @pl.when(j == 0)
    def _():
        g = g_ref[...]

        def rows(r, carry):
            rs = pl.ds(pl.multiple_of(r * ROW_CHUNK, ROW_CHUNK), ROW_CHUNK)
            x = x_ref[rs, :]
            h_ref[rs, :] = _rms(x, g).astype(BF16)
            o_ref[rs, :] = x
            return carry

        lax.fori_loop(0, n_chunks, rows, 0, unroll=2)

    def accumulate():
        h = h_ref[...]
        a = _dot(h, wg_ref[...].astype(BF16))
        b = _dot(h, wu_ref[...].astype(BF16))
        hid = (a * jax.nn.sigmoid(a)) * b
        o_ref[...] += _dot(hid.astype(BF16), (0.5 * wd_ref[...]).astype(BF16))

    pl.when(j < last)(accumulate)

    @pl.when(j == last)
    def _():
        accumulate()
        y = _rms(o_ref[...], post_ref[...])
        if final:
            o_ref[...] = y
        else:
            rest[0][...] = y.astype(BF16)


def _next_tile_after(i, j, first_step, n_tiles):
    return jnp.minimum(i + jnp.where(j >= first_step, 1, 0), n_tiles - 1)


def _ffn(x, g, wg, wu, wd, post, *, final, tm=1024, tf=256):
    n, d = x.shape
    dff = wg.shape[1]
    nt = n // tm
    row_tile = pl.BlockSpec((tm, d), lambda i, j: (i, 0))
    if final:
        out_specs, out_shape = row_tile, jax.ShapeDtypeStruct((n, d), F32)
    else:
        out_specs = [row_tile, row_tile]
        out_shape = [jax.ShapeDtypeStruct((n, d), F32), jax.ShapeDtypeStruct((n, d), BF16)]
    return pl.pallas_call(
        functools.partial(_ffn_kernel, final=final),
        grid=(nt, dff // tf),
        in_specs=[
            row_tile,
            pl.BlockSpec((1, d), lambda i, j: (0, 0)),
            pl.BlockSpec((d, tf), lambda i, j: (0, j)),
            pl.BlockSpec((d, tf), lambda i, j: (0, j)),
            pl.BlockSpec((tf, d), lambda i, j: (j, 0)),
            pl.BlockSpec((1, d), lambda i, j: (0, 0)),
        ],
        out_specs=out_specs,
        out_shape=out_shape,
        scratch_shapes=[pltpu.VMEM((tm, d), BF16)] if final else [],
        compiler_params=pltpu.CompilerParams(
            dimension_semantics=("parallel", "arbitrary"), vmem_limit_bytes=VMEM_LIMIT),
        name="ffn_final" if final else "ffn",
    )(x, g, wg, wu, wd, post)


def _mix_in_kernel(h_ref, w_ref, lng_ref, lnb_ref, ws_ref, bs_ref, ua_ref, yb_ref, vn_ref, *, sw):
    j = pl.program_id(1)

    @pl.when(j == 0)
    def _():
        ua_ref[...] = _dot(h_ref[...], w_ref[...])

    @pl.when(j == 1)
    def _():
        rb = vn_ref.shape[0] // LN_ROW_BLOCKS
        for k in range(LN_ROW_BLOCKS):
            rows = slice(k * rb, (k + 1) * rb)
            v = jax.nn.gelu(_dot(h_ref[rows, :], w_ref[...]))
            mu = jnp.mean(v, axis=-1, keepdims=True)
            vc = v - mu
            var = jnp.mean(vc * vc, axis=-1, keepdims=True)
            vn_ref[rows, :] = ((vc * lax.rsqrt(var + NORM_EPS)) * lng_ref[...] + lnb_ref[...]).astype(BF16)

    @pl.when(j == 2)
    def _():
        tm = vn_ref.shape[0]
        u = jax.nn.gelu(_dot(h_ref[...], w_ref[...]))
        r = lax.broadcasted_iota(jnp.int32, (SGU_CHUNK, SGU_CHUNK), 0) // SGU_CAUSAL_BLOCK
        c = lax.broadcasted_iota(jnp.int32, (SGU_CHUNK, SGU_CHUNK), 1) // SGU_CAUSAL_BLOCK
        keep = r >= c
        hd = sw // SGU_HEADS
        for h in range(SGU_HEADS):
            wsm = jnp.where(keep, ws_ref[h], 0.0).astype(BF16)
            cols = slice(h * hd, (h + 1) * hd)
            for q in range(0, tm // SGU_CHUNK, 2):
                ra = slice(q * SGU_CHUNK, (q + 1) * SGU_CHUNK)
                rb = slice((q + 1) * SGU_CHUNK, (q + 2) * SGU_CHUNK)
                both = _dot(wsm, jnp.concatenate([vn_ref[ra, cols], vn_ref[rb, cols]], axis=1))
                yb_ref[ra, cols] = (u[ra, cols] * (both[:, :hd] + bs_ref[:, cols])).astype(BF16)
                yb_ref[rb, cols] = (u[rb, cols] * (both[:, hd:] + bs_ref[:, cols])).astype(BF16)


def _mix_in(h, w_in, ln_g, ln_b, w_s, bsb, *, sw, tm=1024):
    n, d = h.shape
    assert w_in.shape[1] == 3 * sw
    nt = n // tm
    return pl.pallas_call(
        functools.partial(_mix_in_kernel, sw=sw),
        grid=(nt, 3),
        in_specs=[
            pl.BlockSpec((tm, d), lambda i, j: (i, 0)),
            pl.BlockSpec((d, sw), lambda i, j: (0, jnp.where(j == 0, 0, 3 - j))),
            pl.BlockSpec((1, sw), lambda i, j: (0, 0)),
            pl.BlockSpec((1, sw), lambda i, j: (0, 0)),
            pl.BlockSpec(w_s.shape, lambda i, j: (0, 0, 0)),
            pl.BlockSpec(bsb.shape, lambda i, j: (0, 0)),
        ],
        out_specs=[
            pl.BlockSpec((tm, sw), lambda i, j: (i, 0)),
            pl.BlockSpec((tm, sw), lambda i, j: (i, 0)),
        ],
        out_shape=[
            jax.ShapeDtypeStruct((n, sw), F32),
            jax.ShapeDtypeStruct((n, sw), BF16),
        ],
        scratch_shapes=[pltpu.VMEM((tm, sw), BF16)],
        compiler_params=pltpu.CompilerParams(
            dimension_semantics=("parallel", "arbitrary"), vmem_limit_bytes=VMEM_LIMIT),
        name="mix_in",
    )(h, w_in, ln_g, ln_b, w_s, bsb)


def _dot_split(a, b):
    a_hi = a.astype(BF16)
    b_hi = b.astype(BF16)
    a_lo = (a - a_hi.astype(F32)).astype(BF16)
    b_lo = (b - b_hi.astype(F32)).astype(BF16)
    return _dot(a_hi, b_hi) + (_dot(a_hi, b_lo) + _dot(a_lo, b_hi))


def _expand(x, sel):
    x1 = x.astype(BF16)
    r1 = x - x1.astype(F32)
    x2 = r1.astype(BF16)
    x3 = (r1 - x2.astype(F32)).astype(BF16)
    return _dot(x1, sel) + (_dot(x2, sel) + _dot(x3, sel))


def _expand_t(x, sel):
    tn = (((0,), (0,)), ((), ()))
    x1 = x.astype(BF16)
    r1 = x - x1.astype(F32)
    x2 = r1.astype(BF16)
    x3 = (r1 - x2.astype(F32)).astype(BF16)
    dot_t = lambda v: lax.dot_general(v, sel, tn, preferred_element_type=F32)
    return dot_t(x1) + (dot_t(x2) + dot_t(x3))


def _s5_kernel(u_ref, tab_ref, c2_ref, cb_ref, bre_ref, bim_ref, rep_ref, til_ref, a_ref, d_ref, o_ref,
               dn_ref, upr_ref, upi_ref, wt_ref, wb_ref, wc_ref,
               ut_ref, vt_ref, v_ref, xs_ref, xt_ref, yt_ref, yn_ref):
    nc = u_ref.shape[0] // S5_CHUNK
    gw = S5_GROUP_WIDTH
    gb = GROUPS_PER_BLOCK
    tw = S5_CHUNK * gw
    ns = gb * S5_STATE
    sl = 2 * S5_STATE

    tab = tab_ref[...]
    ar2, an, ap = tab[:, 0:sl], tab[:, sl:2 * sl], tab[:, 2 * sl:3 * sl]
    q, qs = tab[:, 3 * sl:4 * sl], tab[:, 4 * sl:5 * sl]
    lo = lax.broadcasted_iota(jnp.int32, (gb, sl), 1) < S5_STATE
    x = jnp.where(lo, 1.0, 0.0)
    xs = jnp.where(lo, 0.0, 1.0)
    dn_ref[pl.ds((S5_CHUNK - 1) * gb, gb), :] = q
    for j in range(1, S5_CHUNK + 1):
        x, xs = x * ar2 + xs * an, xs * ar2 + x * ap
        upr_ref[pl.ds((j - 1) * gb, gb), :] = jnp.where(lo, x, xs)
        upi_ref[pl.ds((j - 1) * gb, gb), :] = jnp.where(lo, xs, x)
        if j < S5_CHUNK:
            q, qs = q * ar2 + qs * an, qs * ar2 + q * ap
            dn_ref[pl.ds((S5_CHUNK - 1 - j) * gb, gb), :] = q

    lane = lax.broadcasted_iota(jnp.int32, (gw, tw), 1)
    for g in range(gb):
        pw = _expand_t(dn_ref[pl.ds(g, S5_CHUNK, stride=gb), :], rep_ref[...])
        br = _expand(bre_ref[g], til_ref[...])
        bi = _expand(bim_ref[g], til_ref[...])
        pr, pi = pw[:S5_STATE], pw[S5_STATE:]
        wbf = jnp.concatenate([pr * br - pi * bi, pr * bi + pi * br], axis=0)
        wb_ref[g] = wbf.astype(BF16)
        hrow = _dot_split(c2_ref[g], wbf)
        ca, cb = c2_ref[g], cb_ref[g]
        for t in range(S5_CHUNK):
            shift = (gw * (t + 1)) % tw
            rolled = pltpu.roll(hrow, shift, axis=1) if shift else hrow
            wt_ref[g, t * gw:(t + 1) * gw, :] = jnp.where(lane < gw * (t + 1), rolled, 0.0).astype(BF16)
            row = t * gb + g
            wc_ref[g, t * gw:(t + 1) * gw, :] = (
                ca * upr_ref[row:row + 1, :] + cb * upi_ref[row:row + 1, :]).astype(BF16)

    for s in range(S5_CHUNK):
        piece = u_ref[pl.ds(s, nc, stride=S5_CHUNK), :]
        pt = piece.T
        ut_ref[s] = pt.astype(BF16)
        yt_ref[s] = d_ref[...] * pt

    def chunk_inputs(g):
        return jnp.concatenate(
            [ut_ref[s, g * gw:(g + 1) * gw, :] for s in range(S5_CHUNK)], axis=0)

    for g in range(GROUPS_PER_BLOCK):
        ug = chunk_inputs(g)
        vg = _dot(wb_ref[g], ug)
        vt_ref[g * S5_STATE:(g + 1) * S5_STATE, :] = vg[:S5_STATE]
        vt_ref[ns + g * S5_STATE:ns + (g + 1) * S5_STATE, :] = vg[S5_STATE:]
        yg = _dot(wt_ref[g], ug)
        for t in range(S5_CHUNK):
            yt_ref[t, g * gw:(g + 1) * gw, :] += yg[t * gw:(t + 1) * gw, :]
    v_ref[...] = vt_ref[...].T

    ar = a_ref[0:1, :]
    ai = a_ref[1:2, :]

    def step(c, carry):
        xr, xi = carry
        xs_ref[pl.ds(c, 1), 0:ns] = xr
        xs_ref[pl.ds(c, 1), ns:2 * ns] = xi
        vr = v_ref[pl.ds(c, 1), 0:ns]
        vi = v_ref[pl.ds(c, 1), ns:2 * ns]
        return ar * xr - ai * xi + vr, ar * xi + ai * xr + vi

    zero = jnp.zeros((1, ns), F32)
    lax.fori_loop(0, nc, step, (zero, zero), unroll=True)
    xt_ref[...] = xs_ref[...].T.astype(BF16)

    for g in range(GROUPS_PER_BLOCK):
        xg = jnp.concatenate(
            [xt_ref[g * S5_STATE:(g + 1) * S5_STATE, :],
             xt_ref[ns + g * S5_STATE:ns + (g + 1) * S5_STATE, :]], axis=0)
        yg = _dot(wc_ref[g], xg)
        for t in range(S5_CHUNK):
            yt_ref[t, g * gw:(g + 1) * gw, :] += yg[t * gw:(t + 1) * gw, :]

    for t in range(S5_CHUNK):
        yn_ref[pl.ds(t, nc, stride=S5_CHUNK), :] = jax.nn.gelu(yt_ref[t].T)
    o_ref[...] = yn_ref[...].astype(BF16)


def _s5_core(u, tab, c2, cb, b_re, b_im, rep, til, a16, dsk):
    n, sw = u.shape
    nb = sw // LANE
    nc = n // S5_CHUNK
    gb = GROUPS_PER_BLOCK
    ns2 = 2 * gb * S5_STATE
    tw = S5_CHUNK * S5_GROUP_WIDTH
    per_group = lambda a: pl.BlockSpec((gb,) + a.shape[1:], lambda j: (j,) + (0,) * (a.ndim - 1))
    whole = lambda a: pl.BlockSpec(a.shape, lambda j: (0, 0))
    table_rows = pltpu.VMEM((S5_CHUNK * gb, 2 * S5_STATE), F32)
    return pl.pallas_call(
        _s5_kernel,
        grid=(nb,),
        in_specs=[
            pl.BlockSpec((n, LANE), lambda j: (0, j)),
            per_group(tab), per_group(c2), per_group(cb), per_group(b_re), per_group(b_im),
            whole(rep), whole(til),
            pl.BlockSpec((None, 2, ns2 // 2), lambda j: (j, 0, 0)),
            pl.BlockSpec((LANE, 1), lambda j: (j, 0)),
        ],
        out_specs=pl.BlockSpec((n, LANE), lambda j: (0, j)),
        out_shape=jax.ShapeDtypeStruct((n, sw), BF16),
        scratch_shapes=[
            table_rows, table_rows, table_rows,
            pltpu.VMEM((gb, tw, tw), BF16),
            pltpu.VMEM((gb, 2 * S5_STATE, tw), BF16),
            pltpu.VMEM((gb, tw, 2 * S5_STATE), BF16),
            pltpu.VMEM((S5_CHUNK, LANE, nc), BF16),
            pltpu.VMEM((ns2, nc), F32),
            pltpu.VMEM((nc, ns2), F32),
            pltpu.VMEM((nc, ns2), F32),
            pltpu.VMEM((ns2, nc), BF16),
            pltpu.VMEM((S5_CHUNK, LANE, nc), F32),
            pltpu.VMEM((n, LANE), F32),
        ],
        compiler_params=pltpu.CompilerParams(
            dimension_semantics=("parallel",), vmem_limit_bytes=VMEM_LIMIT),
        name="s5_core",
    )(u, tab, c2, cb, b_re, b_im, rep, til, a16, dsk)


def _s5_weights(a_re, a_im, log_dt, b_re, b_im, c_re, c_im, d_skip):
    g, p = a_re.shape
    dt = jnp.exp(log_dt)[:, None]
    decay = jnp.exp(a_re * dt)
    ab_re = decay * jnp.cos(a_im * dt)
    ab_im = decay * jnp.sin(a_im * dt)
    denom = a_re * a_re + a_im * a_im
    num_re = ab_re - 1.0
    num_im = ab_im
    k_re = (num_re * a_re + num_im * a_im) / denom
    k_im = (num_im * a_re - num_re * a_im) / denom
    tab = jnp.concatenate([ab_re, ab_re, -ab_im, ab_im, ab_im, -ab_im, k_re, k_im, k_im, k_re], axis=1)
    r, i = ab_re, ab_im
    for _ in range(S5_CHUNK.bit_length() - 1):
        r, i = r * r - i * i, 2.0 * (r * i)
    gb = GROUPS_PER_BLOCK
    a16 = jnp.stack([r.reshape(g // gb, gb * p), i.reshape(g // gb, gb * p)], axis=1)
    c2 = jnp.concatenate([c_re, -c_im], axis=-1)
    cb = jnp.concatenate([-c_im, -c_re], axis=-1)
    eye = jnp.eye(S5_GROUP_WIDTH, dtype=BF16)
    rep = jnp.repeat(eye, S5_GROUP_WIDTH, axis=1)
    til = jnp.tile(eye, (1, S5_CHUNK))
    return (tab, c2, cb, b_re, b_im, rep, til, a16, d_skip.reshape(-1, 1))


def _merge_kernel(x_ref, h_ref, yap_ref, yb_ref, wglu_ref, bglu_ref, wga_ref, wgb_ref,
                  bga_ref, bgb_ref, wa_ref, wb_ref, *rest, n1, n2):
    wo_refs = rest[:n2]
    o_ref, ya_ref, m_ref = rest[n2:]
    j = pl.program_id(1)

    def gated_tile(with_glu):
        if with_glu:
            yp = yap_ref[...]
            z = _dot(yp, wglu_ref[...]) + bglu_ref[...]
            ya_ref[...] = (yp.astype(F32) * jax.nn.sigmoid(z)).astype(BF16)
        h = h_ref[...]
        ga = jax.nn.sigmoid(_dot(h, wga_ref[...].astype(BF16)) + bga_ref[...])
        gb = jax.nn.sigmoid(_dot(h, wgb_ref[...].astype(BF16)) + bgb_ref[...])
        pa = _dot(ya_ref[...], wa_ref[...].astype(BF16))
        pb = _dot(yb_ref[...], wb_ref[...].astype(BF16))
        m_ref[j] = (ga * pa + gb * pb).astype(BF16)

    pl.when(j == 0)(functools.partial(gated_tile, True))
    pl.when(jnp.logical_and(j > 0, j < n1))(functools.partial(gated_tile, False))

    for k in range(n2):
        @pl.when(j == n1 + k)
        def _(k=k):
            m = jnp.concatenate([m_ref[q] for q in range(n1)], axis=1)
            o_ref[...] = x_ref[...] + _dot(m, wo_refs[k][...])


def _merge(x, h, yap, yb, wglu, bglu, wgate, bgate, wa, wb, wo, *, tm=1024, t1=256, t2=512):
    n, d = x.shape
    sw = yap.shape[1]
    n1, n2 = d // t1, d // t2
    nt = n // tm
    second = lambda j: jnp.maximum(j - n1, 0)

    def first(j, back):
        return jnp.where(j < n1, j, jnp.where(j < n1 + back, n1 - 1, 0))

    once = pl.Buffered(1)
    return pl.pallas_call(
        functools.partial(_merge_kernel, n1=n1, n2=n2),
        grid=(nt, n1 + n2),
        in_specs=[
            pl.BlockSpec((tm, t2), lambda i, j: (i, second(j))),
            pl.BlockSpec((tm, d), lambda i, j: (_next_tile_after(i, j, n1, nt), 0)),
            pl.BlockSpec((tm, sw), lambda i, j: (_next_tile_after(i, j, 1, nt), 0)),
            pl.BlockSpec((tm, sw), lambda i, j: (_next_tile_after(i, j, n1, nt), 0)),
            pl.BlockSpec((sw, sw), lambda i, j: (0, 0), pipeline_mode=once),
            pl.BlockSpec((1, sw), lambda i, j: (0, 0)),
            pl.BlockSpec((d, t1), lambda i, j: (0, first(j, 1))),
            pl.BlockSpec((d, t1), lambda i, j: (0, first(j, 2) + n1)),
            pl.BlockSpec((1, t1), lambda i, j: (0, first(j, 1))),
            pl.BlockSpec((1, t1), lambda i, j: (0, first(j, 2) + n1)),
            pl.BlockSpec((sw, t1), lambda i, j: (0, first(j, 3))),
            pl.BlockSpec((sw, t1), lambda i, j: (0, first(j, 3))),
        ] + [pl.BlockSpec((d, t2), lambda i, j, k=k: (0, k), pipeline_mode=once) for k in range(n2)],
        out_specs=pl.BlockSpec((tm, t2), lambda i, j: (i, second(j))),
        out_shape=jax.ShapeDtypeStruct((n, d), F32),
        scratch_shapes=[pltpu.VMEM((tm, sw), BF16), pltpu.VMEM((n1, tm, t1), BF16)],
        compiler_params=pltpu.CompilerParams(
            dimension_semantics=("parallel", "arbitrary"), vmem_limit_bytes=VMEM_LIMIT),
        name="merge",
    )(x, h, yap, yb, wglu, bglu, wgate, wgate, bgate, bgate, wa, wb, *([wo] * n2))


def kernel(x, ffn1_norm, ffn1_w_gate, ffn1_w_up, ffn1_w_down, mix_norm, w_in, s5_a_re, s5_a_im, s5_log_dt, s5_b_re, s5_b_im, s5_c_re, s5_c_im, s5_d, s5_w_glu, s5_b_glu, sgu_ln_g, sgu_ln_b, sgu_w_s, sgu_b_s, w_branch_a, w_branch_b, w_gate, b_gate, w_out, ffn2_norm, ffn2_w_gate, ffn2_w_up, ffn2_w_down, final_norm):
    bsz, seq, d = x.shape
    depth = ffn1_norm.shape[0]
    sw = s5_w_glu.shape[1]
    assert bsz == 1 and sw == w_branch_b.shape[1] and sw % LANE == 0
    assert seq % (S5_CHUNK * LANE) == 0 and sgu_w_s.shape[2] == SGU_CHUNK
    bf = lambda w: w.astype(BF16)
    row = lambda v: v.reshape(1, -1)
    xs = x.reshape(seq, d)
    fin = row(final_norm)
    for i in range(depth):
        xs, h = _ffn(xs, row(ffn1_norm[i]), ffn1_w_gate[i], ffn1_w_up[i], ffn1_w_down[i],
                     row(mix_norm[i]), final=False)
        bsb = jnp.repeat(jnp.transpose(sgu_b_s[i]), sw // SGU_HEADS, axis=1)
        ua, yb = _mix_in(h, bf(w_in[i]), row(sgu_ln_g[i]), row(sgu_ln_b[i]), sgu_w_s[i], bsb, sw=sw)
        yap = _s5_core(ua, *_s5_weights(s5_a_re[i], s5_a_im[i], s5_log_dt[i], s5_b_re[i], s5_b_im[i],
                                        s5_c_re[i], s5_c_im[i], s5_d[i]))
        xs = _merge(xs, h, yap, yb, bf(s5_w_glu[i]), row(s5_b_glu[i]), w_gate[i], row(b_gate[i]),
                    w_branch_a[i], w_branch_b[i], bf(w_out[i]))
        if i == depth - 1:
            xs = _ffn(xs, row(ffn2_norm[i]), ffn2_w_gate[i], ffn2_w_up[i], ffn2_w_down[i], fin, final=True)
        else:
            xs, _ = _ffn(xs, row(ffn2_norm[i]), ffn2_w_gate[i], ffn2_w_up[i], ffn2_w_down[i], fin,
                         final=False)
    return xs.reshape(bsz, seq, d)
```
